```python
import math
import jax, jax.numpy as jnp
from jax import lax
import numpy as np

D_MODEL = 1024
BATCH = 4
SEQ = 8192
DEPTH = 2
DEC_BATCH = 16
DEC_SEQ = 64
PAST_LEN = 1024

CHUNK = 64
Q_BLOCK = 128
N_MIXERS = 2
N_DIFF_LAYERS = (DEPTH + 1) // 2
N_DSA_LAYERS = DEPTH // 2
DIFF_HEADS = 8
DIFF_HEAD_DIM = 64
DIFF_QK = DIFF_HEADS * 2 * DIFF_HEAD_DIM
DIFF_IN = 2 * DIFF_QK + DIFF_HEADS * 2 * DIFF_HEAD_DIM
DSA_HEADS = 16
DSA_KV_HEADS = 4
DSA_HEAD_DIM = 64
DSA_GROUP = DSA_HEADS // DSA_KV_HEADS
IDX_HEADS = 8
IDX_DIM = 64
IDX_TOPK = 256
DSA_Q = DSA_HEADS * DSA_HEAD_DIM
DSA_KV = DSA_KV_HEADS * DSA_HEAD_DIM
DSA_IN = DSA_Q + 2 * DSA_KV + IDX_HEADS * IDX_DIM + IDX_DIM + IDX_HEADS
REL_BUCKETS = 32
REL_MAX_DIST = 128
BIAS_HEADS = 16
N_EXPERTS = 16
N_GROUPS = 4
EXPERTS_PER_GROUP = N_EXPERTS // N_GROUPS
MOE_TOPK = 2
EXPERT_FF = 512
RMS_EPS = 1e-6

kernel_name = 'hybrid_diff_dsa_moe_stream_step'


def rms_norm(x, g):
    x32 = x.astype(jnp.float32)
    y = x32 * lax.rsqrt(jnp.mean(x32 * x32, axis=-1, keepdims=True) + RMS_EPS)
    return (y * g.astype(jnp.float32)).astype(x.dtype)


def ada_mod(c, w, b):
    m = jax.nn.silu(c) @ w + b
    return jnp.split(m[:, None, :], 6, axis=-1)


def modulate(h, shift, scale):
    return h * (1.0 + scale) + shift


def rel_bucket(rel):
    half = REL_BUCKETS // 2
    max_exact = half // 2
    ret = jnp.where(rel > 0, half, 0)
    n = jnp.abs(rel)
    n_f = jnp.maximum(n, 1).astype(jnp.float32)
    large = max_exact + (jnp.log(n_f / max_exact) / math.log(REL_MAX_DIST / max_exact)
                         * (half - max_exact)).astype(jnp.int32)
    large = jnp.minimum(large, half - 1)
    return ret + jnp.where(n < max_exact, n, large)


def chunk_visible(q_pos, k_pos):
    return (k_pos[None, :] // CHUNK) <= (q_pos[:, None] // CHUNK)


def over_query_blocks(fn, q_args, q_pos):
    S = q_pos.shape[0]
    if S <= Q_BLOCK:
        return fn(q_args, q_pos)
    nb = S // Q_BLOCK
    blk = tuple(jnp.moveaxis(a.reshape(a.shape[0], nb, Q_BLOCK, *a.shape[2:]), 1, 0) for a in q_args)
    out = lax.map(lambda xs: fn(xs[0], xs[1]), (blk, q_pos.reshape(nb, Q_BLOCK)))
    out = jnp.moveaxis(out, 0, 1)
    return out.reshape(out.shape[0], S, *out.shape[3:])


def diff_attend(q, k, v, q_pos, k_pos, rel_bias, lam):
    Q, L = q.shape[1], k.shape[1]
    logits = jnp.einsum('bqhmd,bkhmd->bhmqk', q, k).astype(jnp.float32) * (DIFF_HEAD_DIM ** -0.5)
    bias = rel_bias[rel_bucket(k_pos[None, :] - q_pos[:, None])]
    bias = jnp.transpose(bias, (2, 0, 1)).reshape(DIFF_HEADS, 2, Q, L).astype(jnp.float32)
    logits = jnp.where(chunk_visible(q_pos, k_pos), logits + bias, -jnp.inf)
    p = jax.nn.softmax(logits, axis=-1)
    p_diff = (p[:, :, 0] - lam * p[:, :, 1]).astype(v.dtype)
    return jnp.einsum('bhqk,bkhe->bqhe', p_diff, v)


def diff_mixer(h, past_k, past_v, q_pos, k_pos, w_in, w_out, lam_p, subln_g, rel_bias, lam_init):
    B, S, _ = h.shape
    proj = h @ w_in
    q = proj[..., :DIFF_QK].reshape(B, S, DIFF_HEADS, 2, DIFF_HEAD_DIM)
    k = proj[..., DIFF_QK:2 * DIFF_QK].reshape(B, S, DIFF_HEADS, 2, DIFF_HEAD_DIM)
    v = proj[..., 2 * DIFF_QK:].reshape(B, S, DIFF_HEADS, 2 * DIFF_HEAD_DIM)
    k_all = k if past_k is None else jnp.concatenate([past_k, k], axis=1)
    v_all = v if past_v is None else jnp.concatenate([past_v, v], axis=1)
    lp = lam_p.astype(jnp.float32)
    lam = jnp.exp(jnp.sum(lp[0] * lp[1])) - jnp.exp(jnp.sum(lp[2] * lp[3])) + lam_init
    o = over_query_blocks(lambda qa, qp: diff_attend(qa[0], k_all, v_all, qp, k_pos, rel_bias, lam),
                          (q,), q_pos)
    o = rms_norm(o, subln_g) * (1.0 - lam_init)
    return o.reshape(B, S, D_MODEL) @ w_out, k, v


def dsa_attend(q, qi, wi, q_pos, k, v, ki, k_pos, rel_bias, topk):
    B, Q = q.shape[:2]
    s = jnp.einsum('bqhd,bkd->bqhk', qi, ki)
    score = jnp.einsum('bqh,bqhk->bqk', wi, jax.nn.relu(s)).astype(jnp.float32)
    score = score * (IDX_DIM ** -0.5 * IDX_HEADS ** -0.5)
    score = jnp.where(chunk_visible(q_pos, k_pos), score, -jnp.inf)
    top_val, idx = lax.top_k(score, topk)
    valid = top_val > -jnp.inf
    gather = jax.vmap(lambda rows, ids: rows[ids])
    k_sel = gather(k, idx)
    v_sel = gather(v, idx)
    qg = q.reshape(B, Q, DSA_KV_HEADS, DSA_GROUP, DSA_HEAD_DIM)
    logits = jnp.einsum('bqcgd,bqjcd->bqcgj', qg, k_sel).astype(jnp.float32) * (DSA_HEAD_DIM ** -0.5)
    bias = rel_bias[rel_bucket(k_pos[idx] - q_pos[None, :, None])]
    bias = jnp.moveaxis(bias, 2, -1).reshape(B, Q, DSA_KV_HEADS, DSA_GROUP, topk).astype(jnp.float32)
    logits = jnp.where(valid[:, :, None, None, :], logits + bias, -jnp.inf)
    p = jax.nn.softmax(logits, axis=-1).astype(v.dtype)
    o = jnp.einsum('bqcgj,bqjcd->bqcgd', p, v_sel)
    return o.reshape(B, Q, DSA_HEADS, DSA_HEAD_DIM)


def dsa_mixer(h, past_k, past_v, past_ki, q_pos, k_pos, w_in, w_out, rel_bias):
    B, S, _ = h.shape
    proj = h @ w_in
    o1 = DSA_Q
    o2 = o1 + DSA_KV
    o3 = o2 + DSA_KV
    o4 = o3 + IDX_HEADS * IDX_DIM
    o5 = o4 + IDX_DIM
    q = proj[..., :o1].reshape(B, S, DSA_HEADS, DSA_HEAD_DIM)
    k = proj[..., o1:o2].reshape(B, S, DSA_KV_HEADS, DSA_HEAD_DIM)
    v = proj[..., o2:o3].reshape(B, S, DSA_KV_HEADS, DSA_HEAD_DIM)
    qi = proj[..., o3:o4].reshape(B, S, IDX_HEADS, IDX_DIM)
    ki = proj[..., o4:o5]
    wi = proj[..., o5:]
    k_all = k if past_k is None else jnp.concatenate([past_k, k], axis=1)
    v_all = v if past_v is None else jnp.concatenate([past_v, v], axis=1)
    ki_all = ki if past_ki is None else jnp.concatenate([past_ki, ki], axis=1)
    topk = min(IDX_TOPK, k_all.shape[1] // 4)
    o = over_query_blocks(
        lambda qa, qp: dsa_attend(qa[0], qa[1], qa[2], qp, k_all, v_all, ki_all, k_pos, rel_bias, topk),
        (q, qi, wi), q_pos)
    return o.reshape(B, S, D_MODEL) @ w_out, k, v, ki


def moe_ffn(h, w_router, router_bias, w_gate, w_up, w_down):
    B, S, D = h.shape
    t = h.reshape(B * S, D)
    probs = jax.nn.softmax((t @ w_router).astype(jnp.float32), axis=-1)
    sel = probs + router_bias.astype(jnp.float32)
    grp_top = lax.top_k(sel.reshape(-1, N_GROUPS, EXPERTS_PER_GROUP), MOE_TOPK)[0]
    best_group = jnp.argmax(jnp.sum(grp_top, axis=-1), axis=-1)
    in_group = (jnp.arange(N_EXPERTS) // EXPERTS_PER_GROUP)[None, :] == best_group[:, None]
    _, idx = lax.top_k(jnp.where(in_group, sel, -jnp.inf), MOE_TOPK)
    w = jnp.take_along_axis(probs, idx, axis=-1)
    w = w / jnp.sum(w, axis=-1, keepdims=True)
    gates = jnp.sum(jax.nn.one_hot(idx, N_EXPERTS, dtype=jnp.float32) * w[..., None], axis=1).astype(h.dtype)
    out = jnp.zeros_like(t)
    for e in range(N_EXPERTS):
        a = jax.nn.silu(t @ w_gate[e]) * (t @ w_up[e])
        out = out + gates[:, e:e + 1] * (a @ w_down[e])
    return out.reshape(B, S, D)


def setup_inputs(seed: int = 0) -> dict:
    key = jax.random.key(seed)
    ks = jax.random.split(key, 26)
    f32 = jnp.float32
    nrm = lambda k, shape, s: jax.random.normal(k, shape, f32) * s
    d_inv = D_MODEL ** -0.5
    return {
        'x_prompt': nrm(ks[0], (BATCH, SEQ, D_MODEL), 1.0),
        'x_sample': nrm(ks[1], (DEC_BATCH, DEC_SEQ, D_MODEL), 1.0),
        'cache_k_diff': nrm(ks[2], (N_DIFF_LAYERS, DEC_BATCH, PAST_LEN, DIFF_HEADS, 2, DIFF_HEAD_DIM), 1.0),
        'cache_v_diff': nrm(ks[3], (N_DIFF_LAYERS, DEC_BATCH, PAST_LEN, DIFF_HEADS, 2 * DIFF_HEAD_DIM), 1.0),
        'cache_k_dsa': nrm(ks[4], (N_DSA_LAYERS, DEC_BATCH, PAST_LEN, DSA_KV_HEADS, DSA_HEAD_DIM), 1.0),
        'cache_v_dsa': nrm(ks[5], (N_DSA_LAYERS, DEC_BATCH, PAST_LEN, DSA_KV_HEADS, DSA_HEAD_DIM), 1.0),
        'cache_kidx_dsa': nrm(ks[6], (N_DSA_LAYERS, DEC_BATCH, PAST_LEN, IDX_DIM), 1.0),
        'c_prompt': nrm(ks[7], (BATCH, D_MODEL), 1.0),
        'c_sample': nrm(ks[8], (DEC_BATCH, D_MODEL), 1.0),
        'rel_bias': nrm(ks[9], (REL_BUCKETS, BIAS_HEADS), 0.2),
        'w_in_diff': nrm(ks[10], (N_DIFF_LAYERS, D_MODEL, DIFF_IN), d_inv),
        'w_out_diff': nrm(ks[11], (N_DIFF_LAYERS, D_MODEL, D_MODEL), d_inv),
        'lam_diff': nrm(ks[12], (N_DIFF_LAYERS, 4, DIFF_HEAD_DIM), 0.1),
        'subln_g_diff': 1.0 + nrm(ks[13], (N_DIFF_LAYERS, 2 * DIFF_HEAD_DIM), 0.02),
        'w_in_dsa': nrm(ks[14], (N_DSA_LAYERS, D_MODEL, DSA_IN), d_inv),
        'w_out_dsa': nrm(ks[15], (N_DSA_LAYERS, D_MODEL, D_MODEL), d_inv),
        'w_ada': nrm(ks[16], (DEPTH, D_MODEL, 6 * D_MODEL), 0.5 * d_inv),
        'b_ada': nrm(ks[17], (DEPTH, 6 * D_MODEL), 0.01),
        'g_norm_mix': 1.0 + nrm(ks[18], (DEPTH, D_MODEL), 0.02),
        'g_norm_ffn': 1.0 + nrm(ks[19], (DEPTH, D_MODEL), 0.02),
        'w_router': nrm(ks[20], (D_MODEL, N_EXPERTS), d_inv),
        'router_bias': nrm(ks[21], (N_EXPERTS,), 0.01),
        'w_gate': nrm(ks[22], (DEPTH, N_EXPERTS, D_MODEL, EXPERT_FF), d_inv),
        'w_up': nrm(ks[23], (DEPTH, N_EXPERTS, D_MODEL, EXPERT_FF), d_inv),
        'w_down': nrm(ks[24], (DEPTH, N_EXPERTS, EXPERT_FF, D_MODEL), EXPERT_FF ** -0.5),
        'g_final': 1.0 + nrm(ks[25], (D_MODEL,), 0.02),
    }


def reference(x_prompt, x_sample, cache_k_diff, cache_v_diff, cache_k_dsa, cache_v_dsa, cache_kidx_dsa,
              c_prompt, c_sample, rel_bias, w_in_diff, w_out_diff, lam_diff, subln_g_diff,
              w_in_dsa, w_out_dsa, w_ada, b_ada, g_norm_mix, g_norm_ffn,
              w_router, router_bias, w_gate, w_up, w_down, g_final):
    seq = x_prompt.shape[1]
    past = cache_k_diff.shape[2]
    n_new = x_sample.shape[1]
    pos_p = jnp.arange(seq, dtype=jnp.int32)
    pos_all = jnp.arange(past + n_new, dtype=jnp.int32)
    pos_new = pos_all[past:]
    xp, xs = x_prompt, x_sample
    kd_p, vd_p, kd_s, vd_s = [], [], [], []
    ks_p, vs_p, ki_p, ks_s, vs_s, ki_s = [], [], [], [], [], []
    for i in range(DEPTH):
        j = i // N_MIXERS
        mp = ada_mod(c_prompt, w_ada[i], b_ada[i])
        ms = ada_mod(c_sample, w_ada[i], b_ada[i])
        hp = modulate(rms_norm(xp, g_norm_mix[i]), mp[0], mp[1])
        hs = modulate(rms_norm(xs, g_norm_mix[i]), ms[0], ms[1])
        if i % N_MIXERS == 0:
            lam_init = 0.8 - 0.6 * math.exp(-0.3 * i)
            wts = (w_in_diff[j], w_out_diff[j], lam_diff[j], subln_g_diff[j], rel_bias, lam_init)
            op, k_new, v_new = diff_mixer(hp, None, None, pos_p, pos_p, *wts)
            kd_p.append(k_new)
            vd_p.append(v_new)
            os_, k_new, v_new = diff_mixer(hs, cache_k_diff[j], cache_v_diff[j], pos_new, pos_all, *wts)
            kd_s.append(k_new)
            vd_s.append(v_new)
        else:
            wts = (w_in_dsa[j], w_out_dsa[j], rel_bias)
            op, k_new, v_new, ki_new = dsa_mixer(hp, None, None, None, pos_p, pos_p, *wts)
            ks_p.append(k_new)
            vs_p.append(v_new)
            ki_p.append(ki_new)
            os_, k_new, v_new, ki_new = dsa_mixer(hs, cache_k_dsa[j], cache_v_dsa[j], cache_kidx_dsa[j],
                                                  pos_new, pos_all, *wts)
            ks_s.append(k_new)
            vs_s.append(v_new)
            ki_s.append(ki_new)
        xp = xp + mp[2] * op
        xs = xs + ms[2] * os_
        xp = xp + mp[5] * moe_ffn(modulate(rms_norm(xp, g_norm_ffn[i]), mp[3], mp[4]),
                                  w_router, router_bias, w_gate[i], w_up[i], w_down[i])
        xs = xs + ms[5] * moe_ffn(modulate(rms_norm(xs, g_norm_ffn[i]), ms[3], ms[4]),
                                  w_router, router_bias, w_gate[i], w_up[i], w_down[i])
    y_prompt = rms_norm(xp, g_final)
    y_sample = rms_norm(xs, g_final)
    k_diff_p = jnp.stack(kd_p)
    v_diff_p = jnp.stack(vd_p)
    k_dsa_p = jnp.stack(ks_p)
    v_dsa_p = jnp.stack(vs_p)
    kidx_dsa_p = jnp.stack(ki_p)
    k_diff_s = jnp.stack(kd_s)
    v_diff_s = jnp.stack(vd_s)
    k_dsa_s = jnp.stack(ks_s)
    v_dsa_s = jnp.stack(vs_s)
    kidx_dsa_s = jnp.stack(ki_s)
    return (y_prompt, y_sample, k_diff_p, v_diff_p, k_dsa_p, v_dsa_p, kidx_dsa_p,
            k_diff_s, v_diff_s, k_dsa_s, v_dsa_s, kidx_dsa_s)
```

```python
import functools

import jax
import jax.numpy as jnp
import numpy as np
from jax import lax
from jax.experimental import pallas as pl
from jax.experimental.pallas import tpu as pltpu

F32 = jnp.float32
BF16 = jnp.bfloat16
I32 = jnp.int32

D_MODEL = 1024
CHUNK = 64
DIFF_HEADS = 8
DIFF_HEAD_DIM = 64
DSA_HEADS = 16
DSA_KV_HEADS = 4
DSA_GROUP = DSA_HEADS // DSA_KV_HEADS
DSA_HEAD_DIM = 64
IDX_HEADS = 8
IDX_DIM = 64
IDX_TOPK = 256
REL_BUCKETS = 32
BIAS_HEADS = 16
N_EXPERTS = 16
N_GROUPS = 4
EXPERTS_PER_GROUP = N_EXPERTS // N_GROUPS
EXPERT_FF = 512
RMS_EPS = 1e-6

LANES = 128
NEG = -1e30
INT_MIN = -(2 ** 31)
NEG_INF_KEY = -0x7F800000
VMEM_LIMIT = 56 * 1024 * 1024

_BUCKET_STEPS = (12, 16, 23, 32, 46, 64, 91)

_NT = (((1,), (1,)), ((), ()))


def _pick_tile(n, prefs):
    for t in prefs:
        if n % t == 0:
            return t
    raise ValueError(f"no tile in {prefs} divides {n}")


def _rms(x, g):
    ms = jnp.mean(x * x, axis=-1, keepdims=True)
    return x * lax.rsqrt(ms + RMS_EPS) * g


def _per_segment(x, seg_vals, op):
    tm, d = x.shape
    nseg = tm // CHUNK
    return op(x.reshape(nseg, CHUNK, d), seg_vals[:, None, :]).reshape(tm, d)


def _norm_mod(x, g, shift, scale):
    y = _rms(x, g)
    y = _per_segment(y, scale, lambda a, s: a * (1.0 + s))
    return _per_segment(y, shift, lambda a, s: a + s)


def _softmax_step(s, v, m_ref, l_ref, acc_ref, idx):
    m_prev = m_ref[idx]
    m_new = jnp.maximum(m_prev, jnp.max(s, axis=1, keepdims=True))
    alpha = jnp.exp(m_prev - m_new)
    p = jnp.exp(s - m_new[:, :1])
    l_ref[idx] = alpha * l_ref[idx] + jnp.sum(p, axis=1, keepdims=True)
    acc_ref[idx] = alpha * acc_ref[idx] + jnp.dot(p.astype(BF16), v, preferred_element_type=F32)
    m_ref[idx] = m_new


def _ada_kernel(c_ref, w_ref, b_ref, o_ref):
    c = c_ref[...]
    a = (c * jax.nn.sigmoid(c)).astype(BF16)
    o_ref[0] = jnp.dot(a, w_ref[0].astype(BF16), preferred_element_type=F32) + b_ref[0]


def _ada_mod(c_all, w_ada, b_ada):
    depth, d, n6 = w_ada.shape
    bc = c_all.shape[0]
    tn = _pick_tile(n6, (1536, 1024, 512, 128))
    return pl.pallas_call(
        _ada_kernel,
        grid=(depth, n6 // tn),
        in_specs=[
            pl.BlockSpec((bc, d), lambda l, j: (0, 0)),
            pl.BlockSpec((1, d, tn), lambda l, j: (l, 0, j)),
            pl.BlockSpec((1, 1, tn), lambda l, j: (l, 0, j)),
        ],
        out_specs=pl.BlockSpec((1, bc, tn), lambda l, j: (l, 0, j)),
        out_shape=jax.ShapeDtypeStruct((depth, bc, n6), F32),
        compiler_params=pltpu.CompilerParams(vmem_limit_bytes=VMEM_LIMIT),
        name="ada_mod",
    )(c_all, w_ada, b_ada.reshape(depth, 1, n6))


def _bias_kernel(off_ref, q0_ref, len_ref, rb_ref, o_ref, *, tq, tk):
    v = pl.program_id(0)
    h = pl.program_id(1)
    r = lax.broadcasted_iota(I32, (tq, tk), 0)
    c = lax.broadcasted_iota(I32, (tq, tk), 1)
    rel = off_ref[v] + c - r
    qpos = q0_ref[v] + r
    kpos = qpos + rel
    n = jnp.abs(rel)
    big = jnp.full((tq, tk), 8, I32)
    for t in _BUCKET_STEPS:
        big = big + jnp.where(n >= t, 1, 0)
    bucket = jnp.where(n < 8, n, big) + jnp.where(rel > 0, REL_BUCKETS // 2, 0)
    val = jnp.zeros((tq, tk), F32)
    for b in range(REL_BUCKETS):
        val = jnp.where(bucket == b, rb_ref[b, h], val)
    vis = jnp.where((kpos >> 6) <= (qpos >> 6), 1, 0) * jnp.where(kpos < len_ref[0], 1, 0)
    o_ref[0, 0] = jnp.where(vis > 0, val, NEG)


def _bias_tiles(rel_bias, offs, q0s, length, tq, tk):
    nvar = len(offs)
    return pl.pallas_call(
        functools.partial(_bias_kernel, tq=tq, tk=tk),
        grid_spec=pltpu.PrefetchScalarGridSpec(
            num_scalar_prefetch=3,
            grid=(nvar, BIAS_HEADS),
            in_specs=[pl.BlockSpec(memory_space=pltpu.SMEM)],
            out_specs=pl.BlockSpec((1, 1, tq, tk), lambda v, h, *_: (v, h, 0, 0)),
        ),
        out_shape=jax.ShapeDtypeStruct((nvar, BIAS_HEADS, tq, tk), F32),
        name="bias_tiles",
    )(jnp.asarray(offs, I32), jnp.asarray(q0s, I32), jnp.asarray([length], I32), rel_bias)


def _inproj_kernel(x_ref, g_ref, shift_ref, scale_ref, w_ref, *out_refs, cols):
    h = _norm_mod(x_ref[...], g_ref[...], shift_ref[0], scale_ref[0]).astype(BF16)
    for start, width, dests in cols:
        p = jnp.dot(h, w_ref[:, start:start + width], preferred_element_type=F32)
        for out_idx, out_off, mult in dests:
            o_ref = out_refs[out_idx]
            val = p if mult == 1.0 else p * mult
            o_ref[:, out_off:out_off + width] = val.astype(o_ref.dtype)


def _in_proj(x, g, shift, scale, w_bf, cols, out_defs, tm):
    n, d = x.shape
    nt = n // tm
    nseg = tm // CHUNK
    nout = w_bf.shape[1]
    seg_spec = pl.BlockSpec((1, nseg, d), lambda t: (t, 0, 0))
    return pl.pallas_call(
        functools.partial(_inproj_kernel, cols=cols),
        grid=(nt,),
        in_specs=[
            pl.BlockSpec((tm, d), lambda t: (t, 0)),
            pl.BlockSpec((1, d), lambda t: (0, 0)),
            seg_spec, seg_spec,
            pl.BlockSpec((d, nout), lambda t: (0, 0)),
        ],
        out_specs=[pl.BlockSpec((tm, w), lambda t: (t, 0)) for w, _ in out_defs],
        out_shape=[jax.ShapeDtypeStruct((n, w), dt) for w, dt in out_defs],
        compiler_params=pltpu.CompilerParams(vmem_limit_bytes=VMEM_LIMIT),
        name="in_proj",
    )(x, g.reshape(1, d), shift.reshape(nt, nseg, d), scale.reshape(nt, nseg, d), w_bf)


def _outproj_kernel(o_ref, x_ref, gate_ref, w_ref, out_ref):
    r = jnp.dot(o_ref[...], w_ref[...], preferred_element_type=F32)
    out_ref[...] = x_ref[...] + _per_segment(r, gate_ref[0], lambda a, s: a * s)


def _out_proj(o, x, gate, w_bf, tm):
    n, d = x.shape
    kdim = o.shape[1]
    nt = n // tm
    nseg = tm // CHUNK
    return pl.pallas_call(
        _outproj_kernel,
        grid=(nt,),
        in_specs=[
            pl.BlockSpec((tm, kdim), lambda t: (t, 0)),
            pl.BlockSpec((tm, d), lambda t: (t, 0)),
            pl.BlockSpec((1, nseg, d), lambda t: (t, 0, 0)),
            pl.BlockSpec((kdim, d), lambda t: (0, 0)),
        ],
        out_specs=pl.BlockSpec((tm, d), lambda t: (t, 0)),
        out_shape=jax.ShapeDtypeStruct((n, d), F32),
        compiler_params=pltpu.CompilerParams(vmem_limit_bytes=VMEM_LIMIT),
        name="out_proj",
    )(o, x, gate.reshape(nt, nseg, d), w_bf)


def _diff_attn_kernel(it_ref, jt_ref, var_ref, fin_ref, lam_ref, q_ref, k_ref, v_ref, bias_ref, g_ref,
                      o_ref, m_s, l_s, acc_s, *, tq, lam_init):
    del it_ref, var_ref
    t = pl.program_id(1)

    @pl.when(jt_ref[t] == 0)
    def _():
        m_s[...] = jnp.full(m_s.shape, NEG, F32)
        l_s[...] = jnp.zeros(l_s.shape, F32)
        acc_s[...] = jnp.zeros(acc_s.shape, F32)

    lane = lax.broadcasted_iota(I32, (tq, LANES), 1)
    for h in range(DIFF_HEADS):
        sl = slice(h * LANES, (h + 1) * LANES)
        qh = q_ref[0, :, sl].astype(F32)
        kh = k_ref[0, :, sl]
        vh = v_ref[0, :, sl]
        for mm in range(2):
            keep = (lane < DIFF_HEAD_DIM) if mm == 0 else (lane >= DIFF_HEAD_DIM)
            qm = jnp.where(keep, qh, 0.0).astype(BF16)
            s = lax.dot_general(qm, kh, _NT, preferred_element_type=F32) + bias_ref[0, 2 * h + mm]
            _softmax_step(s, vh, m_s, l_s, acc_s, 2 * h + mm)

    @pl.when(fin_ref[t] == 1)
    def _():
        lp = lam_ref[...]
        lam = (jnp.exp(jnp.sum(lp[0:1] * lp[1:2], axis=1, keepdims=True))
               - jnp.exp(jnp.sum(lp[2:3] * lp[3:4], axis=1, keepdims=True)) + lam_init)
        g = g_ref[...]
        for h in range(DIFF_HEADS):
            o = acc_s[2 * h] / l_s[2 * h] - lam * (acc_s[2 * h + 1] / l_s[2 * h + 1])
            o = _rms(o, g) * (1.0 - lam_init)
            o_ref[0, :, h * LANES:(h + 1) * LANES] = o.astype(o_ref.dtype)


def _diff_attn(q, k, v, bias, pairs, lam_p, subln_g, lam_init, tq, tk):
    b, sq, dq = q.shape
    it = np.asarray([p[0] for p in pairs], np.int32)
    jt = np.asarray([p[1] for p in pairs], np.int32)
    vt = np.asarray([p[2] for p in pairs], np.int32)
    fin = np.asarray([1 if (n + 1 == len(pairs) or pairs[n + 1][0] != p[0]) else 0
                      for n, p in enumerate(pairs)], np.int32)
    return pl.pallas_call(
        functools.partial(_diff_attn_kernel, tq=tq, lam_init=lam_init),
        grid_spec=pltpu.PrefetchScalarGridSpec(
            num_scalar_prefetch=4,
            grid=(b, len(pairs)),
            in_specs=[
                pl.BlockSpec((4, DIFF_HEAD_DIM), lambda bb, t, *_: (0, 0)),
                pl.BlockSpec((1, tq, dq), lambda bb, t, it_, jt_, vt_, fin_: (bb, it_[t], 0)),
                pl.BlockSpec((1, tk, dq), lambda bb, t, it_, jt_, vt_, fin_: (bb, jt_[t], 0)),
                pl.BlockSpec((1, tk, dq), lambda bb, t, it_, jt_, vt_, fin_: (bb, jt_[t], 0)),
                pl.BlockSpec((1, BIAS_HEADS, tq, tk), lambda bb, t, it_, jt_, vt_, fin_: (vt_[t], 0, 0, 0)),
                pl.BlockSpec((1, 2 * DIFF_HEAD_DIM), lambda bb, t, *_: (0, 0)),
            ],
            out_specs=pl.BlockSpec((1, tq, dq), lambda bb, t, it_, jt_, vt_, fin_: (bb, it_[t], 0)),
            scratch_shapes=[pltpu.VMEM((2 * DIFF_HEADS, tq, LANES), F32)] * 3,
        ),
        out_shape=jax.ShapeDtypeStruct((b, sq, dq), BF16),
        compiler_params=pltpu.CompilerParams(vmem_limit_bytes=VMEM_LIMIT),
        name="diff_attn",
    )(jnp.asarray(it), jnp.asarray(jt), jnp.asarray(vt), jnp.asarray(fin), lam_p, q, k, v, bias,
      subln_g.reshape(1, -1))


def _sort_key(x):
    bits = pltpu.bitcast(x, I32)
    return jnp.where(bits < 0, INT_MIN - bits, bits)


def _dsa_kernel(nvis_ref, var_ref, q_ref, qi_ref, wi_ref, ki_ref, k_ref, v_ref, bias_ref, o_ref,
                keys_s, qs_s, m_s, l_s, acc_s, *, tq, tk, nk, q0, length, topk, nbits):
    i = pl.program_id(1)
    nvis = nvis_ref[i]
    gq = DSA_GROUP * tq

    for c in range(DSA_KV_HEADS):
        for g in range(DSA_GROUP):
            hd = DSA_GROUP * c + g
            qs_s[c, g * tq:(g + 1) * tq, :] = q_ref[0, :, hd * LANES:(hd + 1) * LANES]
    qi_stack = jnp.concatenate([qi_ref[0, :, u * LANES:(u + 1) * LANES] for u in range(IDX_HEADS // 2)], axis=0)
    wcols = [wi_ref[0, :, h:h + 1] for h in range(IDX_HEADS)]

    r_io = lax.broadcasted_iota(I32, (tq, tk), 0)
    c_io = lax.broadcasted_iota(I32, (tq, tk), 1)
    q_chunk = (q0 + i * tq + r_io) >> 6
    score_scale = IDX_DIM ** -0.5 * IDX_HEADS ** -0.5

    def score_body(j, carry):
        off = pl.multiple_of(j * tk, tk)
        ki_lo = ki_ref[0, pl.ds(off, tk), 0:LANES]
        ki_hi = ki_ref[0, pl.ds(off, tk), LANES:2 * LANES]
        s_even = lax.dot_general(qi_stack, ki_lo, _NT, preferred_element_type=F32)
        s_odd = lax.dot_general(qi_stack, ki_hi, _NT, preferred_element_type=F32)
        score = jnp.zeros((tq, tk), F32)
        for u in range(IDX_HEADS // 2):
            score = score + jnp.maximum(s_even[u * tq:(u + 1) * tq], 0.0) * wcols[2 * u]
            score = score + jnp.maximum(s_odd[u * tq:(u + 1) * tq], 0.0) * wcols[2 * u + 1]
        key = _sort_key(score * score_scale)
        kpos = j * tk + c_io
        vis = jnp.where((kpos >> 6) <= q_chunk, 1, 0) * jnp.where(kpos < length, 1, 0)
        keys_s[j] = jnp.where(vis > 0, key, INT_MIN)
        return carry

    lax.fori_loop(0, nvis, score_body, 0)

    def count_ge(cand):
        def body(j, acc):
            return acc + jnp.where(keys_s[j] >= cand, 1.0, 0.0)
        acc = lax.fori_loop(0, nvis, body, jnp.zeros((tq, tk), F32))
        return jnp.sum(acc, axis=1, keepdims=True)

    def bit_body(it, ans):
        cand = ans | jnp.left_shift(jnp.int32(1), 31 - it)
        return jnp.where(count_ge(cand ^ INT_MIN) >= topk, cand, ans)

    thr = lax.fori_loop(0, 32, bit_body, jnp.zeros((tq, 1), I32)) ^ INT_MIN
    thr_sel = jnp.maximum(thr, NEG_INF_KEY + 1)
    row_full = thr >= thr_sel
    cnt_ge = count_ge(thr)
    cnt_gt = count_ge(thr + 1)
    excess = jnp.max(jnp.where(row_full, cnt_ge - topk, 0.0))

    @pl.when(excess > 0.0)
    def _():
        quota = topk - cnt_gt

        def count_tied_before(pos):
            def body(j, acc):
                kpos = j * tk + c_io
                hit = jnp.where(keys_s[j] == thr, 1.0, 0.0) * jnp.where(kpos < pos, 1.0, 0.0)
                return acc + hit
            acc = lax.fori_loop(0, nvis, body, jnp.zeros((tq, tk), F32))
            return jnp.sum(acc, axis=1, keepdims=True)

        def pos_body(it, p):
            cand = p | jnp.left_shift(jnp.int32(1), nbits - 1 - it)
            return jnp.where(count_tied_before(cand) < quota, cand, p)

        last = lax.fori_loop(0, nbits, pos_body, jnp.zeros((tq, 1), I32))
        last = jnp.where(row_full, last, nk * tk)

        def demote(j, carry):
            kpos = j * tk + c_io
            kk = keys_s[j]
            drop = jnp.where(kk == thr, 1, 0) * jnp.where(kpos > last, 1, 0)
            keys_s[j] = jnp.where(drop > 0, thr - 1, kk)
            return carry

        lax.fori_loop(0, nvis, demote, 0)

    m_s[...] = jnp.full(m_s.shape, NEG, F32)
    l_s[...] = jnp.zeros(l_s.shape, F32)
    acc_s[...] = jnp.zeros(acc_s.shape, F32)

    def attn_body(j, carry):
        off = pl.multiple_of(j * tk, tk)
        var = var_ref[i * nk + j]
        maskb = jnp.where(keys_s[j] >= thr_sel, 0.0, NEG)
        for c in range(DSA_KV_HEADS):
            pair = slice((c // 2) * LANES, (c // 2 + 1) * LANES)
            kp = k_ref[0, pl.ds(off, tk), pair]
            vp = v_ref[0, pl.ds(off, tk), pair]
            s = lax.dot_general(qs_s[c], kp, _NT, preferred_element_type=F32)
            bias = bias_ref[var, DSA_GROUP * c:DSA_GROUP * (c + 1)]
            s = (s.reshape(DSA_GROUP, tq, tk) + bias + maskb[None]).reshape(gq, tk)
            _softmax_step(s, vp, m_s, l_s, acc_s, c)
        return carry

    lax.fori_loop(0, nvis, attn_body, 0)

    for c in range(DSA_KV_HEADS):
        o = acc_s[c] / l_s[c]
        for g in range(DSA_GROUP):
            hd = DSA_GROUP * c + g
            o_ref[0, :, hd * LANES:(hd + 1) * LANES] = o[g * tq:(g + 1) * tq].astype(o_ref.dtype)


def _dsa_attn(q, qi, wi, ki2, k, v, bias, nvis, var, q0, length, topk, tq, tk):
    b, sq, dq = q.shape
    lp = k.shape[1]
    nq = sq // tq
    nk = lp // tk
    nvar = bias.shape[0]
    gq = DSA_GROUP * tq
    kernel = functools.partial(_dsa_kernel, tq=tq, tk=tk, nk=nk, q0=q0, length=length, topk=topk,
                               nbits=max(1, (lp - 1).bit_length()))
    full = lambda bb, ii, *_: (bb, 0, 0)
    tile = lambda bb, ii, *_: (bb, ii, 0)
    return pl.pallas_call(
        kernel,
        grid_spec=pltpu.PrefetchScalarGridSpec(
            num_scalar_prefetch=2,
            grid=(b, nq),
            in_specs=[
                pl.BlockSpec((1, tq, dq), tile),
                pl.BlockSpec((1, tq, qi.shape[2]), tile),
                pl.BlockSpec((1, tq, LANES), tile),
                pl.BlockSpec((1, lp, 2 * LANES), full),
                pl.BlockSpec((1, lp, k.shape[2]), full),
                pl.BlockSpec((1, lp, v.shape[2]), full),
                pl.BlockSpec((nvar, BIAS_HEADS, tq, tk), lambda bb, ii, *_: (0, 0, 0, 0)),
            ],
            out_specs=pl.BlockSpec((1, tq, dq), tile),
            scratch_shapes=[
                pltpu.VMEM((nk, tq, tk), I32),
                pltpu.VMEM((DSA_KV_HEADS, gq, LANES), BF16),
                pltpu.VMEM((DSA_KV_HEADS, gq, LANES), F32),
                pltpu.VMEM((DSA_KV_HEADS, gq, LANES), F32),
                pltpu.VMEM((DSA_KV_HEADS, gq, LANES), F32),
            ],
        ),
        out_shape=jax.ShapeDtypeStruct((b, sq, dq), BF16),
        compiler_params=pltpu.CompilerParams(vmem_limit_bytes=VMEM_LIMIT),
        name="dsa_attn",
    )(jnp.asarray(nvis, I32), jnp.asarray(var, I32).reshape(-1), q, qi, wi, ki2, k, v, bias)


def _route(lt_s, rb_ref, gt_s):
    rows = [lt_s[e:e + 1, :] for e in range(N_EXPERTS)]
    mx = functools.reduce(jnp.maximum, rows)
    ex = [jnp.exp(r - mx) for r in rows]
    z = functools.reduce(lambda a, b: a + b, ex)
    probs = [x / z for x in ex]
    sel = [probs[e] + rb_ref[e] for e in range(N_EXPERTS)]
    best_val, best_grp = None, None
    for gidx in range(N_GROUPS):
        a, b, c, d = sel[EXPERTS_PER_GROUP * gidx:EXPERTS_PER_GROUP * (gidx + 1)]
        hi1, lo1, hi2, lo2 = jnp.maximum(a, b), jnp.minimum(a, b), jnp.maximum(c, d), jnp.minimum(c, d)
        top2 = jnp.maximum(hi1, hi2) + jnp.maximum(jnp.minimum(hi1, hi2), jnp.maximum(lo1, lo2))
        if gidx == 0:
            best_val, best_grp = top2, jnp.zeros_like(top2, dtype=I32)
        else:
            better = top2 > best_val
            best_grp = jnp.where(better, gidx, best_grp)
            best_val = jnp.where(better, top2, best_val)
    masked = [jnp.where(best_grp == e // EXPERTS_PER_GROUP, sel[e], -jnp.inf) for e in range(N_EXPERTS)]

    def first_argmax(vals):
        bv, bi = vals[0], jnp.zeros_like(best_grp)
        for e in range(1, N_EXPERTS):
            better = vals[e] > bv
            bi = jnp.where(better, e, bi)
            bv = jnp.where(better, vals[e], bv)
        return bi

    i1 = first_argmax(masked)
    i2 = first_argmax([jnp.where(i1 == e, -jnp.inf, masked[e]) for e in range(N_EXPERTS)])
    w1 = functools.reduce(lambda a, b: a + b, [jnp.where(i1 == e, probs[e], 0.0) for e in range(N_EXPERTS)])
    w2 = functools.reduce(lambda a, b: a + b, [jnp.where(i2 == e, probs[e], 0.0) for e in range(N_EXPERTS)])
    wsum = w1 + w2
    w1, w2 = w1 / wsum, w2 / wsum
    for e in range(N_EXPERTS):
        gt_s[e:e + 1, :] = jnp.where(i1 == e, w1, 0.0) + jnp.where(i2 == e, w2, 0.0)


def _moe_kernel(rb_ref, x_ref, g_ref, shift_ref, scale_ref, gate_ref, wrh_ref, wrl_ref, wg_ref, wu_ref, wd_ref,
                gfin_ref, out_ref, hb_s, lt_s, gt_s, gates_s, acc_s, *, final_norm):
    e = pl.program_id(1)

    @pl.when(e == 0)
    def _():
        h = _norm_mod(x_ref[...], g_ref[...], shift_ref[0], scale_ref[0])
        hb = h.astype(BF16)
        hb_s[...] = hb
        hl = (h - hb.astype(F32)).astype(BF16)
        lt_s[...] = (lax.dot_general(wrh_ref[...], hb, _NT, preferred_element_type=F32)
                     + lax.dot_general(wrl_ref[...], hb, _NT, preferred_element_type=F32)
                     + lax.dot_general(wrh_ref[...], hl, _NT, preferred_element_type=F32))
        gt_s[...] = jnp.zeros(gt_s.shape, F32)
        _route(lt_s, rb_ref, gt_s)
        gates_s[...] = gt_s[...].T
        acc_s[...] = jnp.zeros(acc_s.shape, F32)

    hb = hb_s[...]
    up = jnp.dot(hb, wu_ref[0], preferred_element_type=F32)
    gt = jnp.dot(hb, wg_ref[0], preferred_element_type=F32)
    lane = lax.broadcasted_iota(I32, gates_s.shape, 1)
    gcol = jnp.sum(jnp.where(lane == e, gates_s[...], 0.0), axis=1, keepdims=True)
    a = (gt * jax.nn.sigmoid(gt) * up * gcol).astype(BF16)
    acc_s[...] += jnp.dot(a, wd_ref[0], preferred_element_type=F32)

    @pl.when(e == N_EXPERTS - 1)
    def _():
        y = x_ref[...] + _per_segment(acc_s[...], gate_ref[0], lambda v, s: v * s)
        if final_norm:
            y = _rms(y, gfin_ref[...])
        out_ref[...] = y


def _moe(x, g, shift, scale, gate, w_router, router_bias, wg_bf, wu_bf, wd_bf, g_final, final_norm, tm):
    n, d = x.shape
    nt = n // tm
    nseg = tm // CHUNK
    ff = wg_bf.shape[2]
    wrt = w_router.T
    wrh = wrt.astype(BF16)
    wrl = (wrt - wrh.astype(F32)).astype(BF16)
    seg_spec = pl.BlockSpec((1, nseg, d), lambda t, e: (t, 0, 0))
    row_spec = pl.BlockSpec((1, d), lambda t, e: (0, 0))
    return pl.pallas_call(
        functools.partial(_moe_kernel, final_norm=final_norm),
        grid=(nt, N_EXPERTS),
        in_specs=[
            pl.BlockSpec(memory_space=pltpu.SMEM),
            pl.BlockSpec((tm, d), lambda t, e: (t, 0)),
            row_spec, seg_spec, seg_spec, seg_spec,
            pl.BlockSpec((N_EXPERTS, d), lambda t, e: (0, 0)),
            pl.BlockSpec((N_EXPERTS, d), lambda t, e: (0, 0)),
            pl.BlockSpec((1, d, ff), lambda t, e: (e, 0, 0)),
            pl.BlockSpec((1, d, ff), lambda t, e: (e, 0, 0)),
            pl.BlockSpec((1, ff, d), lambda t, e: (e, 0, 0)),
            row_spec,
        ],
        out_specs=pl.BlockSpec((tm, d), lambda t, e: (t, 0)),
        out_shape=jax.ShapeDtypeStruct((n, d), F32),
        scratch_shapes=[
            pltpu.VMEM((tm, d), BF16),
            pltpu.VMEM((N_EXPERTS, tm), F32),
            pltpu.VMEM((LANES, tm), F32),
            pltpu.VMEM((tm, LANES), F32),
            pltpu.VMEM((tm, d), F32),
        ],
        compiler_params=pltpu.CompilerParams(vmem_limit_bytes=VMEM_LIMIT),
        name="moe",
    )(router_bias, x, g.reshape(1, d), shift.reshape(nt, nseg, d), scale.reshape(nt, nseg, d),
      gate.reshape(nt, nseg, d), wrh, wrl, wg_bf, wu_bf, wd_bf, g_final.reshape(1, d))


def _dsa_in_weights(w):
    d = w.shape[0]
    o1 = DSA_HEADS * DSA_HEAD_DIM
    o2 = o1 + DSA_KV_HEADS * DSA_HEAD_DIM
    o3 = o2 + DSA_KV_HEADS * DSA_HEAD_DIM
    o4 = o3 + IDX_HEADS * IDX_DIM
    o5 = o4 + IDX_DIM
    wq = w[:, :o1].reshape(d, DSA_HEADS, 1, DSA_HEAD_DIM)
    half = ((np.arange(DSA_HEADS) // DSA_GROUP) % 2).reshape(1, DSA_HEADS, 1, 1)
    wq = jnp.concatenate([jnp.where(half == 0, wq, 0.0), jnp.where(half == 1, wq, 0.0)], axis=2)
    wki = w[:, o4:o5]
    zk = jnp.zeros_like(wki)
    wwi = jnp.pad(w[:, o5:], ((0, 0), (0, LANES - IDX_HEADS)))
    return jnp.concatenate([wq.reshape(d, 2 * o1), w[:, o1:o4], wki, zk, zk, wki, wwi], axis=1).astype(BF16)


def _dsa_out_weights(w):
    d = w.shape[1]
    w4 = w.reshape(DSA_HEADS, 1, DSA_HEAD_DIM, d)
    half = ((np.arange(DSA_HEADS) // DSA_GROUP) % 2).reshape(DSA_HEADS, 1, 1, 1)
    w4 = jnp.concatenate([jnp.where(half == 0, w4, 0.0), jnp.where(half == 1, w4, 0.0)], axis=1)
    return w4.reshape(2 * DSA_HEADS * DSA_HEAD_DIM, d).astype(BF16)


def _pad_keys(x, lp):
    return jnp.pad(x, ((0, 0), (0, lp - x.shape[1]), (0, 0)))


def kernel(x_prompt, x_sample, cache_k_diff, cache_v_diff, cache_k_dsa, cache_v_dsa, cache_kidx_dsa,
           c_prompt, c_sample, rel_bias, w_in_diff, w_out_diff, lam_diff, subln_g_diff, w_in_dsa, w_out_dsa,
           w_ada, b_ada, g_norm_mix, g_norm_ffn, w_router, router_bias, w_gate, w_up, w_down, g_final):
    b, s, d = x_prompt.shape
    bs, ss, _ = x_sample.shape
    past = cache_k_diff.shape[2]
    depth = w_ada.shape[0]
    n_p, n_s = b * s, bs * ss
    n = n_p + n_s
    assert d == D_MODEL and ss == CHUNK and s % 256 == 0 and past % CHUNK == 0 and depth == 2
    tm = _pick_tile(n, (512, 256, 128, 64))
    tm_moe = _pick_tile(n, (1024, 512, 256, 128))
    l_s = past + ss
    lp_s = -(-l_s // LANES) * LANES

    x = jnp.concatenate([x_prompt.reshape(n_p, d), x_sample.reshape(n_s, d)], axis=0)
    mod = _ada_mod(jnp.concatenate([c_prompt, c_sample], axis=0), w_ada, b_ada)
    seg_batch = np.concatenate([np.repeat(np.arange(b), s // CHUNK), b + np.arange(bs)])

    def mods(layer):
        m = mod[layer][seg_batch]
        return [m[:, i * d:(i + 1) * d] for i in range(6)]

    def split(a, width):
        return a[:n_p].reshape(b, s, width), a[n_p:].reshape(bs, ss, width)

    m0 = mods(0)
    qk = DIFF_HEADS * 2 * DIFF_HEAD_DIM
    cols = ((0, qk, ((0, 0, DIFF_HEAD_DIM ** -0.5),)),
            (qk, qk, ((1, 0, 1.0), (3, 0, 1.0))),
            (2 * qk, qk, ((2, 0, 1.0), (4, 0, 1.0))))
    outs = ((qk, BF16), (qk, F32), (qk, F32), (qk, BF16), (qk, BF16))
    q_bf, k_f, v_f, k_bf, v_bf = _in_proj(x, g_norm_mix[0], m0[0], m0[1], w_in_diff[0].astype(BF16), cols, outs, tm)
    kd_p, kd_s = split(k_f, qk)
    vd_p, vd_s = split(v_f, qk)
    q_p, q_s = split(q_bf, qk)
    kb_p, kb_s = split(k_bf, qk)
    vb_p, vb_s = split(v_bf, qk)
    lam_init = 0.8 - 0.6 * float(np.exp(-0.3 * 0))

    tq = tk = 256
    nq = s // tq
    pairs = [(i, j, min(i - j, 2)) for i in range(nq) for j in range(i + 1)]
    bias_p = _bias_tiles(rel_bias, [-tk * v for v in range(3)], [2 * tk] * 3, 1 << 30, tq, tk)
    o_p = _diff_attn(q_p, kb_p, vb_p, bias_p, pairs, lam_diff[0], subln_g_diff[0], lam_init, tq, tk)

    ka_s = _pad_keys(jnp.concatenate([cache_k_diff[0].reshape(bs, past, qk).astype(BF16), kb_s], axis=1), lp_s)
    va_s = _pad_keys(jnp.concatenate([cache_v_diff[0].reshape(bs, past, qk).astype(BF16), vb_s], axis=1), lp_s)
    bias_s = _bias_tiles(rel_bias, [-past], [past], l_s, ss, lp_s)
    o_s = _diff_attn(q_s, ka_s, va_s, bias_s, [(0, 0, 0)], lam_diff[0], subln_g_diff[0], lam_init, ss, lp_s)

    o = jnp.concatenate([o_p.reshape(n_p, qk), o_s.reshape(n_s, qk)], axis=0)
    x = _out_proj(o, x, m0[2], w_out_diff[0].astype(BF16), tm)
    x = _moe(x, g_norm_ffn[0], m0[3], m0[4], m0[5], w_router, router_bias, w_gate[0].astype(BF16),
             w_up[0].astype(BF16), w_down[0].astype(BF16), g_final, False, tm_moe)

    m1 = mods(1)
    nq_w = 2 * DSA_HEADS * DSA_HEAD_DIM
    kv_w = DSA_KV_HEADS * DSA_HEAD_DIM
    qi_w = IDX_HEADS * IDX_DIM
    c_k, c_v, c_qi, c_ki, c_wi = nq_w, nq_w + kv_w, nq_w + 2 * kv_w, nq_w + 2 * kv_w + qi_w, nq_w + 2 * kv_w + qi_w + 2 * LANES
    half_q = nq_w // 2
    cols = ((0, half_q, ((0, 0, DSA_HEAD_DIM ** -0.5),)),
            (half_q, half_q, ((0, half_q, DSA_HEAD_DIM ** -0.5),)),
            (c_k, kv_w, ((1, 0, 1.0), (3, 0, 1.0))),
            (c_v, kv_w, ((2, 0, 1.0), (4, 0, 1.0))),
            (c_qi, qi_w, ((5, 0, 1.0),)),
            (c_ki, 2 * LANES, ((6, 0, 1.0), (7, 0, 1.0))),
            (c_wi, LANES, ((8, 0, 1.0),)))
    outs = ((nq_w, BF16), (kv_w, F32), (kv_w, F32), (kv_w, BF16), (kv_w, BF16), (qi_w, BF16),
            (2 * LANES, F32), (2 * LANES, BF16), (LANES, F32))
    q2, k2_f, v2_f, k2_bf, v2_bf, qi_bf, ki2_f, ki2_bf, wi_f = _in_proj(
        x, g_norm_mix[1], m1[0], m1[1], _dsa_in_weights(w_in_dsa[0]), cols, outs, tm)
    ks_p, ks_s = split(k2_f, kv_w)
    vs_p, vs_s = split(v2_f, kv_w)
    ki_p, ki_s = split(ki2_f[:, :IDX_DIM], IDX_DIM)
    q2_p, q2_s = split(q2, nq_w)
    k2b_p, k2b_s = split(k2_bf, kv_w)
    v2b_p, v2b_s = split(v2_bf, kv_w)
    qi_p, qi_s = split(qi_bf, qi_w)
    kib_p, kib_s = split(ki2_bf, 2 * LANES)
    wi_p, wi_s = split(wi_f, LANES)

    tq, tk = 128, 256
    nq, nk = s // tq, s // tk
    nvis = [i // 2 + 1 for i in range(nq)]
    var = [[min(max(i - 2 * j, 0), 3) for j in range(nk)] for i in range(nq)]
    bias_p2 = _bias_tiles(rel_bias, [-tq * v for v in range(4)], [4 * tq] * 4, 1 << 30, tq, tk)
    o2_p = _dsa_attn(q2_p, qi_p, wi_p, kib_p, k2b_p, v2b_p, bias_p2, nvis, var, 0, s, min(IDX_TOPK, s // 4), tq, tk)

    ck = cache_k_dsa[0].reshape(bs, past, kv_w).astype(BF16)
    cv = cache_v_dsa[0].reshape(bs, past, kv_w).astype(BF16)
    cki = cache_kidx_dsa[0].astype(BF16)
    zki = jnp.zeros_like(cki)
    ka2 = _pad_keys(jnp.concatenate([ck, k2b_s], axis=1), lp_s)
    va2 = _pad_keys(jnp.concatenate([cv, v2b_s], axis=1), lp_s)
    kia = _pad_keys(jnp.concatenate([jnp.concatenate([cki, zki, zki, cki], axis=2), kib_s], axis=1), lp_s)
    tk_s = LANES
    nk_s = lp_s // tk_s
    bias_s2 = _bias_tiles(rel_bias, [j * tk_s - past for j in range(nk_s)], [past] * nk_s, l_s, ss, tk_s)
    o2_s = _dsa_attn(q2_s, qi_s, wi_s, kia, ka2, va2, bias_s2, [nk_s], [list(range(nk_s))], past, l_s,
                     min(IDX_TOPK, l_s // 4), ss, tk_s)

    o2 = jnp.concatenate([o2_p.reshape(n_p, nq_w), o2_s.reshape(n_s, nq_w)], axis=0)
    x = _out_proj(o2, x, m1[2], _dsa_out_weights(w_out_dsa[0]), tm)
    y = _moe(x, g_norm_ffn[1], m1[3], m1[4], m1[5], w_router, router_bias, w_gate[1].astype(BF16),
             w_up[1].astype(BF16), w_down[1].astype(BF16), g_final, True, tm_moe)

    h2 = (DIFF_HEADS, 2, DIFF_HEAD_DIM)
    hv = (DIFF_HEADS, 2 * DIFF_HEAD_DIM)
    hk = (DSA_KV_HEADS, DSA_HEAD_DIM)
    return (y[:n_p].reshape(b, s, d), y[n_p:].reshape(bs, ss, d),
            kd_p.reshape(1, b, s, *h2), vd_p.reshape(1, b, s, *hv),
            ks_p.reshape(1, b, s, *hk), vs_p.reshape(1, b, s, *hk), ki_p.reshape(1, b, s, IDX_DIM),
            kd_s.reshape(1, bs, ss, *h2), vd_s.reshape(1, bs, ss, *hv),
            ks_s.reshape(1, bs, ss, *hk), vs_s.reshape(1, bs, ss, *hk), ki_s.reshape(1, bs, ss, IDX_DIM))
```

```python
import functools

import jax
import jax.numpy as jnp
import numpy as np
from jax import lax
from jax.experimental import pallas as pl
from jax.experimental.pallas import tpu as pltpu

F32 = jnp.float32
BF16 = jnp.bfloat16
I32 = jnp.int32

D_MODEL = 1024
CHUNK = 64
DIFF_HEADS = 8
DIFF_HEAD_DIM = 64
DSA_HEADS = 16
DSA_KV_HEADS = 4
DSA_GROUP = DSA_HEADS // DSA_KV_HEADS
DSA_HEAD_DIM = 64
IDX_HEADS = 8
IDX_DIM = 64
IDX_TOPK = 256
REL_BUCKETS = 32
BIAS_HEADS = 16
N_EXPERTS = 16
N_GROUPS = 4
EXPERTS_PER_GROUP = N_EXPERTS // N_GROUPS
EXPERT_FF = 512
RMS_EPS = 1e-6

LANES = 128
NEG = -1e30
INT_MIN = -(2 ** 31)
NEG_INF_KEY = -0x7F800000
VMEM_LIMIT = 56 * 1024 * 1024

_BUCKET_STEPS = (12, 16, 23, 32, 46, 64, 91)

_NT = (((1,), (1,)), ((), ()))


def _pick_tile(n, prefs):
    for t in prefs:
        if n % t == 0:
            return t
    raise ValueError(f"no tile in {prefs} divides {n}")


def _rms(x, g):
    ms = jnp.mean(x * x, axis=-1, keepdims=True)
    return x * lax.rsqrt(ms + RMS_EPS) * g


def _per_segment(x, seg_vals, op):
    tm, d = x.shape
    nseg = tm // CHUNK
    return op(x.reshape(nseg, CHUNK, d), seg_vals[:, None, :]).reshape(tm, d)


def _norm_mod(x, g, shift, scale):
    y = _rms(x, g)
    y = _per_segment(y, scale, lambda a, s: a * (1.0 + s))
    return _per_segment(y, shift, lambda a, s: a + s)


def _ada_kernel(c_ref, w_ref, b_ref, o_ref):
    c = c_ref[...]
    a = (c * jax.nn.sigmoid(c)).astype(BF16)
    o_ref[0] = jnp.dot(a, w_ref[0].astype(BF16), preferred_element_type=F32) + b_ref[0]


def _ada_mod(c_all, w_ada, b_ada):
    depth, d, n6 = w_ada.shape
    bc = c_all.shape[0]
    tn = _pick_tile(n6, (1536, 1024, 512, 128))
    return pl.pallas_call(
        _ada_kernel,
        grid=(depth, n6 // tn),
        in_specs=[
            pl.BlockSpec((bc, d), lambda l, j: (0, 0)),
            pl.BlockSpec((1, d, tn), lambda l, j: (l, 0, j)),
            pl.BlockSpec((1, 1, tn), lambda l, j: (l, 0, j)),
        ],
        out_specs=pl.BlockSpec((1, bc, tn), lambda l, j: (l, 0, j)),
        out_shape=jax.ShapeDtypeStruct((depth, bc, n6), F32),
        compiler_params=pltpu.CompilerParams(vmem_limit_bytes=VMEM_LIMIT),
        name="ada_mod",
    )(c_all, w_ada, b_ada.reshape(depth, 1, n6))


def _bias_kernel(off_ref, q0_ref, len_ref, rb_ref, o_ref, *, tq, tk):
    v = pl.program_id(0)
    h = pl.program_id(1)
    r = lax.broadcasted_iota(I32, (tq, tk), 0)
    c = lax.broadcasted_iota(I32, (tq, tk), 1)
    rel = off_ref[v] + c - r
    qpos = q0_ref[v] + r
    kpos = qpos + rel
    n = jnp.abs(rel)
    big = jnp.full((tq, tk), 8, I32)
    for t in _BUCKET_STEPS:
        big = big + jnp.where(n >= t, 1, 0)
    bucket = jnp.where(n < 8, n, big) + jnp.where(rel > 0, REL_BUCKETS // 2, 0)
    val = jnp.zeros((tq, tk), F32)
    for b in range(REL_BUCKETS):
        val = jnp.where(bucket == b, rb_ref[b, h], val)
    vis = jnp.where((kpos >> 6) <= (qpos >> 6), 1, 0) * jnp.where(kpos < len_ref[0], 1, 0)
    o_ref[0, 0] = jnp.where(vis > 0, val, NEG)


def _bias_tiles(rel_bias, offs, q0s, length, tq, tk):
    nvar = len(offs)
    return pl.pallas_call(
        functools.partial(_bias_kernel, tq=tq, tk=tk),
        grid_spec=pltpu.PrefetchScalarGridSpec(
            num_scalar_prefetch=3,
            grid=(nvar, BIAS_HEADS),
            in_specs=[pl.BlockSpec(memory_space=pltpu.SMEM)],
            out_specs=pl.BlockSpec((1, 1, tq, tk), lambda v, h, *_: (v, h, 0, 0)),
        ),
        out_shape=jax.ShapeDtypeStruct((nvar, BIAS_HEADS, tq, tk), F32),
        name="bias_tiles",
    )(jnp.asarray(offs, I32), jnp.asarray(q0s, I32), jnp.asarray([length], I32), rel_bias)


def _inproj_kernel(x_ref, g_ref, shift_ref, scale_ref, w_ref, *out_refs, cols):
    h = _norm_mod(x_ref[...], g_ref[...], shift_ref[0], scale_ref[0]).astype(BF16)
    for start, width, dests in cols:
        p = jnp.dot(h, w_ref[:, start:start + width], preferred_element_type=F32)
        for out_idx, out_off, mult in dests:
            o_ref = out_refs[out_idx]
            val = p if mult == 1.0 else p * mult
            o_ref[:, out_off:out_off + width] = val.astype(o_ref.dtype)


def _in_proj(x, g, shift, scale, w_bf, cols, out_defs, tm):
    n, d = x.shape
    nt = n // tm
    nseg = tm // CHUNK
    nout = w_bf.shape[1]
    seg_spec = pl.BlockSpec((1, nseg, d), lambda t: (t, 0, 0))
    return pl.pallas_call(
        functools.partial(_inproj_kernel, cols=cols),
        grid=(nt,),
        in_specs=[
            pl.BlockSpec((tm, d), lambda t: (t, 0)),
            pl.BlockSpec((1, d), lambda t: (0, 0)),
            seg_spec, seg_spec,
            pl.BlockSpec((d, nout), lambda t: (0, 0)),
        ],
        out_specs=[pl.BlockSpec((tm, w), lambda t: (t, 0)) for w, _ in out_defs],
        out_shape=[jax.ShapeDtypeStruct((n, w), dt) for w, dt in out_defs],
        compiler_params=pltpu.CompilerParams(vmem_limit_bytes=VMEM_LIMIT),
        name="in_proj",
    )(x, g.reshape(1, d), shift.reshape(nt, nseg, d), scale.reshape(nt, nseg, d), w_bf)


def _outproj_kernel(o_ref, x_ref, gate_ref, w_ref, out_ref):
    r = jnp.dot(o_ref[...], w_ref[...], preferred_element_type=F32)
    out_ref[...] = x_ref[...] + _per_segment(r, gate_ref[0], lambda a, s: a * s)


def _out_proj(o, x, gate, w_bf, tm):
    n, d = x.shape
    kdim = o.shape[1]
    nt = n // tm
    nseg = tm // CHUNK
    return pl.pallas_call(
        _outproj_kernel,
        grid=(nt,),
        in_specs=[
            pl.BlockSpec((tm, kdim), lambda t: (t, 0)),
            pl.BlockSpec((tm, d), lambda t: (t, 0)),
            pl.BlockSpec((1, nseg, d), lambda t: (t, 0, 0)),
            pl.BlockSpec((kdim, d), lambda t: (0, 0)),
        ],
        out_specs=pl.BlockSpec((tm, d), lambda t: (t, 0)),
        out_shape=jax.ShapeDtypeStruct((n, d), F32),
        compiler_params=pltpu.CompilerParams(vmem_limit_bytes=VMEM_LIMIT),
        name="out_proj",
    )(o, x, gate.reshape(nt, nseg, d), w_bf)


def _diff_attn_kernel(it_ref, jt_ref, var_ref, fin_ref, lam_ref, q_ref, k_ref, v_ref, bias_ref, g_ref,
                      o_ref, qb_s, p_s, al_s, m_s, acc_s, *, tq, tk, rc, lam_init):
    del it_ref, var_ref
    t = pl.program_id(1)
    nchunk = 2 * tq // rc

    @pl.when(jt_ref[t] == 0)
    def _():
        m_s[...] = jnp.full(m_s.shape, NEG, F32)
        acc_s[...] = jnp.zeros(acc_s.shape, F32)
        lane = lax.broadcasted_iota(I32, (tq, LANES), 1)
        for h in range(DIFF_HEADS):
            qh = q_ref[0, :, h * LANES:(h + 1) * LANES].astype(F32)
            qb_s[h, 0:tq, :] = jnp.where(lane < DIFF_HEAD_DIM, qh, 0.0).astype(BF16)
            qb_s[h, tq:2 * tq, :] = jnp.where(lane >= DIFF_HEAD_DIM, qh, 0.0).astype(BF16)

    def logits(h):
        return lax.dot_general(qb_s[h], k_ref[0, :, h * LANES:(h + 1) * LANES], _NT, preferred_element_type=F32)

    s_next = logits(0)
    for h in range(DIFF_HEADS):
        s_all = s_next
        if h + 1 < DIFF_HEADS:
            s_next = logits(h + 1)
        for c in range(nchunk):
            rows = slice(c * rc, (c + 1) * rc)
            brow = (c * rc) % tq
            s = s_all[rows] + bias_ref[0, 2 * h + (c * rc) // tq, brow:brow + rc, :]
            m_prev = m_s[h, rows]
            m_new = jnp.maximum(m_prev, jnp.max(s, axis=1, keepdims=True))
            p = jnp.exp(s - jnp.concatenate([m_new] * (tk // LANES), axis=1))
            p_s[h, rows] = p.astype(BF16)
            al_s[h, rows] = jnp.exp(m_prev - m_new)
            m_s[h, rows] = m_new
        vh = v_ref[0, :, h * LANES:(h + 1) * LANES]
        v1 = jnp.concatenate([vh, jnp.ones_like(vh)], axis=1)
        alpha = al_s[h]
        acc_s[h] = (jnp.concatenate([alpha, alpha], axis=1) * acc_s[h]
                    + jnp.dot(p_s[h], v1, preferred_element_type=F32))

    @pl.when(fin_ref[t] == 1)
    def _():
        lp = lam_ref[...]
        lam = (jnp.exp(jnp.sum(lp[0:1] * lp[1:2], axis=1, keepdims=True))
               - jnp.exp(jnp.sum(lp[2:3] * lp[3:4], axis=1, keepdims=True)) + lam_init)
        g = g_ref[...]
        for h in range(DIFF_HEADS):
            o0 = acc_s[h, 0:tq, 0:LANES] / acc_s[h, 0:tq, LANES:2 * LANES]
            o1 = acc_s[h, tq:2 * tq, 0:LANES] / acc_s[h, tq:2 * tq, LANES:2 * LANES]
            o = _rms(o0 - lam * o1, g) * (1.0 - lam_init)
            o_ref[0, :, h * LANES:(h + 1) * LANES] = o.astype(o_ref.dtype)


def _diff_attn(q, k, v, bias, pairs, lam_p, subln_g, lam_init, tq, tk):
    b, sq, dq = q.shape
    it = np.asarray([p[0] for p in pairs], np.int32)
    jt = np.asarray([p[1] for p in pairs], np.int32)
    vt = np.asarray([p[2] for p in pairs], np.int32)
    fin = np.asarray([1 if (n + 1 == len(pairs) or pairs[n + 1][0] != p[0]) else 0
                      for n, p in enumerate(pairs)], np.int32)
    rc = min(tq, 64)
    return pl.pallas_call(
        functools.partial(_diff_attn_kernel, tq=tq, tk=tk, rc=rc, lam_init=lam_init),
        grid_spec=pltpu.PrefetchScalarGridSpec(
            num_scalar_prefetch=4,
            grid=(b, len(pairs)),
            in_specs=[
                pl.BlockSpec((4, DIFF_HEAD_DIM), lambda bb, t, *_: (0, 0)),
                pl.BlockSpec((1, tq, dq), lambda bb, t, it_, jt_, vt_, fin_: (bb, it_[t], 0)),
                pl.BlockSpec((1, tk, dq), lambda bb, t, it_, jt_, vt_, fin_: (bb, jt_[t], 0)),
                pl.BlockSpec((1, tk, dq), lambda bb, t, it_, jt_, vt_, fin_: (bb, jt_[t], 0)),
                pl.BlockSpec((1, BIAS_HEADS, tq, tk), lambda bb, t, it_, jt_, vt_, fin_: (vt_[t], 0, 0, 0)),
                pl.BlockSpec((1, 2 * DIFF_HEAD_DIM), lambda bb, t, *_: (0, 0)),
            ],
            out_specs=pl.BlockSpec((1, tq, dq), lambda bb, t, it_, jt_, vt_, fin_: (bb, it_[t], 0)),
            scratch_shapes=[
                pltpu.VMEM((DIFF_HEADS, 2 * tq, LANES), BF16),
                pltpu.VMEM((DIFF_HEADS, 2 * tq, tk), BF16),
                pltpu.VMEM((DIFF_HEADS, 2 * tq, LANES), F32),
                pltpu.VMEM((DIFF_HEADS, 2 * tq, LANES), F32),
                pltpu.VMEM((DIFF_HEADS, 2 * tq, 2 * LANES), F32),
            ],
        ),
        out_shape=jax.ShapeDtypeStruct((b, sq, dq), BF16),
        compiler_params=pltpu.CompilerParams(vmem_limit_bytes=VMEM_LIMIT),
        name="diff_attn",
    )(jnp.asarray(it), jnp.asarray(jt), jnp.asarray(vt), jnp.asarray(fin), lam_p, q, k, v, bias,
      subln_g.reshape(1, -1))


def _diff_prompt(q, k, v, rel_bias, lam_p, subln_g, lam_init):
    s = q.shape[1]
    tq = tk = 256
    nq = s // tq
    pairs = [(i, j, min(i - j, 2)) for i in range(nq) for j in range(i + 1)]
    bias = _bias_tiles(rel_bias, [-tk * v_ for v_ in range(3)], [2 * tk] * 3, 1 << 30, tq, tk)
    return _diff_attn(q, k, v, bias, pairs, lam_p, subln_g, lam_init, tq, tk)


def _sort_key(x):
    bits = pltpu.bitcast(x, I32)
    return jnp.where(bits < 0, INT_MIN - bits, bits)


def _dsa_kernel(nvis_ref, var_ref, q_ref, qi_ref, wi_ref, ki_ref, k_ref, v_ref, bias_ref, o_ref,
                keys_s, qs_s, p_s, al_s, m_s, acc_s, *, tq, tk, nk, rc, q0, length, topk, nbits):
    i = pl.program_id(1)
    nvis = nvis_ref[i]
    npair = DSA_KV_HEADS // 2
    hpp = 2 * DSA_GROUP
    nchunk = hpp * tq // rc

    for hd in range(DSA_HEADS):
        qs_s[hd // hpp, (hd % hpp) * tq:(hd % hpp + 1) * tq, :] = q_ref[0, :, hd * LANES:(hd + 1) * LANES]
    qi_stack = jnp.concatenate([qi_ref[0, :, u * LANES:(u + 1) * LANES] for u in range(IDX_HEADS // 2)], axis=0)
    wcols = [wi_ref[0, :, h:h + 1] for h in range(IDX_HEADS)]

    r_io = lax.broadcasted_iota(I32, (tq, tk), 0)
    c_io = lax.broadcasted_iota(I32, (tq, tk), 1)
    q_chunk = (q0 + i * tq + r_io) >> 6
    score_scale = IDX_DIM ** -0.5 * IDX_HEADS ** -0.5

    def score_body(j, carry):
        off = pl.multiple_of(j * tk, tk)
        ki_lo = ki_ref[0, pl.ds(off, tk), 0:LANES]
        ki_hi = ki_ref[0, pl.ds(off, tk), LANES:2 * LANES]
        s_even = lax.dot_general(qi_stack, ki_lo, _NT, preferred_element_type=F32)
        s_odd = lax.dot_general(qi_stack, ki_hi, _NT, preferred_element_type=F32)
        score = jnp.zeros((tq, tk), F32)
        for u in range(IDX_HEADS // 2):
            score = score + jnp.maximum(s_even[u * tq:(u + 1) * tq], 0.0) * wcols[2 * u]
            score = score + jnp.maximum(s_odd[u * tq:(u + 1) * tq], 0.0) * wcols[2 * u + 1]
        key = _sort_key(score * score_scale)
        kpos = j * tk + c_io
        vis = jnp.where((kpos >> 6) <= q_chunk, 1, 0) * jnp.where(kpos < length, 1, 0)
        keys_s[j] = jnp.where(vis > 0, key, INT_MIN)
        return carry

    lax.fori_loop(0, nvis, score_body, 0)

    def count_ge(cand):
        below = jnp.broadcast_to(cand - 1, (tq, LANES))

        def body(j, acc):
            kk = keys_s[j]
            for u in range(tk // LANES):
                acc = acc + jnp.where(kk[:, u * LANES:(u + 1) * LANES] > below, 1.0, 0.0)
            return acc

        acc = lax.fori_loop(0, nvis, body, jnp.zeros((tq, LANES), F32))
        return jnp.sum(acc, axis=1, keepdims=True)

    def bit_body(it, ans):
        cand = ans | jnp.left_shift(jnp.int32(1), 31 - it)
        return jnp.where(count_ge(cand ^ INT_MIN) >= topk, cand, ans)

    thr = lax.fori_loop(0, 32, bit_body, jnp.zeros((tq, 1), I32)) ^ INT_MIN
    thr_sel = jnp.maximum(thr, NEG_INF_KEY + 1)
    row_full = thr >= thr_sel
    cnt_ge = count_ge(thr_sel)
    cnt_gt = count_ge(thr_sel + 1)
    excess = jnp.max(jnp.where(row_full, cnt_ge - topk, 0.0))

    @pl.when(excess > 0.0)
    def _():
        quota = topk - cnt_gt

        def count_tied_before(pos):
            def body(j, acc):
                kpos = j * tk + c_io
                hit = jnp.where(keys_s[j] == thr, 1.0, 0.0) * jnp.where(kpos < pos, 1.0, 0.0)
                return acc + hit
            acc = lax.fori_loop(0, nvis, body, jnp.zeros((tq, tk), F32))
            return jnp.sum(acc, axis=1, keepdims=True)

        def pos_body(it, p):
            cand = p | jnp.left_shift(jnp.int32(1), nbits - 1 - it)
            return jnp.where(count_tied_before(cand) < quota, cand, p)

        last = lax.fori_loop(0, nbits, pos_body, jnp.zeros((tq, 1), I32))
        last = jnp.where(row_full, last, nk * tk)

        def demote(j, carry):
            kpos = j * tk + c_io
            kk = keys_s[j]
            drop = jnp.where(kk == thr, 1, 0) * jnp.where(kpos > last, 1, 0)
            keys_s[j] = jnp.where(drop > 0, thr - 1, kk)
            return carry

        lax.fori_loop(0, nvis, demote, 0)

    m_s[...] = jnp.full(m_s.shape, NEG, F32)
    acc_s[...] = jnp.zeros(acc_s.shape, F32)

    def attn_body(j, carry):
        off = pl.multiple_of(j * tk, tk)
        var = var_ref[i * nk + j]
        maskb = jnp.where(keys_s[j] >= thr_sel, 0.0, NEG)
        s_pair = [lax.dot_general(qs_s[pr], k_ref[0, pl.ds(off, tk), pr * LANES:(pr + 1) * LANES], _NT,
                                  preferred_element_type=F32) for pr in range(npair)]
        for pr in range(npair):
            for c in range(nchunk):
                rows = slice(c * rc, (c + 1) * rc)
                hd = pr * hpp + (c * rc) // tq
                brow = (c * rc) % tq
                s = s_pair[pr][rows] + bias_ref[var, hd, brow:brow + rc, :] + maskb[brow:brow + rc]
                m_prev = m_s[pr, rows]
                m_new = jnp.maximum(m_prev, jnp.max(s, axis=1, keepdims=True))
                p = jnp.exp(s - jnp.concatenate([m_new] * (tk // LANES), axis=1))
                p_s[pr, rows] = p.astype(BF16)
                al_s[pr, rows] = jnp.exp(m_prev - m_new)
                m_s[pr, rows] = m_new
            vp = v_ref[0, pl.ds(off, tk), pr * LANES:(pr + 1) * LANES]
            v1 = jnp.concatenate([vp, jnp.ones_like(vp)], axis=1)
            alpha = al_s[pr]
            acc_s[pr] = (jnp.concatenate([alpha, alpha], axis=1) * acc_s[pr]
                         + jnp.dot(p_s[pr], v1, preferred_element_type=F32))
        return carry

    lax.fori_loop(0, nvis, attn_body, 0)

    for hd in range(DSA_HEADS):
        rows = slice((hd % hpp) * tq, (hd % hpp + 1) * tq)
        o = acc_s[hd // hpp, rows, 0:LANES] / acc_s[hd // hpp, rows, LANES:2 * LANES]
        o_ref[0, :, hd * LANES:(hd + 1) * LANES] = o.astype(o_ref.dtype)


def _dsa_attn(q, qi, wi, ki2, k, v, bias, nvis, var, q0, length, topk, tq, tk):
    b, sq, dq = q.shape
    lp = k.shape[1]
    nq = sq // tq
    nk = lp // tk
    nvar = bias.shape[0]
    npair = DSA_KV_HEADS // 2
    prow = 2 * DSA_GROUP * tq
    kernel = functools.partial(_dsa_kernel, tq=tq, tk=tk, nk=nk, rc=min(tq, 64), q0=q0, length=length,
                               topk=topk, nbits=max(1, (lp - 1).bit_length()))
    full = lambda bb, ii, *_: (bb, 0, 0)
    tile = lambda bb, ii, *_: (bb, ii, 0)
    return pl.pallas_call(
        kernel,
        grid_spec=pltpu.PrefetchScalarGridSpec(
            num_scalar_prefetch=2,
            grid=(b, nq),
            in_specs=[
                pl.BlockSpec((1, tq, dq), tile),
                pl.BlockSpec((1, tq, qi.shape[2]), tile),
                pl.BlockSpec((1, tq, LANES), tile),
                pl.BlockSpec((1, lp, 2 * LANES), full),
                pl.BlockSpec((1, lp, k.shape[2]), full),
                pl.BlockSpec((1, lp, v.shape[2]), full),
                pl.BlockSpec((nvar, BIAS_HEADS, tq, tk), lambda bb, ii, *_: (0, 0, 0, 0)),
            ],
            out_specs=pl.BlockSpec((1, tq, dq), tile),
            scratch_shapes=[
                pltpu.VMEM((nk, tq, tk), I32),
                pltpu.VMEM((npair, prow, LANES), BF16),
                pltpu.VMEM((npair, prow, tk), BF16),
                pltpu.VMEM((npair, prow, LANES), F32),
                pltpu.VMEM((npair, prow, LANES), F32),
                pltpu.VMEM((npair, prow, 2 * LANES), F32),
            ],
        ),
        out_shape=jax.ShapeDtypeStruct((b, sq, dq), BF16),
        compiler_params=pltpu.CompilerParams(vmem_limit_bytes=VMEM_LIMIT),
        name="dsa_attn",
    )(jnp.asarray(nvis, I32), jnp.asarray(var, I32).reshape(-1), q, qi, wi, ki2, k, v, bias)


def _dsa_prompt(q, qi, wi, ki2, k, v, rel_bias):
    s = q.shape[1]
    tq, tk = 128, 256
    nq, nk = s // tq, s // tk
    nvis = [i // 2 + 1 for i in range(nq)]
    var = [[min(max(i - 2 * j, 0), 3) for j in range(nk)] for i in range(nq)]
    bias = _bias_tiles(rel_bias, [-tq * v_ for v_ in range(4)], [4 * tq] * 4, 1 << 30, tq, tk)
    return _dsa_attn(q, qi, wi, ki2, k, v, bias, nvis, var, 0, s, min(IDX_TOPK, s // 4), tq, tk)


def _route(lt_s, rb_ref, gt_s):
    rows = [lt_s[e:e + 1, :] for e in range(N_EXPERTS)]
    mx = functools.reduce(jnp.maximum, rows)
    ex = [jnp.exp(r - mx) for r in rows]
    z = functools.reduce(lambda a, b: a + b, ex)
    probs = [x / z for x in ex]
    sel = [probs[e] + rb_ref[e] for e in range(N_EXPERTS)]
    best_val, best_grp = None, None
    for gidx in range(N_GROUPS):
        a, b, c, d = sel[EXPERTS_PER_GROUP * gidx:EXPERTS_PER_GROUP * (gidx + 1)]
        hi1, lo1, hi2, lo2 = jnp.maximum(a, b), jnp.minimum(a, b), jnp.maximum(c, d), jnp.minimum(c, d)
        top2 = jnp.maximum(hi1, hi2) + jnp.maximum(jnp.minimum(hi1, hi2), jnp.maximum(lo1, lo2))
        if gidx == 0:
            best_val, best_grp = top2, jnp.zeros_like(top2, dtype=I32)
        else:
            better = top2 > best_val
            best_grp = jnp.where(better, gidx, best_grp)
            best_val = jnp.where(better, top2, best_val)
    masked = [jnp.where(best_grp == e // EXPERTS_PER_GROUP, sel[e], -jnp.inf) for e in range(N_EXPERTS)]

    def first_argmax(vals):
        bv, bi = vals[0], jnp.zeros_like(best_grp)
        for e in range(1, N_EXPERTS):
            better = vals[e] > bv
            bi = jnp.where(better, e, bi)
            bv = jnp.where(better, vals[e], bv)
        return bi

    i1 = first_argmax(masked)
    i2 = first_argmax([jnp.where(i1 == e, -jnp.inf, masked[e]) for e in range(N_EXPERTS)])
    w1 = functools.reduce(lambda a, b: a + b, [jnp.where(i1 == e, probs[e], 0.0) for e in range(N_EXPERTS)])
    w2 = functools.reduce(lambda a, b: a + b, [jnp.where(i2 == e, probs[e], 0.0) for e in range(N_EXPERTS)])
    wsum = w1 + w2
    w1, w2 = w1 / wsum, w2 / wsum
    for e in range(N_EXPERTS):
        gt_s[e:e + 1, :] = jnp.where(i1 == e, w1, 0.0) + jnp.where(i2 == e, w2, 0.0)


def _moe_kernel(rb_ref, x_ref, g_ref, shift_ref, scale_ref, gate_ref, wrh_ref, wrl_ref, wg_ref, wu_ref, wd_ref,
                gfin_ref, out_ref, hb_s, lt_s, gt_s, gates_s, acc_s, *, final_norm):
    e = pl.program_id(1)

    @pl.when(e == 0)
    def _():
        h = _norm_mod(x_ref[...], g_ref[...], shift_ref[0], scale_ref[0])
        hb = h.astype(BF16)
        hb_s[...] = hb
        hl = (h - hb.astype(F32)).astype(BF16)
        lt_s[...] = (lax.dot_general(wrh_ref[...], hb, _NT, preferred_element_type=F32)
                     + lax.dot_general(wrl_ref[...], hb, _NT, preferred_element_type=F32)
                     + lax.dot_general(wrh_ref[...], hl, _NT, preferred_element_type=F32))
        gt_s[...] = jnp.zeros(gt_s.shape, F32)
        _route(lt_s, rb_ref, gt_s)
        gates_s[...] = gt_s[...].T
        acc_s[...] = jnp.zeros(acc_s.shape, F32)

    hb = hb_s[...]
    up = jnp.dot(hb, wu_ref[0], preferred_element_type=F32)
    gt = jnp.dot(hb, wg_ref[0], preferred_element_type=F32)
    lane = lax.broadcasted_iota(I32, gates_s.shape, 1)
    gcol = jnp.sum(jnp.where(lane == e, gates_s[...], 0.0), axis=1, keepdims=True)
    a = (gt * jax.nn.sigmoid(gt) * up * gcol).astype(BF16)
    acc_s[...] += jnp.dot(a, wd_ref[0], preferred_element_type=F32)

    @pl.when(e == N_EXPERTS - 1)
    def _():
        y = x_ref[...] + _per_segment(acc_s[...], gate_ref[0], lambda v, s: v * s)
        if final_norm:
            y = _rms(y, gfin_ref[...])
        out_ref[...] = y


def _moe(x, g, shift, scale, gate, w_router, router_bias, wg_bf, wu_bf, wd_bf, g_final, final_norm, tm):
    n, d = x.shape
    nt = n // tm
    nseg = tm // CHUNK
    ff = wg_bf.shape[2]
    wrt = w_router.T
    wrh = wrt.astype(BF16)
    wrl = (wrt - wrh.astype(F32)).astype(BF16)
    seg_spec = pl.BlockSpec((1, nseg, d), lambda t, e: (t, 0, 0))
    row_spec = pl.BlockSpec((1, d), lambda t, e: (0, 0))
    return pl.pallas_call(
        functools.partial(_moe_kernel, final_norm=final_norm),
        grid=(nt, N_EXPERTS),
        in_specs=[
            pl.BlockSpec(memory_space=pltpu.SMEM),
            pl.BlockSpec((tm, d), lambda t, e: (t, 0)),
            row_spec, seg_spec, seg_spec, seg_spec,
            pl.BlockSpec((N_EXPERTS, d), lambda t, e: (0, 0)),
            pl.BlockSpec((N_EXPERTS, d), lambda t, e: (0, 0)),
            pl.BlockSpec((1, d, ff), lambda t, e: (e, 0, 0)),
            pl.BlockSpec((1, d, ff), lambda t, e: (e, 0, 0)),
            pl.BlockSpec((1, ff, d), lambda t, e: (e, 0, 0)),
            row_spec,
        ],
        out_specs=pl.BlockSpec((tm, d), lambda t, e: (t, 0)),
        out_shape=jax.ShapeDtypeStruct((n, d), F32),
        scratch_shapes=[
            pltpu.VMEM((tm, d), BF16),
            pltpu.VMEM((N_EXPERTS, tm), F32),
            pltpu.VMEM((LANES, tm), F32),
            pltpu.VMEM((tm, LANES), F32),
            pltpu.VMEM((tm, d), F32),
        ],
        compiler_params=pltpu.CompilerParams(vmem_limit_bytes=VMEM_LIMIT),
        name="moe",
    )(router_bias, x, g.reshape(1, d), shift.reshape(nt, nseg, d), scale.reshape(nt, nseg, d),
      gate.reshape(nt, nseg, d), wrh, wrl, wg_bf, wu_bf, wd_bf, g_final.reshape(1, d))


def _dsa_in_weights(w):
    d = w.shape[0]
    o1 = DSA_HEADS * DSA_HEAD_DIM
    o2 = o1 + DSA_KV_HEADS * DSA_HEAD_DIM
    o3 = o2 + DSA_KV_HEADS * DSA_HEAD_DIM
    o4 = o3 + IDX_HEADS * IDX_DIM
    o5 = o4 + IDX_DIM
    wq = w[:, :o1].reshape(d, DSA_HEADS, 1, DSA_HEAD_DIM)
    half = ((np.arange(DSA_HEADS) // DSA_GROUP) % 2).reshape(1, DSA_HEADS, 1, 1)
    wq = jnp.concatenate([jnp.where(half == 0, wq, 0.0), jnp.where(half == 1, wq, 0.0)], axis=2)
    wki = w[:, o4:o5]
    zk = jnp.zeros_like(wki)
    wwi = jnp.pad(w[:, o5:], ((0, 0), (0, LANES - IDX_HEADS)))
    return jnp.concatenate([wq.reshape(d, 2 * o1), w[:, o1:o4], wki, zk, zk, wki, wwi], axis=1).astype(BF16)


def _dsa_out_weights(w):
    d = w.shape[1]
    w4 = w.reshape(DSA_HEADS, 1, DSA_HEAD_DIM, d)
    half = ((np.arange(DSA_HEADS) // DSA_GROUP) % 2).reshape(DSA_HEADS, 1, 1, 1)
    w4 = jnp.concatenate([jnp.where(half == 0, w4, 0.0), jnp.where(half == 1, w4, 0.0)], axis=1)
    return w4.reshape(2 * DSA_HEADS * DSA_HEAD_DIM, d).astype(BF16)


def _pad_keys(x, lp):
    return jnp.pad(x, ((0, 0), (0, lp - x.shape[1]), (0, 0)))


def kernel(x_prompt, x_sample, cache_k_diff, cache_v_diff, cache_k_dsa, cache_v_dsa, cache_kidx_dsa,
           c_prompt, c_sample, rel_bias, w_in_diff, w_out_diff, lam_diff, subln_g_diff, w_in_dsa, w_out_dsa,
           w_ada, b_ada, g_norm_mix, g_norm_ffn, w_router, router_bias, w_gate, w_up, w_down, g_final):
    b, s, d = x_prompt.shape
    bs, ss, _ = x_sample.shape
    past = cache_k_diff.shape[2]
    depth = w_ada.shape[0]
    n_p, n_s = b * s, bs * ss
    n = n_p + n_s
    assert d == D_MODEL and ss == CHUNK and s % 256 == 0 and past % CHUNK == 0 and depth == 2
    tm = _pick_tile(n, (512, 256, 128, 64))
    tm_moe = _pick_tile(n, (1024, 512, 256, 128))
    l_s = past + ss
    lp_s = -(-l_s // LANES) * LANES

    x = jnp.concatenate([x_prompt.reshape(n_p, d), x_sample.reshape(n_s, d)], axis=0)
    mod = _ada_mod(jnp.concatenate([c_prompt, c_sample], axis=0), w_ada, b_ada)
    seg_batch = np.concatenate([np.repeat(np.arange(b), s // CHUNK), b + np.arange(bs)])

    def mods(layer):
        m = mod[layer][seg_batch]
        return [m[:, i * d:(i + 1) * d] for i in range(6)]

    def split(a, width):
        return a[:n_p].reshape(b, s, width), a[n_p:].reshape(bs, ss, width)

    m0 = mods(0)
    qk = DIFF_HEADS * 2 * DIFF_HEAD_DIM
    cols = ((0, qk, ((0, 0, DIFF_HEAD_DIM ** -0.5),)),
            (qk, qk, ((1, 0, 1.0), (3, 0, 1.0))),
            (2 * qk, qk, ((2, 0, 1.0), (4, 0, 1.0))))
    outs = ((qk, BF16), (qk, F32), (qk, F32), (qk, BF16), (qk, BF16))
    q_bf, k_f, v_f, k_bf, v_bf = _in_proj(x, g_norm_mix[0], m0[0], m0[1], w_in_diff[0].astype(BF16), cols, outs, tm)
    kd_p, kd_s = split(k_f, qk)
    vd_p, vd_s = split(v_f, qk)
    q_p, q_s = split(q_bf, qk)
    kb_p, kb_s = split(k_bf, qk)
    vb_p, vb_s = split(v_bf, qk)
    lam_init = 0.8 - 0.6 * float(np.exp(-0.3 * 0))

    o_p = _diff_prompt(q_p, kb_p, vb_p, rel_bias, lam_diff[0], subln_g_diff[0], lam_init)

    ka_s = _pad_keys(jnp.concatenate([cache_k_diff[0].reshape(bs, past, qk).astype(BF16), kb_s], axis=1), lp_s)
    va_s = _pad_keys(jnp.concatenate([cache_v_diff[0].reshape(bs, past, qk).astype(BF16), vb_s], axis=1), lp_s)
    bias_s = _bias_tiles(rel_bias, [-past], [past], l_s, ss, lp_s)
    o_s = _diff_attn(q_s, ka_s, va_s, bias_s, [(0, 0, 0)], lam_diff[0], subln_g_diff[0], lam_init, ss, lp_s)

    o = jnp.concatenate([o_p.reshape(n_p, qk), o_s.reshape(n_s, qk)], axis=0)
    x = _out_proj(o, x, m0[2], w_out_diff[0].astype(BF16), tm)
    x = _moe(x, g_norm_ffn[0], m0[3], m0[4], m0[5], w_router, router_bias, w_gate[0].astype(BF16),
             w_up[0].astype(BF16), w_down[0].astype(BF16), g_final, False, tm_moe)

    m1 = mods(1)
    nq_w = 2 * DSA_HEADS * DSA_HEAD_DIM
    kv_w = DSA_KV_HEADS * DSA_HEAD_DIM
    qi_w = IDX_HEADS * IDX_DIM
    c_k, c_v, c_qi, c_ki, c_wi = nq_w, nq_w + kv_w, nq_w + 2 * kv_w, nq_w + 2 * kv_w + qi_w, nq_w + 2 * kv_w + qi_w + 2 * LANES
    half_q = nq_w // 2
    cols = ((0, half_q, ((0, 0, DSA_HEAD_DIM ** -0.5),)),
            (half_q, half_q, ((0, half_q, DSA_HEAD_DIM ** -0.5),)),
            (c_k, kv_w, ((1, 0, 1.0), (3, 0, 1.0))),
            (c_v, kv_w, ((2, 0, 1.0), (4, 0, 1.0))),
            (c_qi, qi_w, ((5, 0, 1.0),)),
            (c_ki, 2 * LANES, ((6, 0, 1.0), (7, 0, 1.0))),
            (c_wi, LANES, ((8, 0, 1.0),)))
    outs = ((nq_w, BF16), (kv_w, F32), (kv_w, F32), (kv_w, BF16), (kv_w, BF16), (qi_w, BF16),
            (2 * LANES, F32), (2 * LANES, BF16), (LANES, F32))
    q2, k2_f, v2_f, k2_bf, v2_bf, qi_bf, ki2_f, ki2_bf, wi_f = _in_proj(
        x, g_norm_mix[1], m1[0], m1[1], _dsa_in_weights(w_in_dsa[0]), cols, outs, tm)
    ks_p, ks_s = split(k2_f, kv_w)
    vs_p, vs_s = split(v2_f, kv_w)
    ki_p, ki_s = split(ki2_f[:, :IDX_DIM], IDX_DIM)
    q2_p, q2_s = split(q2, nq_w)
    k2b_p, k2b_s = split(k2_bf, kv_w)
    v2b_p, v2b_s = split(v2_bf, kv_w)
    qi_p, qi_s = split(qi_bf, qi_w)
    kib_p, kib_s = split(ki2_bf, 2 * LANES)
    wi_p, wi_s = split(wi_f, LANES)

    o2_p = _dsa_prompt(q2_p, qi_p, wi_p, kib_p, k2b_p, v2b_p, rel_bias)

    ck = cache_k_dsa[0].reshape(bs, past, kv_w).astype(BF16)
    cv = cache_v_dsa[0].reshape(bs, past, kv_w).astype(BF16)
    cki = cache_kidx_dsa[0].astype(BF16)
    zki = jnp.zeros_like(cki)
    ka2 = _pad_keys(jnp.concatenate([ck, k2b_s], axis=1), lp_s)
    va2 = _pad_keys(jnp.concatenate([cv, v2b_s], axis=1), lp_s)
    kia = _pad_keys(jnp.concatenate([jnp.concatenate([cki, zki, zki, cki], axis=2), kib_s], axis=1), lp_s)
    tk_s = LANES
    nk_s = lp_s // tk_s
    bias_s2 = _bias_tiles(rel_bias, [j * tk_s - past for j in range(nk_s)], [past] * nk_s, l_s, ss, tk_s)
    o2_s = _dsa_attn(q2_s, qi_s, wi_s, kia, ka2, va2, bias_s2, [nk_s], [list(range(nk_s))], past, l_s,
                     min(IDX_TOPK, l_s // 4), ss, tk_s)

    o2 = jnp.concatenate([o2_p.reshape(n_p, nq_w), o2_s.reshape(n_s, nq_w)], axis=0)
    x = _out_proj(o2, x, m1[2], _dsa_out_weights(w_out_dsa[0]), tm)
    y = _moe(x, g_norm_ffn[1], m1[3], m1[4], m1[5], w_router, router_bias, w_gate[1].astype(BF16),
             w_up[1].astype(BF16), w_down[1].astype(BF16), g_final, True, tm_moe)

    h2 = (DIFF_HEADS, 2, DIFF_HEAD_DIM)
    hv = (DIFF_HEADS, 2 * DIFF_HEAD_DIM)
    hk = (DSA_KV_HEADS, DSA_HEAD_DIM)
    return (y[:n_p].reshape(b, s, d), y[n_p:].reshape(bs, ss, d),
            kd_p.reshape(1, b, s, *h2), vd_p.reshape(1, b, s, *hv),
            ks_p.reshape(1, b, s, *hk), vs_p.reshape(1, b, s, *hk), ki_p.reshape(1, b, s, IDX_DIM),
            kd_s.reshape(1, bs, ss, *h2), vd_s.reshape(1, bs, ss, *hv),
            ks_s.reshape(1, bs, ss, *hk), vs_s.reshape(1, bs, ss, *hk), ki_s.reshape(1, bs, ss, IDX_DIM))
```

```python
import functools

import jax
import jax.numpy as jnp
import numpy as np
from jax import lax
from jax.experimental import pallas as pl
from jax.experimental.pallas import tpu as pltpu

F32 = jnp.float32
BF16 = jnp.bfloat16
I32 = jnp.int32

D_MODEL = 1024
CHUNK = 64
DIFF_HEADS = 8
DIFF_HEAD_DIM = 64
DSA_HEADS = 16
DSA_KV_HEADS = 4
DSA_GROUP = DSA_HEADS // DSA_KV_HEADS
DSA_HEAD_DIM = 64
IDX_HEADS = 8
IDX_DIM = 64
IDX_TOPK = 256
REL_BUCKETS = 32
BIAS_HEADS = 16
N_EXPERTS = 16
N_GROUPS = 4
EXPERTS_PER_GROUP = N_EXPERTS // N_GROUPS
EXPERT_FF = 512
RMS_EPS = 1e-6

LANES = 128
NEG = -1e30
LOG2E = 1.4426950408889634
INT_MIN = -(2 ** 31)
NEG_INF_KEY = -0x7F800000
VMEM_LIMIT = 56 * 1024 * 1024

_BUCKET_STEPS = (12, 16, 23, 32, 46, 64, 91)
FAR_BLOCKS = 2

_NT = (((1,), (1,)), ((), ()))


def _pick_tile(n, prefs):
    for t in prefs:
        if n % t == 0:
            return t
    raise ValueError(f"no tile in {prefs} divides {n}")


def _rms(x, g):
    ms = jnp.mean(x * x, axis=-1, keepdims=True)
    return x * lax.rsqrt(ms + RMS_EPS) * g


def _per_segment(x, seg_vals, op):
    tm, d = x.shape
    nseg = tm // CHUNK
    return op(x.reshape(nseg, CHUNK, d), seg_vals[:, None, :]).reshape(tm, d)


def _norm_mod(x, g, shift, scale):
    y = _rms(x, g)
    y = _per_segment(y, scale, lambda a, s: a * (1.0 + s))
    return _per_segment(y, shift, lambda a, s: a + s)


def _ada_kernel(c_ref, w_ref, b_ref, o_ref):
    c = c_ref[...]
    a = (c * jax.nn.sigmoid(c)).astype(BF16)
    o_ref[0] = jnp.dot(a, w_ref[0].astype(BF16), preferred_element_type=F32) + b_ref[0]


def _ada_mod(c_all, w_ada, b_ada):
    depth, d, n6 = w_ada.shape
    bc = c_all.shape[0]
    tn = _pick_tile(n6, (1536, 1024, 512, 128))
    return pl.pallas_call(
        _ada_kernel,
        grid=(depth, n6 // tn),
        in_specs=[
            pl.BlockSpec((bc, d), lambda l, j: (0, 0)),
            pl.BlockSpec((1, d, tn), lambda l, j: (l, 0, j)),
            pl.BlockSpec((1, 1, tn), lambda l, j: (l, 0, j)),
        ],
        out_specs=pl.BlockSpec((1, bc, tn), lambda l, j: (l, 0, j)),
        out_shape=jax.ShapeDtypeStruct((depth, bc, n6), F32),
        compiler_params=pltpu.CompilerParams(vmem_limit_bytes=VMEM_LIMIT),
        name="ada_mod",
    )(c_all, w_ada, b_ada.reshape(depth, 1, n6))


def _bias_kernel(off_ref, q0_ref, len_ref, rb_ref, o_ref, *, tq, tk):
    v = pl.program_id(0)
    h = pl.program_id(1)
    r = lax.broadcasted_iota(I32, (tq, tk), 0)
    c = lax.broadcasted_iota(I32, (tq, tk), 1)
    rel = off_ref[v] + c - r
    qpos = q0_ref[v] + r
    kpos = qpos + rel
    n = jnp.abs(rel)
    big = jnp.full((tq, tk), 8, I32)
    for t in _BUCKET_STEPS:
        big = big + jnp.where(n >= t, 1, 0)
    bucket = jnp.where(n < 8, n, big) + jnp.where(rel > 0, REL_BUCKETS // 2, 0)
    val = jnp.zeros((tq, tk), F32)
    for b in range(REL_BUCKETS):
        val = jnp.where(bucket == b, rb_ref[b, h], val)
    vis = jnp.where((kpos >> 6) <= (qpos >> 6), 1, 0) * jnp.where(kpos < len_ref[0], 1, 0)
    o_ref[0, 0] = jnp.where(vis > 0, val * LOG2E, NEG)


def _bias_tiles(rel_bias, offs, q0s, length, tq, tk):
    nvar = len(offs)
    return pl.pallas_call(
        functools.partial(_bias_kernel, tq=tq, tk=tk),
        grid_spec=pltpu.PrefetchScalarGridSpec(
            num_scalar_prefetch=3,
            grid=(nvar, BIAS_HEADS),
            in_specs=[pl.BlockSpec(memory_space=pltpu.SMEM)],
            out_specs=pl.BlockSpec((1, 1, tq, tk), lambda v, h, *_: (v, h, 0, 0)),
        ),
        out_shape=jax.ShapeDtypeStruct((nvar, BIAS_HEADS, tq, tk), F32),
        name="bias_tiles",
    )(jnp.asarray(offs, I32), jnp.asarray(q0s, I32), jnp.asarray([length], I32), rel_bias)


def _inproj_kernel(x_ref, g_ref, shift_ref, scale_ref, w_ref, *out_refs, cols):
    h = _norm_mod(x_ref[...], g_ref[...], shift_ref[0], scale_ref[0]).astype(BF16)
    for start, width, dests in cols:
        p = jnp.dot(h, w_ref[:, start:start + width], preferred_element_type=F32)
        for out_idx, out_off, mult in dests:
            o_ref = out_refs[out_idx]
            val = p if mult == 1.0 else p * mult
            o_ref[:, out_off:out_off + width] = val.astype(o_ref.dtype)


def _in_proj(x, g, shift, scale, w_bf, cols, out_defs, tm):
    n, d = x.shape
    nt = n // tm
    nseg = tm // CHUNK
    nout = w_bf.shape[1]
    seg_spec = pl.BlockSpec((1, nseg, d), lambda t: (t, 0, 0))
    return pl.pallas_call(
        functools.partial(_inproj_kernel, cols=cols),
        grid=(nt,),
        in_specs=[
            pl.BlockSpec((tm, d), lambda t: (t, 0)),
            pl.BlockSpec((1, d), lambda t: (0, 0)),
            seg_spec, seg_spec,
            pl.BlockSpec((d, nout), lambda t: (0, 0)),
        ],
        out_specs=[pl.BlockSpec((tm, w), lambda t: (t, 0)) for w, _ in out_defs],
        out_shape=[jax.ShapeDtypeStruct((n, w), dt) for w, dt in out_defs],
        compiler_params=pltpu.CompilerParams(vmem_limit_bytes=VMEM_LIMIT),
        name="in_proj",
    )(x, g.reshape(1, d), shift.reshape(nt, nseg, d), scale.reshape(nt, nseg, d), w_bf)


def _outproj_kernel(o_ref, x_ref, gate_ref, w_ref, out_ref):
    r = jnp.dot(o_ref[...], w_ref[...], preferred_element_type=F32)
    out_ref[...] = x_ref[...] + _per_segment(r, gate_ref[0], lambda a, s: a * s)


def _out_proj(o, x, gate, w_bf, tm):
    n, d = x.shape
    kdim = o.shape[1]
    nt = n // tm
    nseg = tm // CHUNK
    return pl.pallas_call(
        _outproj_kernel,
        grid=(nt,),
        in_specs=[
            pl.BlockSpec((tm, kdim), lambda t: (t, 0)),
            pl.BlockSpec((tm, d), lambda t: (t, 0)),
            pl.BlockSpec((1, nseg, d), lambda t: (t, 0, 0)),
            pl.BlockSpec((kdim, d), lambda t: (0, 0)),
        ],
        out_specs=pl.BlockSpec((tm, d), lambda t: (t, 0)),
        out_shape=jax.ShapeDtypeStruct((n, d), F32),
        compiler_params=pltpu.CompilerParams(vmem_limit_bytes=VMEM_LIMIT),
        name="out_proj",
    )(o, x, gate.reshape(nt, nseg, d), w_bf)


def _diff_attn_kernel(it_ref, jt_ref, var_ref, fin_ref, lam_ref, q_ref, k_ref, v_ref, bias_ref, g_ref,
                      o_ref, qb_s, p_s, al_s, m_s, acc_s, *, tq, tk, rc, nkb, lam_init):
    t = pl.program_id(1)
    nchunk = 2 * tq // rc
    ncb = tk // LANES
    bvar = [var_ref[it_ref[t] * nkb + jt_ref[t] * ncb + cb] for cb in range(ncb)]

    @pl.when(jt_ref[t] == 0)
    def _():
        m_s[...] = jnp.full(m_s.shape, NEG, F32)
        acc_s[...] = jnp.zeros(acc_s.shape, F32)
        lane = lax.broadcasted_iota(I32, (tq, LANES), 1)
        for h in range(DIFF_HEADS):
            qh = q_ref[0, :, h * LANES:(h + 1) * LANES].astype(F32)
            qb_s[h, 0:tq, :] = jnp.where(lane < DIFF_HEAD_DIM, qh, 0.0).astype(BF16)
            qb_s[h, tq:2 * tq, :] = jnp.where(lane >= DIFF_HEAD_DIM, qh, 0.0).astype(BF16)

    def logits(h):
        return lax.dot_general(qb_s[h], k_ref[0, :, h * LANES:(h + 1) * LANES], _NT, preferred_element_type=F32)

    s_next = logits(0)
    for h in range(DIFF_HEADS):
        s_all = s_next
        if h + 1 < DIFF_HEADS:
            s_next = logits(h + 1)
        for c in range(nchunk):
            rows = slice(c * rc, (c + 1) * rc)
            brow = (c * rc) % tq
            hm = 2 * h + (c * rc) // tq
            s = jnp.concatenate(
                [s_all[rows, cb * LANES:(cb + 1) * LANES] + bias_ref[bvar[cb], hm, brow:brow + rc, :]
                 for cb in range(ncb)], axis=1)
            m_prev = m_s[h, rows]
            m_new = jnp.maximum(m_prev, jnp.max(s, axis=1, keepdims=True))
            p = jnp.exp2(s - jnp.concatenate([m_new] * ncb, axis=1))
            p_s[h, rows] = p.astype(BF16)
            al_s[h, rows] = jnp.exp2(m_prev - m_new)
            m_s[h, rows] = m_new
        vh = v_ref[0, :, h * LANES:(h + 1) * LANES]
        v1 = jnp.concatenate([vh, jnp.ones_like(vh)], axis=1)
        alpha = al_s[h]
        acc_s[h] = (jnp.concatenate([alpha, alpha], axis=1) * acc_s[h]
                    + jnp.dot(p_s[h], v1, preferred_element_type=F32))

    @pl.when(fin_ref[t] == 1)
    def _():
        lp = lam_ref[...]
        lam = (jnp.exp(jnp.sum(lp[0:1] * lp[1:2], axis=1, keepdims=True))
               - jnp.exp(jnp.sum(lp[2:3] * lp[3:4], axis=1, keepdims=True)) + lam_init)
        g = g_ref[...]
        for h in range(DIFF_HEADS):
            o0 = acc_s[h, 0:tq, 0:LANES] / acc_s[h, 0:tq, LANES:2 * LANES]
            o1 = acc_s[h, tq:2 * tq, 0:LANES] / acc_s[h, tq:2 * tq, LANES:2 * LANES]
            o = _rms(o0 - lam * o1, g) * (1.0 - lam_init)
            o_ref[0, :, h * LANES:(h + 1) * LANES] = o.astype(o_ref.dtype)


def _diff_attn(q, k, v, bias, pairs, var, lam_p, subln_g, lam_init, tq, tk):
    b, sq, dq = q.shape
    nvar = bias.shape[0]
    nkb = k.shape[1] // LANES
    it = np.asarray([p[0] for p in pairs], np.int32)
    jt = np.asarray([p[1] for p in pairs], np.int32)
    vt = np.asarray(var, np.int32).reshape(-1)
    fin = np.asarray([1 if (n + 1 == len(pairs) or pairs[n + 1][0] != p[0]) else 0
                      for n, p in enumerate(pairs)], np.int32)
    rc = min(tq, 64)
    return pl.pallas_call(
        functools.partial(_diff_attn_kernel, tq=tq, tk=tk, rc=rc, nkb=nkb, lam_init=lam_init),
        grid_spec=pltpu.PrefetchScalarGridSpec(
            num_scalar_prefetch=4,
            grid=(b, len(pairs)),
            in_specs=[
                pl.BlockSpec((4, DIFF_HEAD_DIM), lambda bb, t, *_: (0, 0)),
                pl.BlockSpec((1, tq, dq), lambda bb, t, it_, jt_, vt_, fin_: (bb, it_[t], 0)),
                pl.BlockSpec((1, tk, dq), lambda bb, t, it_, jt_, vt_, fin_: (bb, jt_[t], 0)),
                pl.BlockSpec((1, tk, dq), lambda bb, t, it_, jt_, vt_, fin_: (bb, jt_[t], 0)),
                pl.BlockSpec((nvar, BIAS_HEADS, tq, LANES), lambda bb, t, *_: (0, 0, 0, 0),
                             pipeline_mode=pl.Buffered(1)),
                pl.BlockSpec((1, 2 * DIFF_HEAD_DIM), lambda bb, t, *_: (0, 0)),
            ],
            out_specs=pl.BlockSpec((1, tq, dq), lambda bb, t, it_, jt_, vt_, fin_: (bb, it_[t], 0)),
            scratch_shapes=[
                pltpu.VMEM((DIFF_HEADS, 2 * tq, LANES), BF16),
                pltpu.VMEM((DIFF_HEADS, 2 * tq, tk), BF16),
                pltpu.VMEM((DIFF_HEADS, 2 * tq, LANES), F32),
                pltpu.VMEM((DIFF_HEADS, 2 * tq, LANES), F32),
                pltpu.VMEM((DIFF_HEADS, 2 * tq, 2 * LANES), F32),
            ],
        ),
        out_shape=jax.ShapeDtypeStruct((b, sq, dq), BF16),
        compiler_params=pltpu.CompilerParams(vmem_limit_bytes=VMEM_LIMIT),
        name="diff_attn",
    )(jnp.asarray(it), jnp.asarray(jt), jnp.asarray(vt), jnp.asarray(fin), lam_p, q, k, v, bias,
      subln_g.reshape(1, -1))


def _block_bias(rel_bias, nq, nkb, tq):
    bpq = tq // LANES
    lo, hi = -bpq, FAR_BLOCKS
    bias = _bias_tiles(rel_bias, [-LANES * d for d in range(lo, hi + 1)], [4 * tq] * (hi - lo + 1), 1 << 30, tq, LANES)
    var = [[min(max(bpq * i - jb, lo), hi) - lo for jb in range(nkb)] for i in range(nq)]
    return bias, var


def _diff_prompt(q, k, v, rel_bias, lam_p, subln_g, lam_init):
    s = q.shape[1]
    tq, tk = 256, 512
    nq = s // tq
    pairs = [(i, j) for i in range(nq) for j in range((i * tq) // tk + 1)]
    bias, var = _block_bias(rel_bias, nq, s // LANES, tq)
    return _diff_attn(q, k, v, bias, pairs, var, lam_p, subln_g, lam_init, tq, tk)


def _sort_key(x):
    bits = pltpu.bitcast(x, I32)
    return jnp.where(bits < 0, INT_MIN - bits, bits)


def _dsa_kernel(nvis_ref, var_ref, q_ref, qi_ref, wi_ref, ki_ref, k_ref, v_ref, bias_ref, o_ref,
                keys_s, qs_s, p_s, al_s, m_s, acc_s, *, tq, tk, nk, rc, q0, length, topk, nbits):
    i = pl.program_id(1)
    nvis = nvis_ref[i]
    npair = DSA_KV_HEADS // 2
    hpp = 2 * DSA_GROUP
    nchunk = hpp * tq // rc

    for hd in range(DSA_HEADS):
        qs_s[hd // hpp, (hd % hpp) * tq:(hd % hpp + 1) * tq, :] = q_ref[0, :, hd * LANES:(hd + 1) * LANES]
    qi_stack = jnp.concatenate([qi_ref[0, :, u * LANES:(u + 1) * LANES] for u in range(IDX_HEADS // 2)], axis=0)
    wcols = [wi_ref[0, :, h:h + 1] for h in range(IDX_HEADS)]

    r_io = lax.broadcasted_iota(I32, (tq, tk), 0)
    c_io = lax.broadcasted_iota(I32, (tq, tk), 1)
    q_chunk = (q0 + i * tq + r_io) >> 6
    score_scale = IDX_DIM ** -0.5 * IDX_HEADS ** -0.5

    def score_body(j, carry):
        off = pl.multiple_of(j * tk, tk)
        ki_lo = ki_ref[0, pl.ds(off, tk), 0:LANES]
        ki_hi = ki_ref[0, pl.ds(off, tk), LANES:2 * LANES]
        s_even = lax.dot_general(qi_stack, ki_lo, _NT, preferred_element_type=F32)
        s_odd = lax.dot_general(qi_stack, ki_hi, _NT, preferred_element_type=F32)
        score = jnp.zeros((tq, tk), F32)
        for u in range(IDX_HEADS // 2):
            score = score + jnp.maximum(s_even[u * tq:(u + 1) * tq], 0.0) * wcols[2 * u]
            score = score + jnp.maximum(s_odd[u * tq:(u + 1) * tq], 0.0) * wcols[2 * u + 1]
        key = _sort_key(score * score_scale)
        kpos = j * tk + c_io
        vis = jnp.where((kpos >> 6) <= q_chunk, 1, 0) * jnp.where(kpos < length, 1, 0)
        keys_s[j] = jnp.where(vis > 0, key, INT_MIN)
        return carry

    lax.fori_loop(0, nvis, score_body, 0)

    def count_ge(cand):
        below = jnp.broadcast_to(cand - 1, (tq, LANES))

        def body(j, acc):
            kk = keys_s[j]
            for u in range(tk // LANES):
                acc = acc + jnp.where(kk[:, u * LANES:(u + 1) * LANES] > below, 1.0, 0.0)
            return acc

        acc = lax.fori_loop(0, nvis, body, jnp.zeros((tq, LANES), F32))
        return jnp.sum(acc, axis=1, keepdims=True)

    def bit_body(it, ans):
        cand = ans | jnp.left_shift(jnp.int32(1), 31 - it)
        return jnp.where(count_ge(cand ^ INT_MIN) >= topk, cand, ans)

    thr = lax.fori_loop(0, 32, bit_body, jnp.zeros((tq, 1), I32)) ^ INT_MIN
    thr_sel = jnp.maximum(thr, NEG_INF_KEY + 1)
    row_full = thr >= thr_sel
    cnt_ge = count_ge(thr_sel)
    cnt_gt = count_ge(thr_sel + 1)
    excess = jnp.max(jnp.where(row_full, cnt_ge - topk, 0.0))

    @pl.when(excess > 0.0)
    def _():
        quota = topk - cnt_gt

        def count_tied_before(pos):
            def body(j, acc):
                kpos = j * tk + c_io
                hit = jnp.where(keys_s[j] == thr, 1.0, 0.0) * jnp.where(kpos < pos, 1.0, 0.0)
                return acc + hit
            acc = lax.fori_loop(0, nvis, body, jnp.zeros((tq, tk), F32))
            return jnp.sum(acc, axis=1, keepdims=True)

        def pos_body(it, p):
            cand = p | jnp.left_shift(jnp.int32(1), nbits - 1 - it)
            return jnp.where(count_tied_before(cand) < quota, cand, p)

        last = lax.fori_loop(0, nbits, pos_body, jnp.zeros((tq, 1), I32))
        last = jnp.where(row_full, last, nk * tk)

        def demote(j, carry):
            kpos = j * tk + c_io
            kk = keys_s[j]
            drop = jnp.where(kk == thr, 1, 0) * jnp.where(kpos > last, 1, 0)
            keys_s[j] = jnp.where(drop > 0, thr - 1, kk)
            return carry

        lax.fori_loop(0, nvis, demote, 0)

    m_s[...] = jnp.full(m_s.shape, NEG, F32)
    acc_s[...] = jnp.zeros(acc_s.shape, F32)

    ncb = tk // LANES

    def attn_body(j, carry):
        off = pl.multiple_of(j * tk, tk)
        bvar = [var_ref[(i * nk + j) * ncb + cb] for cb in range(ncb)]
        maskb = jnp.where(keys_s[j] >= thr_sel, 0.0, NEG)
        s_pair = [lax.dot_general(qs_s[pr], k_ref[0, pl.ds(off, tk), pr * LANES:(pr + 1) * LANES], _NT,
                                  preferred_element_type=F32) for pr in range(npair)]
        for pr in range(npair):
            for c in range(nchunk):
                rows = slice(c * rc, (c + 1) * rc)
                hd = pr * hpp + (c * rc) // tq
                brow = (c * rc) % tq
                s = jnp.concatenate(
                    [s_pair[pr][rows, cb * LANES:(cb + 1) * LANES] + bias_ref[bvar[cb], hd, brow:brow + rc, :]
                     for cb in range(ncb)], axis=1) + maskb[brow:brow + rc]
                m_prev = m_s[pr, rows]
                m_new = jnp.maximum(m_prev, jnp.max(s, axis=1, keepdims=True))
                p = jnp.exp2(s - jnp.concatenate([m_new] * ncb, axis=1))
                p_s[pr, rows] = p.astype(BF16)
                al_s[pr, rows] = jnp.exp2(m_prev - m_new)
                m_s[pr, rows] = m_new
            vp = v_ref[0, pl.ds(off, tk), pr * LANES:(pr + 1) * LANES]
            v1 = jnp.concatenate([vp, jnp.ones_like(vp)], axis=1)
            alpha = al_s[pr]
            acc_s[pr] = (jnp.concatenate([alpha, alpha], axis=1) * acc_s[pr]
                         + jnp.dot(p_s[pr], v1, preferred_element_type=F32))
        return carry

    lax.fori_loop(0, nvis, attn_body, 0)

    for hd in range(DSA_HEADS):
        rows = slice((hd % hpp) * tq, (hd % hpp + 1) * tq)
        o = acc_s[hd // hpp, rows, 0:LANES] / acc_s[hd // hpp, rows, LANES:2 * LANES]
        o_ref[0, :, hd * LANES:(hd + 1) * LANES] = o.astype(o_ref.dtype)


def _dsa_attn(q, qi, wi, ki2, k, v, bias, nvis, var, q0, length, topk, tq, tk):
    b, sq, dq = q.shape
    lp = k.shape[1]
    nq = sq // tq
    nk = lp // tk
    nvar = bias.shape[0]
    npair = DSA_KV_HEADS // 2
    prow = 2 * DSA_GROUP * tq
    kernel = functools.partial(_dsa_kernel, tq=tq, tk=tk, nk=nk, rc=min(tq, 64), q0=q0, length=length,
                               topk=topk, nbits=max(1, (lp - 1).bit_length()))
    full = lambda bb, ii, *_: (bb, 0, 0)
    tile = lambda bb, ii, *_: (bb, ii, 0)
    return pl.pallas_call(
        kernel,
        grid_spec=pltpu.PrefetchScalarGridSpec(
            num_scalar_prefetch=2,
            grid=(b, nq),
            in_specs=[
                pl.BlockSpec((1, tq, dq), tile),
                pl.BlockSpec((1, tq, qi.shape[2]), tile),
                pl.BlockSpec((1, tq, LANES), tile),
                pl.BlockSpec((1, lp, 2 * LANES), full),
                pl.BlockSpec((1, lp, k.shape[2]), full),
                pl.BlockSpec((1, lp, v.shape[2]), full),
                pl.BlockSpec((nvar, BIAS_HEADS, tq, LANES), lambda bb, ii, *_: (0, 0, 0, 0),
                             pipeline_mode=pl.Buffered(1)),
            ],
            out_specs=pl.BlockSpec((1, tq, dq), tile),
            scratch_shapes=[
                pltpu.VMEM((nk, tq, tk), I32),
                pltpu.VMEM((npair, prow, LANES), BF16),
                pltpu.VMEM((npair, prow, tk), BF16),
                pltpu.VMEM((npair, prow, LANES), F32),
                pltpu.VMEM((npair, prow, LANES), F32),
                pltpu.VMEM((npair, prow, 2 * LANES), F32),
            ],
        ),
        out_shape=jax.ShapeDtypeStruct((b, sq, dq), BF16),
        compiler_params=pltpu.CompilerParams(vmem_limit_bytes=VMEM_LIMIT),
        name="dsa_attn",
    )(jnp.asarray(nvis, I32), jnp.asarray(var, I32).reshape(-1), q, qi, wi, ki2, k, v, bias)


def _dsa_prompt(q, qi, wi, ki2, k, v, rel_bias):
    s = q.shape[1]
    tq, tk = 128, 512
    nq = s // tq
    nvis = [(i * tq) // tk + 1 for i in range(nq)]
    bias, var = _block_bias(rel_bias, nq, s // LANES, tq)
    return _dsa_attn(q, qi, wi, ki2, k, v, bias, nvis, var, 0, s, min(IDX_TOPK, s // 4), tq, tk)


def _route(lt_s, rb_ref, gt_s):
    rows = [lt_s[e:e + 1, :] for e in range(N_EXPERTS)]
    mx = functools.reduce(jnp.maximum, rows)
    ex = [jnp.exp(r - mx) for r in rows]
    z = functools.reduce(lambda a, b: a + b, ex)
    probs = [x / z for x in ex]
    sel = [probs[e] + rb_ref[e] for e in range(N_EXPERTS)]
    best_val, best_grp = None, None
    for gidx in range(N_GROUPS):
        a, b, c, d = sel[EXPERTS_PER_GROUP * gidx:EXPERTS_PER_GROUP * (gidx + 1)]
        hi1, lo1, hi2, lo2 = jnp.maximum(a, b), jnp.minimum(a, b), jnp.maximum(c, d), jnp.minimum(c, d)
        top2 = jnp.maximum(hi1, hi2) + jnp.maximum(jnp.minimum(hi1, hi2), jnp.maximum(lo1, lo2))
        if gidx == 0:
            best_val, best_grp = top2, jnp.zeros_like(top2, dtype=I32)
        else:
            better = top2 > best_val
            best_grp = jnp.where(better, gidx, best_grp)
            best_val = jnp.where(better, top2, best_val)
    masked = [jnp.where(best_grp == e // EXPERTS_PER_GROUP, sel[e], -jnp.inf) for e in range(N_EXPERTS)]

    def first_argmax(vals):
        bv, bi = vals[0], jnp.zeros_like(best_grp)
        for e in range(1, N_EXPERTS):
            better = vals[e] > bv
            bi = jnp.where(better, e, bi)
            bv = jnp.where(better, vals[e], bv)
        return bi

    i1 = first_argmax(masked)
    i2 = first_argmax([jnp.where(i1 == e, -jnp.inf, masked[e]) for e in range(N_EXPERTS)])
    w1 = functools.reduce(lambda a, b: a + b, [jnp.where(i1 == e, probs[e], 0.0) for e in range(N_EXPERTS)])
    w2 = functools.reduce(lambda a, b: a + b, [jnp.where(i2 == e, probs[e], 0.0) for e in range(N_EXPERTS)])
    wsum = w1 + w2
    w1, w2 = w1 / wsum, w2 / wsum
    for e in range(N_EXPERTS):
        gt_s[e:e + 1, :] = jnp.where(i1 == e, w1, 0.0) + jnp.where(i2 == e, w2, 0.0)


def _moe_kernel(rb_ref, x_ref, g_ref, shift_ref, scale_ref, gate_ref, wrh_ref, wrl_ref, wg_ref, wu_ref, wd_ref,
                gfin_ref, out_ref, hb_s, lt_s, gt_s, gates_s, acc_s, *, final_norm):
    e = pl.program_id(1)

    @pl.when(e == 0)
    def _():
        h = _norm_mod(x_ref[...], g_ref[...], shift_ref[0], scale_ref[0])
        hb = h.astype(BF16)
        hb_s[...] = hb
        hl = (h - hb.astype(F32)).astype(BF16)
        lt_s[...] = (lax.dot_general(wrh_ref[...], hb, _NT, preferred_element_type=F32)
                     + lax.dot_general(wrl_ref[...], hb, _NT, preferred_element_type=F32)
                     + lax.dot_general(wrh_ref[...], hl, _NT, preferred_element_type=F32))
        gt_s[...] = jnp.zeros(gt_s.shape, F32)
        _route(lt_s, rb_ref, gt_s)
        gates_s[...] = gt_s[...].T
        acc_s[...] = jnp.zeros(acc_s.shape, F32)

    hb = hb_s[...]
    up = jnp.dot(hb, wu_ref[0], preferred_element_type=F32)
    gt = jnp.dot(hb, wg_ref[0], preferred_element_type=F32)
    lane = lax.broadcasted_iota(I32, gates_s.shape, 1)
    gcol = jnp.sum(jnp.where(lane == e, gates_s[...], 0.0), axis=1, keepdims=True)
    a = (gt * jax.nn.sigmoid(gt) * up * gcol).astype(BF16)
    acc_s[...] += jnp.dot(a, wd_ref[0], preferred_element_type=F32)

    @pl.when(e == N_EXPERTS - 1)
    def _():
        y = x_ref[...] + _per_segment(acc_s[...], gate_ref[0], lambda v, s: v * s)
        if final_norm:
            y = _rms(y, gfin_ref[...])
        out_ref[...] = y


def _moe(x, g, shift, scale, gate, w_router, router_bias, wg_bf, wu_bf, wd_bf, g_final, final_norm, tm):
    n, d = x.shape
    nt = n // tm
    nseg = tm // CHUNK
    ff = wg_bf.shape[2]
    wrt = w_router.T
    wrh = wrt.astype(BF16)
    wrl = (wrt - wrh.astype(F32)).astype(BF16)
    seg_spec = pl.BlockSpec((1, nseg, d), lambda t, e: (t, 0, 0))
    row_spec = pl.BlockSpec((1, d), lambda t, e: (0, 0))
    return pl.pallas_call(
        functools.partial(_moe_kernel, final_norm=final_norm),
        grid=(nt, N_EXPERTS),
        in_specs=[
            pl.BlockSpec(memory_space=pltpu.SMEM),
            pl.BlockSpec((tm, d), lambda t, e: (t, 0)),
            row_spec, seg_spec, seg_spec, seg_spec,
            pl.BlockSpec((N_EXPERTS, d), lambda t, e: (0, 0)),
            pl.BlockSpec((N_EXPERTS, d), lambda t, e: (0, 0)),
            pl.BlockSpec((1, d, ff), lambda t, e: (e, 0, 0)),
            pl.BlockSpec((1, d, ff), lambda t, e: (e, 0, 0)),
            pl.BlockSpec((1, ff, d), lambda t, e: (e, 0, 0)),
            row_spec,
        ],
        out_specs=pl.BlockSpec((tm, d), lambda t, e: (t, 0)),
        out_shape=jax.ShapeDtypeStruct((n, d), F32),
        scratch_shapes=[
            pltpu.VMEM((tm, d), BF16),
            pltpu.VMEM((N_EXPERTS, tm), F32),
            pltpu.VMEM((LANES, tm), F32),
            pltpu.VMEM((tm, LANES), F32),
            pltpu.VMEM((tm, d), F32),
        ],
        compiler_params=pltpu.CompilerParams(vmem_limit_bytes=VMEM_LIMIT),
        name="moe",
    )(router_bias, x, g.reshape(1, d), shift.reshape(nt, nseg, d), scale.reshape(nt, nseg, d),
      gate.reshape(nt, nseg, d), wrh, wrl, wg_bf, wu_bf, wd_bf, g_final.reshape(1, d))


def _dsa_in_weights(w):
    d = w.shape[0]
    o1 = DSA_HEADS * DSA_HEAD_DIM
    o2 = o1 + DSA_KV_HEADS * DSA_HEAD_DIM
    o3 = o2 + DSA_KV_HEADS * DSA_HEAD_DIM
    o4 = o3 + IDX_HEADS * IDX_DIM
    o5 = o4 + IDX_DIM
    wq = w[:, :o1].reshape(d, DSA_HEADS, 1, DSA_HEAD_DIM)
    half = ((np.arange(DSA_HEADS) // DSA_GROUP) % 2).reshape(1, DSA_HEADS, 1, 1)
    wq = jnp.concatenate([jnp.where(half == 0, wq, 0.0), jnp.where(half == 1, wq, 0.0)], axis=2)
    wki = w[:, o4:o5]
    zk = jnp.zeros_like(wki)
    wwi = jnp.pad(w[:, o5:], ((0, 0), (0, LANES - IDX_HEADS)))
    return jnp.concatenate([wq.reshape(d, 2 * o1), w[:, o1:o4], wki, zk, zk, wki, wwi], axis=1).astype(BF16)


def _dsa_out_weights(w):
    d = w.shape[1]
    w4 = w.reshape(DSA_HEADS, 1, DSA_HEAD_DIM, d)
    half = ((np.arange(DSA_HEADS) // DSA_GROUP) % 2).reshape(DSA_HEADS, 1, 1, 1)
    w4 = jnp.concatenate([jnp.where(half == 0, w4, 0.0), jnp.where(half == 1, w4, 0.0)], axis=1)
    return w4.reshape(2 * DSA_HEADS * DSA_HEAD_DIM, d).astype(BF16)


def kernel(x_prompt, x_sample, cache_k_diff, cache_v_diff, cache_k_dsa, cache_v_dsa, cache_kidx_dsa,
           c_prompt, c_sample, rel_bias, w_in_diff, w_out_diff, lam_diff, subln_g_diff, w_in_dsa, w_out_dsa,
           w_ada, b_ada, g_norm_mix, g_norm_ffn, w_router, router_bias, w_gate, w_up, w_down, g_final):
    b, s, d = x_prompt.shape
    bs, ss, _ = x_sample.shape
    past = cache_k_diff.shape[2]
    depth = w_ada.shape[0]
    assert d == D_MODEL and ss == CHUNK and s % 512 == 0 and past % CHUNK == 0 and depth == 2
    l_s = past + ss
    lp_s = -(-l_s // LANES) * LANES
    nkb_s = lp_s // LANES

    streams = []
    for x3, rows in ((x_prompt, np.repeat(np.arange(b), s // CHUNK)), (x_sample, b + np.arange(bs))):
        n = x3.shape[0] * x3.shape[1]
        streams.append((x3.reshape(n, d), x3.shape[:2], rows,
                        _pick_tile(n, (512, 256, 128, 64)), _pick_tile(n, (1024, 512, 256, 128, 64))))
    mod = _ada_mod(jnp.concatenate([c_prompt, c_sample], axis=0), w_ada, b_ada)

    def mods(layer, rows):
        m = mod[layer][rows]
        return [m[:, i * d:(i + 1) * d] for i in range(6)]

    def pad_keys(x):
        return jnp.pad(x, ((0, 0), (0, lp_s - x.shape[1]), (0, 0)))

    bias_s = _bias_tiles(rel_bias, [jb * LANES - past for jb in range(nkb_s)], [past] * nkb_s, l_s, ss, LANES)
    var_s = [list(range(nkb_s))]
    xs = [st[0] for st in streams]
    outs_kv = []

    qk = DIFF_HEADS * 2 * DIFF_HEAD_DIM
    lam_init = 0.8 - 0.6 * float(np.exp(-0.3 * 0))
    cols = ((0, qk, ((0, 0, DIFF_HEAD_DIM ** -0.5 * LOG2E),)),
            (qk, qk, ((1, 0, 1.0), (3, 0, 1.0))),
            (2 * qk, qk, ((2, 0, 1.0), (4, 0, 1.0))))
    outs = ((qk, BF16), (qk, F32), (qk, F32), (qk, BF16), (qk, BF16))
    w_in0, w_out0 = w_in_diff[0].astype(BF16), w_out_diff[0].astype(BF16)
    moe_w0 = (w_gate[0].astype(BF16), w_up[0].astype(BF16), w_down[0].astype(BF16))
    for si, (_, (nb, ns), rows, tm, tm_moe) in enumerate(streams):
        m = mods(0, rows)
        q_bf, k_f, v_f, k_bf, v_bf = _in_proj(xs[si], g_norm_mix[0], m[0], m[1], w_in0, cols, outs, tm)
        outs_kv.append((k_f, v_f))
        q3, k3, v3 = (a.reshape(nb, ns, qk) for a in (q_bf, k_bf, v_bf))
        if si == 0:
            o = _diff_prompt(q3, k3, v3, rel_bias, lam_diff[0], subln_g_diff[0], lam_init)
        else:
            ka = pad_keys(jnp.concatenate([cache_k_diff[0].reshape(nb, past, qk).astype(BF16), k3], axis=1))
            va = pad_keys(jnp.concatenate([cache_v_diff[0].reshape(nb, past, qk).astype(BF16), v3], axis=1))
            o = _diff_attn(q3, ka, va, bias_s, [(0, 0)], var_s, lam_diff[0], subln_g_diff[0], lam_init, ns, lp_s)
        x = _out_proj(o.reshape(nb * ns, qk), xs[si], m[2], w_out0, tm)
        xs[si] = _moe(x, g_norm_ffn[0], m[3], m[4], m[5], w_router, router_bias, *moe_w0, g_final, False, tm_moe)

    nq_w = 2 * DSA_HEADS * DSA_HEAD_DIM
    kv_w = DSA_KV_HEADS * DSA_HEAD_DIM
    qi_w = IDX_HEADS * IDX_DIM
    c_k, c_v, c_qi = nq_w, nq_w + kv_w, nq_w + 2 * kv_w
    c_ki, c_wi = c_qi + qi_w, c_qi + qi_w + 2 * LANES
    half_q = nq_w // 2
    cols = ((0, half_q, ((0, 0, DSA_HEAD_DIM ** -0.5 * LOG2E),)),
            (half_q, half_q, ((0, half_q, DSA_HEAD_DIM ** -0.5 * LOG2E),)),
            (c_k, kv_w, ((1, 0, 1.0), (3, 0, 1.0))),
            (c_v, kv_w, ((2, 0, 1.0), (4, 0, 1.0))),
            (c_qi, qi_w, ((5, 0, 1.0),)),
            (c_ki, 2 * LANES, ((7, 0, 1.0),)),
            (c_ki, IDX_DIM, ((6, 0, 1.0),)),
            (c_wi, LANES, ((8, 0, 1.0),)))
    outs = ((nq_w, BF16), (kv_w, F32), (kv_w, F32), (kv_w, BF16), (kv_w, BF16), (qi_w, BF16),
            (IDX_DIM, F32), (2 * LANES, BF16), (LANES, F32))
    w_in1, w_out1 = _dsa_in_weights(w_in_dsa[0]), _dsa_out_weights(w_out_dsa[0])
    moe_w1 = (w_gate[1].astype(BF16), w_up[1].astype(BF16), w_down[1].astype(BF16))
    for si, (_, (nb, ns), rows, tm, tm_moe) in enumerate(streams):
        m = mods(1, rows)
        q2, k_f, v_f, k_bf, v_bf, qi_bf, ki_f, ki2_bf, wi_f = _in_proj(
            xs[si], g_norm_mix[1], m[0], m[1], w_in1, cols, outs, tm)
        outs_kv.append((k_f, v_f, ki_f))
        q3, qi3, wi3 = q2.reshape(nb, ns, nq_w), qi_bf.reshape(nb, ns, qi_w), wi_f.reshape(nb, ns, LANES)
        k3, v3, ki3 = k_bf.reshape(nb, ns, kv_w), v_bf.reshape(nb, ns, kv_w), ki2_bf.reshape(nb, ns, 2 * LANES)
        if si == 0:
            o = _dsa_prompt(q3, qi3, wi3, ki3, k3, v3, rel_bias)
        else:
            cki = cache_kidx_dsa[0].astype(BF16)
            zki = jnp.zeros_like(cki)
            ka = pad_keys(jnp.concatenate([cache_k_dsa[0].reshape(nb, past, kv_w).astype(BF16), k3], axis=1))
            va = pad_keys(jnp.concatenate([cache_v_dsa[0].reshape(nb, past, kv_w).astype(BF16), v3], axis=1))
            kia = pad_keys(jnp.concatenate([jnp.concatenate([cki, zki, zki, cki], axis=2), ki3], axis=1))
            o = _dsa_attn(q3, qi3, wi3, kia, ka, va, bias_s, [nkb_s], var_s, past, l_s,
                          min(IDX_TOPK, l_s // 4), ns, LANES)
        x = _out_proj(o.reshape(nb * ns, nq_w), xs[si], m[2], w_out1, tm)
        xs[si] = _moe(x, g_norm_ffn[1], m[3], m[4], m[5], w_router, router_bias, *moe_w1, g_final, True, tm_moe)

    h2 = (DIFF_HEADS, 2, DIFF_HEAD_DIM)
    hv = (DIFF_HEADS, 2 * DIFF_HEAD_DIM)
    hk = (DSA_KV_HEADS, DSA_HEAD_DIM)
    (kd_p, vd_p), (kd_s, vd_s), (ks_p, vs_p, ki_p), (ks_s, vs_s, ki_s) = outs_kv
    return (xs[0].reshape(b, s, d), xs[1].reshape(bs, ss, d),
            kd_p.reshape(1, b, s, *h2), vd_p.reshape(1, b, s, *hv),
            ks_p.reshape(1, b, s, *hk), vs_p.reshape(1, b, s, *hk), ki_p.reshape(1, b, s, IDX_DIM),
            kd_s.reshape(1, bs, ss, *h2), vd_s.reshape(1, bs, ss, *hv),
            ks_s.reshape(1, bs, ss, *hk), vs_s.reshape(1, bs, ss, *hk), ki_s.reshape(1, bs, ss, IDX_DIM))
```

```python
import functools

import jax
import jax.numpy as jnp
import numpy as np
from jax import lax
from jax.experimental import pallas as pl
from jax.experimental.pallas import tpu as pltpu

F32 = jnp.float32
BF16 = jnp.bfloat16
I32 = jnp.int32
I16 = jnp.int16

D_MODEL = 1024
CHUNK = 64
DIFF_HEADS = 8
DIFF_HEAD_DIM = 64
DSA_HEADS = 16
DSA_KV_HEADS = 4
DSA_GROUP = DSA_HEADS // DSA_KV_HEADS
DSA_HEAD_DIM = 64
IDX_HEADS = 8
IDX_DIM = 64
IDX_TOPK = 256
REL_BUCKETS = 32
BIAS_HEADS = 16
N_EXPERTS = 16
N_GROUPS = 4
EXPERTS_PER_GROUP = N_EXPERTS // N_GROUPS
EXPERT_FF = 512
RMS_EPS = 1e-6

LANES = 128
NEG = -1e30
LOG2E = 1.4426950408889634
INT_MIN = -(2 ** 31)
NEG_INF_KEY = -0x7F800000
VMEM_LIMIT = 56 * 1024 * 1024

_BUCKET_STEPS = (12, 16, 23, 32, 46, 64, 91)
FAR_BLOCKS = 2

_NT = (((1,), (1,)), ((), ()))


def _pick_tile(n, prefs):
    for t in prefs:
        if n % t == 0:
            return t
    raise ValueError(f"no tile in {prefs} divides {n}")


def _rms(x, g):
    ms = jnp.mean(x * x, axis=-1, keepdims=True)
    return x * lax.rsqrt(ms + RMS_EPS) * g


def _per_segment(x, seg_vals, op):
    tm, d = x.shape
    nseg = tm // CHUNK
    return op(x.reshape(nseg, CHUNK, d), seg_vals[:, None, :]).reshape(tm, d)


def _norm_mod(x, g, shift, scale):
    y = _rms(x, g)
    y = _per_segment(y, scale, lambda a, s: a * (1.0 + s))
    return _per_segment(y, shift, lambda a, s: a + s)


def _ada_kernel(c_ref, w_ref, b_ref, o_ref):
    c = c_ref[...]
    a = (c * jax.nn.sigmoid(c)).astype(BF16)
    o_ref[0] = jnp.dot(a, w_ref[0].astype(BF16), preferred_element_type=F32) + b_ref[0]


def _ada_mod(c_all, w_ada, b_ada):
    depth, d, n6 = w_ada.shape
    bc = c_all.shape[0]
    tn = _pick_tile(n6, (1536, 1024, 512, 128))
    return pl.pallas_call(
        _ada_kernel,
        grid=(depth, n6 // tn),
        in_specs=[
            pl.BlockSpec((bc, d), lambda l, j: (0, 0)),
            pl.BlockSpec((1, d, tn), lambda l, j: (l, 0, j)),
            pl.BlockSpec((1, 1, tn), lambda l, j: (l, 0, j)),
        ],
        out_specs=pl.BlockSpec((1, bc, tn), lambda l, j: (l, 0, j)),
        out_shape=jax.ShapeDtypeStruct((depth, bc, n6), F32),
        compiler_params=pltpu.CompilerParams(vmem_limit_bytes=VMEM_LIMIT),
        name="ada_mod",
    )(c_all, w_ada, b_ada.reshape(depth, 1, n6))


def _bias_kernel(off_ref, q0_ref, len_ref, rb_ref, o_ref, *, tq, tk):
    v = pl.program_id(0)
    h = pl.program_id(1)
    r = lax.broadcasted_iota(I32, (tq, tk), 0)
    c = lax.broadcasted_iota(I32, (tq, tk), 1)
    rel = off_ref[v] + c - r
    qpos = q0_ref[v] + r
    kpos = qpos + rel
    n = jnp.abs(rel)
    big = jnp.full((tq, tk), 8, I32)
    for t in _BUCKET_STEPS:
        big = big + jnp.where(n >= t, 1, 0)
    bucket = jnp.where(n < 8, n, big) + jnp.where(rel > 0, REL_BUCKETS // 2, 0)
    val = jnp.zeros((tq, tk), F32)
    for b in range(REL_BUCKETS):
        val = jnp.where(bucket == b, rb_ref[b, h], val)
    vis = jnp.where((kpos >> 6) <= (qpos >> 6), 1, 0) * jnp.where(kpos < len_ref[0], 1, 0)
    o_ref[0, 0] = jnp.where(vis > 0, val * LOG2E, NEG)


def _bias_tiles(rel_bias, offs, q0s, length, tq, tk):
    nvar = len(offs)
    return pl.pallas_call(
        functools.partial(_bias_kernel, tq=tq, tk=tk),
        grid_spec=pltpu.PrefetchScalarGridSpec(
            num_scalar_prefetch=3,
            grid=(nvar, BIAS_HEADS),
            in_specs=[pl.BlockSpec(memory_space=pltpu.SMEM)],
            out_specs=pl.BlockSpec((1, 1, tq, tk), lambda v, h, *_: (v, h, 0, 0)),
        ),
        out_shape=jax.ShapeDtypeStruct((nvar, BIAS_HEADS, tq, tk), F32),
        name="bias_tiles",
    )(jnp.asarray(offs, I32), jnp.asarray(q0s, I32), jnp.asarray([length], I32), rel_bias)


def _inproj_kernel(x_ref, g_ref, shift_ref, scale_ref, w_ref, wt_ref, *out_refs, cols, t_cols):
    h = _norm_mod(x_ref[...], g_ref[...], shift_ref[0], scale_ref[0]).astype(BF16)
    for start, width, dests in cols:
        p = jnp.dot(h, w_ref[:, start:start + width], preferred_element_type=F32)
        for out_idx, out_off, mult in dests:
            o_ref = out_refs[out_idx]
            val = p if mult == 1.0 else p * mult
            o_ref[:, out_off:out_off + width] = val.astype(o_ref.dtype)
    t_refs = out_refs[len(out_refs) - len(t_cols):]
    for (start, width), t_ref in zip(t_cols, t_refs):
        t_ref[0] = lax.dot_general(wt_ref[start:start + width, :], h, _NT, preferred_element_type=F32)


def _in_proj(x, g, shift, scale, w_bf, cols, out_defs, tm, wt_bf=None, t_cols=(), seq=None):
    n, d = x.shape
    nt = n // tm
    nseg = tm // CHUNK
    nout = w_bf.shape[1]
    if wt_bf is None:
        wt_bf = jnp.zeros((16, d), BF16)
    seg_spec = pl.BlockSpec((1, nseg, d), lambda t: (t, 0, 0))
    tpb = (seq or n) // tm
    return pl.pallas_call(
        functools.partial(_inproj_kernel, cols=cols, t_cols=t_cols),
        grid=(nt,),
        in_specs=[
            pl.BlockSpec((tm, d), lambda t: (t, 0)),
            pl.BlockSpec((1, d), lambda t: (0, 0)),
            seg_spec, seg_spec,
            pl.BlockSpec((d, nout), lambda t: (0, 0)),
            pl.BlockSpec(wt_bf.shape, lambda t: (0, 0)),
        ],
        out_specs=([pl.BlockSpec((tm, w), lambda t: (t, 0)) for w, _ in out_defs]
                   + [pl.BlockSpec((1, w, tm), lambda t: (t // tpb, 0, t % tpb)) for _, w in t_cols]),
        out_shape=([jax.ShapeDtypeStruct((n, w), dt) for w, dt in out_defs]
                   + [jax.ShapeDtypeStruct((n // (seq or n), w, seq or n), F32) for _, w in t_cols]),
        compiler_params=pltpu.CompilerParams(vmem_limit_bytes=VMEM_LIMIT),
        name="in_proj",
    )(x, g.reshape(1, d), shift.reshape(nt, nseg, d), scale.reshape(nt, nseg, d), w_bf, wt_bf)


def _outproj_kernel(o_ref, x_ref, gate_ref, w_ref, out_ref):
    r = jnp.dot(o_ref[...], w_ref[...], preferred_element_type=F32)
    out_ref[...] = x_ref[...] + _per_segment(r, gate_ref[0], lambda a, s: a * s)


def _out_proj(o, x, gate, w_bf, tm):
    n, d = x.shape
    kdim = o.shape[1]
    nt = n // tm
    nseg = tm // CHUNK
    return pl.pallas_call(
        _outproj_kernel,
        grid=(nt,),
        in_specs=[
            pl.BlockSpec((tm, kdim), lambda t: (t, 0)),
            pl.BlockSpec((tm, d), lambda t: (t, 0)),
            pl.BlockSpec((1, nseg, d), lambda t: (t, 0, 0)),
            pl.BlockSpec((kdim, d), lambda t: (0, 0)),
        ],
        out_specs=pl.BlockSpec((tm, d), lambda t: (t, 0)),
        out_shape=jax.ShapeDtypeStruct((n, d), F32),
        compiler_params=pltpu.CompilerParams(vmem_limit_bytes=VMEM_LIMIT),
        name="out_proj",
    )(o, x, gate.reshape(nt, nseg, d), w_bf)


def _diff_attn_kernel(it_ref, jt_ref, var_ref, fin_ref, lam_ref, q_ref, k_ref, v_ref, bias_ref, g_ref,
                      o_ref, qb_s, p_s, al_s, m_s, acc_s, *, tq, tk, rc, nkb, lam_init):
    t = pl.program_id(1)
    nchunk = 2 * tq // rc
    ncb = tk // LANES
    bvar = [var_ref[it_ref[t] * nkb + jt_ref[t] * ncb + cb] for cb in range(ncb)]

    @pl.when(jt_ref[t] == 0)
    def _():
        m_s[...] = jnp.full(m_s.shape, NEG, F32)
        acc_s[...] = jnp.zeros(acc_s.shape, F32)
        lane = lax.broadcasted_iota(I32, (tq, LANES), 1)
        for h in range(DIFF_HEADS):
            qh = q_ref[0, :, h * LANES:(h + 1) * LANES].astype(F32)
            qb_s[h, 0:tq, :] = jnp.where(lane < DIFF_HEAD_DIM, qh, 0.0).astype(BF16)
            qb_s[h, tq:2 * tq, :] = jnp.where(lane >= DIFF_HEAD_DIM, qh, 0.0).astype(BF16)

    def logits(h):
        return lax.dot_general(qb_s[h], k_ref[0, :, h * LANES:(h + 1) * LANES], _NT, preferred_element_type=F32)

    s_next = logits(0)
    for h in range(DIFF_HEADS):
        s_all = s_next
        if h + 1 < DIFF_HEADS:
            s_next = logits(h + 1)
        for c in range(nchunk):
            rows = slice(c * rc, (c + 1) * rc)
            brow = (c * rc) % tq
            hm = 2 * h + (c * rc) // tq
            s = jnp.concatenate(
                [s_all[rows, cb * LANES:(cb + 1) * LANES] + bias_ref[bvar[cb], hm, brow:brow + rc, :]
                 for cb in range(ncb)], axis=1)
            m_prev = m_s[h, rows]
            m_new = jnp.maximum(m_prev, jnp.max(s, axis=1, keepdims=True))
            p = jnp.exp2(s - jnp.concatenate([m_new] * ncb, axis=1))
            p_s[h, rows] = p.astype(BF16)
            al_s[h, rows] = jnp.exp2(m_prev - m_new)
            m_s[h, rows] = m_new
        vh = v_ref[0, :, h * LANES:(h + 1) * LANES]
        v1 = jnp.concatenate([vh, jnp.ones_like(vh)], axis=1)
        alpha = al_s[h]
        acc_s[h] = (jnp.concatenate([alpha, alpha], axis=1) * acc_s[h]
                    + jnp.dot(p_s[h], v1, preferred_element_type=F32))

    @pl.when(fin_ref[t] == 1)
    def _():
        lp = lam_ref[...]
        lam = (jnp.exp(jnp.sum(lp[0:1] * lp[1:2], axis=1, keepdims=True))
               - jnp.exp(jnp.sum(lp[2:3] * lp[3:4], axis=1, keepdims=True)) + lam_init)
        g = g_ref[...]
        for h in range(DIFF_HEADS):
            o0 = acc_s[h, 0:tq, 0:LANES] / acc_s[h, 0:tq, LANES:2 * LANES]
            o1 = acc_s[h, tq:2 * tq, 0:LANES] / acc_s[h, tq:2 * tq, LANES:2 * LANES]
            o = _rms(o0 - lam * o1, g) * (1.0 - lam_init)
            o_ref[0, :, h * LANES:(h + 1) * LANES] = o.astype(o_ref.dtype)


def _diff_attn(q, k, v, bias, pairs, var, lam_p, subln_g, lam_init, tq, tk):
    b, sq, dq = q.shape
    nvar = bias.shape[0]
    nkb = k.shape[1] // LANES
    it = np.asarray([p[0] for p in pairs], np.int32)
    jt = np.asarray([p[1] for p in pairs], np.int32)
    vt = np.asarray(var, np.int32).reshape(-1)
    fin = np.asarray([1 if (n + 1 == len(pairs) or pairs[n + 1][0] != p[0]) else 0
                      for n, p in enumerate(pairs)], np.int32)
    rc = min(tq, 64)
    return pl.pallas_call(
        functools.partial(_diff_attn_kernel, tq=tq, tk=tk, rc=rc, nkb=nkb, lam_init=lam_init),
        grid_spec=pltpu.PrefetchScalarGridSpec(
            num_scalar_prefetch=4,
            grid=(b, len(pairs)),
            in_specs=[
                pl.BlockSpec((4, DIFF_HEAD_DIM), lambda bb, t, *_: (0, 0)),
                pl.BlockSpec((1, tq, dq), lambda bb, t, it_, jt_, vt_, fin_: (bb, it_[t], 0)),
                pl.BlockSpec((1, tk, dq), lambda bb, t, it_, jt_, vt_, fin_: (bb, jt_[t], 0)),
                pl.BlockSpec((1, tk, dq), lambda bb, t, it_, jt_, vt_, fin_: (bb, jt_[t], 0)),
                pl.BlockSpec((nvar, BIAS_HEADS, tq, LANES), lambda bb, t, *_: (0, 0, 0, 0),
                             pipeline_mode=pl.Buffered(1)),
                pl.BlockSpec((1, 2 * DIFF_HEAD_DIM), lambda bb, t, *_: (0, 0)),
            ],
            out_specs=pl.BlockSpec((1, tq, dq), lambda bb, t, it_, jt_, vt_, fin_: (bb, it_[t], 0)),
            scratch_shapes=[
                pltpu.VMEM((DIFF_HEADS, 2 * tq, LANES), BF16),
                pltpu.VMEM((DIFF_HEADS, 2 * tq, tk), BF16),
                pltpu.VMEM((DIFF_HEADS, 2 * tq, LANES), F32),
                pltpu.VMEM((DIFF_HEADS, 2 * tq, LANES), F32),
                pltpu.VMEM((DIFF_HEADS, 2 * tq, 2 * LANES), F32),
            ],
        ),
        out_shape=jax.ShapeDtypeStruct((b, sq, dq), BF16),
        compiler_params=pltpu.CompilerParams(vmem_limit_bytes=VMEM_LIMIT),
        name="diff_attn",
    )(jnp.asarray(it), jnp.asarray(jt), jnp.asarray(vt), jnp.asarray(fin), lam_p, q, k, v, bias,
      subln_g.reshape(1, -1))


def _block_bias(rel_bias, nq, nkb, tq):
    bpq = tq // LANES
    lo, hi = -bpq, FAR_BLOCKS
    bias = _bias_tiles(rel_bias, [-LANES * d for d in range(lo, hi + 1)], [4 * tq] * (hi - lo + 1), 1 << 30, tq, LANES)
    var = [[min(max(bpq * i - jb, lo), hi) - lo for jb in range(nkb)] for i in range(nq)]
    return bias, var


def _diff_prompt(q, k, v, rel_bias, lam_p, subln_g, lam_init):
    s = q.shape[1]
    tq, tk = 256, 512
    nq = s // tq
    pairs = [(i, j) for i in range(nq) for j in range((i * tq) // tk + 1)]
    bias, var = _block_bias(rel_bias, nq, s // LANES, tq)
    return _diff_attn(q, k, v, bias, pairs, var, lam_p, subln_g, lam_init, tq, tk)


def _sort_key(x):
    bits = pltpu.bitcast(x, I32)
    return jnp.where(bits < 0, INT_MIN - bits, bits)


def _dsa_kernel(nvis_ref, var_ref, q_ref, qi_ref, wi_ref, ki_ref, k_ref, v_ref, bias_ref, o_ref,
                keys_s, khi_s, klo_s, qs_s, p_s, al_s, m_s, acc_s, *, tq, tk, nk, rc, q0, length, topk, nbits):
    i = pl.program_id(1)
    nvis = nvis_ref[i]
    npair = DSA_KV_HEADS // 2
    hpp = 2 * DSA_GROUP
    nchunk = hpp * tq // rc

    for hd in range(DSA_HEADS):
        qs_s[hd // hpp, (hd % hpp) * tq:(hd % hpp + 1) * tq, :] = q_ref[0, :, hd * LANES:(hd + 1) * LANES]
    qi_stack = jnp.concatenate([qi_ref[0, :, u * LANES:(u + 1) * LANES] for u in range(IDX_HEADS // 2)], axis=0)
    wcols = [wi_ref[0, :, h:h + 1] for h in range(IDX_HEADS)]

    r_io = lax.broadcasted_iota(I32, (tq, tk), 0)
    c_io = lax.broadcasted_iota(I32, (tq, tk), 1)
    q_chunk = (q0 + i * tq + r_io) >> 6
    score_scale = IDX_DIM ** -0.5 * IDX_HEADS ** -0.5

    def score_tile(j):
        off = pl.multiple_of(j * tk, tk)
        ki_lo = ki_ref[0, pl.ds(off, tk), 0:LANES]
        ki_hi = ki_ref[0, pl.ds(off, tk), LANES:2 * LANES]
        s_even = lax.dot_general(qi_stack, ki_lo, _NT, preferred_element_type=F32)
        s_odd = lax.dot_general(qi_stack, ki_hi, _NT, preferred_element_type=F32)
        score = jnp.zeros((tq, tk), F32)
        for u in range(IDX_HEADS // 2):
            score = score + jnp.maximum(s_even[u * tq:(u + 1) * tq], 0.0) * wcols[2 * u]
            score = score + jnp.maximum(s_odd[u * tq:(u + 1) * tq], 0.0) * wcols[2 * u + 1]
        key = _sort_key(score * score_scale)
        kpos = j * tk + c_io
        vis = jnp.where((kpos >> 6) <= q_chunk, 1, 0) * jnp.where(kpos < length, 1, 0)
        key = jnp.where(vis > 0, key, INT_MIN)
        keys_s[j] = key
        khi_s[j] = (key >> 16).astype(I16)
        klo_s[j] = ((key & 0xFFFF) - 0x8000).astype(I16)

    def score_body(jj, carry):
        score_tile(2 * jj)
        score_tile(jnp.minimum(2 * jj + 1, nvis - 1))
        return carry

    lax.fori_loop(0, (nvis + 1) // 2, score_body, 0)

    def count16_gt(ref, below):
        b16 = jnp.broadcast_to(below, (tq, LANES)).astype(I16)

        def body(j, acc):
            kk = ref[j]
            for u in range(tk // LANES):
                acc = acc + jnp.where(kk[:, u * LANES:(u + 1) * LANES] > b16, jnp.int16(1), jnp.int16(0))
            return acc

        acc = lax.fori_loop(0, nvis, body, jnp.zeros((tq, LANES), I16))
        return jnp.sum(acc.astype(F32), axis=1, keepdims=True)

    def kth_largest16(ref, kth):
        def bit_body(it, ans):
            cand = ans | jnp.left_shift(jnp.int32(1), 15 - it)
            return jnp.where(count16_gt(ref, cand - 0x8001) >= kth, cand, ans)
        return lax.fori_loop(0, 16, bit_body, jnp.zeros((tq, 1), I32)) - 0x8000

    thr_hi = kth_largest16(khi_s, jnp.full((tq, 1), float(topk), F32))
    n_above = count16_gt(khi_s, thr_hi)
    hi16 = jnp.broadcast_to(thr_hi, (tq, LANES)).astype(I16)

    def keep_low(j, carry):
        for u in range(tk // LANES):
            blk = slice(u * LANES, (u + 1) * LANES)
            klo_s[j, :, blk] = jnp.where(khi_s[j, :, blk] == hi16, klo_s[j, :, blk], jnp.int16(-0x8000))
        return carry

    lax.fori_loop(0, nvis, keep_low, 0)
    thr_lo = kth_largest16(klo_s, topk - n_above)
    thr = jnp.left_shift(thr_hi, 16) | (thr_lo + 0x8000)
    thr_sel = jnp.maximum(thr, NEG_INF_KEY + 1)
    row_full = thr >= thr_sel

    def count_ge(cand):
        below = jnp.broadcast_to(cand - 1, (tq, LANES))

        def body(j, acc):
            kk = keys_s[j]
            for u in range(tk // LANES):
                acc = acc + jnp.where(kk[:, u * LANES:(u + 1) * LANES] > below, 1.0, 0.0)
            return acc

        acc = lax.fori_loop(0, nvis, body, jnp.zeros((tq, LANES), F32))
        return jnp.sum(acc, axis=1, keepdims=True)

    cnt_ge = count_ge(thr_sel)
    cnt_gt = count_ge(thr_sel + 1)
    excess = jnp.max(jnp.where(row_full, cnt_ge - topk, 0.0))

    @pl.when(excess > 0.0)
    def _():
        quota = topk - cnt_gt

        def count_tied_before(pos):
            def body(j, acc):
                kpos = j * tk + c_io
                hit = jnp.where(keys_s[j] == thr, 1.0, 0.0) * jnp.where(kpos < pos, 1.0, 0.0)
                return acc + hit
            acc = lax.fori_loop(0, nvis, body, jnp.zeros((tq, tk), F32))
            return jnp.sum(acc, axis=1, keepdims=True)

        def pos_body(it, p):
            cand = p | jnp.left_shift(jnp.int32(1), nbits - 1 - it)
            return jnp.where(count_tied_before(cand) < quota, cand, p)

        last = lax.fori_loop(0, nbits, pos_body, jnp.zeros((tq, 1), I32))
        last = jnp.where(row_full, last, nk * tk)

        def demote(j, carry):
            kpos = j * tk + c_io
            kk = keys_s[j]
            drop = jnp.where(kk == thr, 1, 0) * jnp.where(kpos > last, 1, 0)
            keys_s[j] = jnp.where(drop > 0, thr - 1, kk)
            return carry

        lax.fori_loop(0, nvis, demote, 0)

    m_s[...] = jnp.full(m_s.shape, NEG, F32)
    acc_s[...] = jnp.zeros(acc_s.shape, F32)

    ncb = tk // LANES

    def attn_body(j, carry):
        off = pl.multiple_of(j * tk, tk)
        bvar = [var_ref[(i * nk + j) * ncb + cb] for cb in range(ncb)]
        maskb = jnp.where(keys_s[j] >= thr_sel, 0.0, NEG)
        s_pair = [lax.dot_general(qs_s[pr], k_ref[0, pl.ds(off, tk), pr * LANES:(pr + 1) * LANES], _NT,
                                  preferred_element_type=F32) for pr in range(npair)]
        for pr in range(npair):
            for c in range(nchunk):
                rows = slice(c * rc, (c + 1) * rc)
                hd = pr * hpp + (c * rc) // tq
                brow = (c * rc) % tq
                s = jnp.concatenate(
                    [s_pair[pr][rows, cb * LANES:(cb + 1) * LANES] + bias_ref[bvar[cb], hd, brow:brow + rc, :]
                     for cb in range(ncb)], axis=1) + maskb[brow:brow + rc]
                m_prev = m_s[pr, rows]
                m_new = jnp.maximum(m_prev, jnp.max(s, axis=1, keepdims=True))
                p = jnp.exp2(s - jnp.concatenate([m_new] * ncb, axis=1))
                p_s[pr, rows] = p.astype(BF16)
                al_s[pr, rows] = jnp.exp2(m_prev - m_new)
                m_s[pr, rows] = m_new
            vp = v_ref[0, pl.ds(off, tk), pr * LANES:(pr + 1) * LANES]
            v1 = jnp.concatenate([vp, jnp.ones_like(vp)], axis=1)
            alpha = al_s[pr]
            acc_s[pr] = (jnp.concatenate([alpha, alpha], axis=1) * acc_s[pr]
                         + jnp.dot(p_s[pr], v1, preferred_element_type=F32))
        return carry

    lax.fori_loop(0, nvis, attn_body, 0)

    for hd in range(DSA_HEADS):
        rows = slice((hd % hpp) * tq, (hd % hpp + 1) * tq)
        o = acc_s[hd // hpp, rows, 0:LANES] / acc_s[hd // hpp, rows, LANES:2 * LANES]
        o_ref[0, :, hd * LANES:(hd + 1) * LANES] = o.astype(o_ref.dtype)


def _dsa_attn(q, qi, wi, ki2, k, v, bias, nvis, var, q0, length, topk, tq, tk):
    b, sq, dq = q.shape
    lp = k.shape[1]
    nq = sq // tq
    nk = lp // tk
    nvar = bias.shape[0]
    npair = DSA_KV_HEADS // 2
    prow = 2 * DSA_GROUP * tq
    kernel = functools.partial(_dsa_kernel, tq=tq, tk=tk, nk=nk, rc=min(tq, 64), q0=q0, length=length,
                               topk=topk, nbits=max(1, (lp - 1).bit_length()))
    full = lambda bb, ii, *_: (bb, 0, 0)
    tile = lambda bb, ii, *_: (bb, ii, 0)
    return pl.pallas_call(
        kernel,
        grid_spec=pltpu.PrefetchScalarGridSpec(
            num_scalar_prefetch=2,
            grid=(b, nq),
            in_specs=[
                pl.BlockSpec((1, tq, dq), tile),
                pl.BlockSpec((1, tq, qi.shape[2]), tile),
                pl.BlockSpec((1, tq, LANES), tile),
                pl.BlockSpec((1, lp, 2 * LANES), full, pipeline_mode=pl.Buffered(1)),
                pl.BlockSpec((1, lp, k.shape[2]), full, pipeline_mode=pl.Buffered(1)),
                pl.BlockSpec((1, lp, v.shape[2]), full, pipeline_mode=pl.Buffered(1)),
                pl.BlockSpec((nvar, BIAS_HEADS, tq, LANES), lambda bb, ii, *_: (0, 0, 0, 0),
                             pipeline_mode=pl.Buffered(1)),
            ],
            out_specs=pl.BlockSpec((1, tq, dq), tile),
            scratch_shapes=[
                pltpu.VMEM((nk, tq, tk), I32),
                pltpu.VMEM((nk, tq, tk), I16),
                pltpu.VMEM((nk, tq, tk), I16),
                pltpu.VMEM((npair, prow, LANES), BF16),
                pltpu.VMEM((npair, prow, tk), BF16),
                pltpu.VMEM((npair, prow, LANES), F32),
                pltpu.VMEM((npair, prow, LANES), F32),
                pltpu.VMEM((npair, prow, 2 * LANES), F32),
            ],
        ),
        out_shape=jax.ShapeDtypeStruct((b, sq, dq), BF16),
        compiler_params=pltpu.CompilerParams(vmem_limit_bytes=VMEM_LIMIT),
        name="dsa_attn",
    )(jnp.asarray(nvis, I32), jnp.asarray(var, I32).reshape(-1), q, qi, wi, ki2, k, v, bias)


def _dsa_prompt(q, qi, wi, ki2, k, v, rel_bias):
    s = q.shape[1]
    tq, tk = 128, 512
    nq = s // tq
    nvis = [(i * tq) // tk + 1 for i in range(nq)]
    bias, var = _block_bias(rel_bias, nq, s // LANES, tq)
    return _dsa_attn(q, qi, wi, ki2, k, v, bias, nvis, var, 0, s, min(IDX_TOPK, s // 4), tq, tk)


def _route(lt_s, rb_ref, gt_s):
    rows = [lt_s[e:e + 1, :] for e in range(N_EXPERTS)]
    mx = functools.reduce(jnp.maximum, rows)
    ex = [jnp.exp(r - mx) for r in rows]
    z = functools.reduce(lambda a, b: a + b, ex)
    probs = [x / z for x in ex]
    sel = [probs[e] + rb_ref[e] for e in range(N_EXPERTS)]
    best_val, best_grp = None, None
    for gidx in range(N_GROUPS):
        a, b, c, d = sel[EXPERTS_PER_GROUP * gidx:EXPERTS_PER_GROUP * (gidx + 1)]
        hi1, lo1, hi2, lo2 = jnp.maximum(a, b), jnp.minimum(a, b), jnp.maximum(c, d), jnp.minimum(c, d)
        top2 = jnp.maximum(hi1, hi2) + jnp.maximum(jnp.minimum(hi1, hi2), jnp.maximum(lo1, lo2))
        if gidx == 0:
            best_val, best_grp = top2, jnp.zeros_like(top2, dtype=I32)
        else:
            better = top2 > best_val
            best_grp = jnp.where(better, gidx, best_grp)
            best_val = jnp.where(better, top2, best_val)
    masked = [jnp.where(best_grp == e // EXPERTS_PER_GROUP, sel[e], -jnp.inf) for e in range(N_EXPERTS)]

    def first_argmax(vals):
        bv, bi = vals[0], jnp.zeros_like(best_grp)
        for e in range(1, N_EXPERTS):
            better = vals[e] > bv
            bi = jnp.where(better, e, bi)
            bv = jnp.where(better, vals[e], bv)
        return bi

    i1 = first_argmax(masked)
    i2 = first_argmax([jnp.where(i1 == e, -jnp.inf, masked[e]) for e in range(N_EXPERTS)])
    w1 = functools.reduce(lambda a, b: a + b, [jnp.where(i1 == e, probs[e], 0.0) for e in range(N_EXPERTS)])
    w2 = functools.reduce(lambda a, b: a + b, [jnp.where(i2 == e, probs[e], 0.0) for e in range(N_EXPERTS)])
    wsum = w1 + w2
    w1, w2 = w1 / wsum, w2 / wsum
    for e in range(N_EXPERTS):
        gt_s[e:e + 1, :] = jnp.where(i1 == e, w1, 0.0) + jnp.where(i2 == e, w2, 0.0)


def _moe_kernel(rb_ref, x_ref, g_ref, shift_ref, scale_ref, gate_ref, wrh_ref, wrl_ref, wg_ref, wu_ref, wd_ref,
                gfin_ref, out_ref, hb_s, lt_s, gt_s, gates_s, acc_s, *, final_norm):
    e = pl.program_id(1)

    @pl.when(e == 0)
    def _():
        h = _norm_mod(x_ref[...], g_ref[...], shift_ref[0], scale_ref[0])
        hb = h.astype(BF16)
        hb_s[...] = hb
        hl = (h - hb.astype(F32)).astype(BF16)
        lt_s[...] = (lax.dot_general(wrh_ref[...], hb, _NT, preferred_element_type=F32)
                     + lax.dot_general(wrl_ref[...], hb, _NT, preferred_element_type=F32)
                     + lax.dot_general(wrh_ref[...], hl, _NT, preferred_element_type=F32))
        gt_s[...] = jnp.zeros(gt_s.shape, F32)
        _route(lt_s, rb_ref, gt_s)
        gates_s[...] = gt_s[...].T
        acc_s[...] = jnp.zeros(acc_s.shape, F32)

    hb = hb_s[...]
    up = jnp.dot(hb, wu_ref[0], preferred_element_type=F32)
    gt = jnp.dot(hb, wg_ref[0], preferred_element_type=F32)
    lane = lax.broadcasted_iota(I32, gates_s.shape, 1)
    gcol = jnp.sum(jnp.where(lane == e, gates_s[...], 0.0), axis=1, keepdims=True)
    a = (gt * jax.nn.sigmoid(gt) * up * gcol).astype(BF16)
    acc_s[...] += jnp.dot(a, wd_ref[0], preferred_element_type=F32)

    @pl.when(e == N_EXPERTS - 1)
    def _():
        y = x_ref[...] + _per_segment(acc_s[...], gate_ref[0], lambda v, s: v * s)
        if final_norm:
            y = _rms(y, gfin_ref[...])
        out_ref[...] = y


def _moe(x, g, shift, scale, gate, w_router, router_bias, wg_bf, wu_bf, wd_bf, g_final, final_norm, tm):
    n, d = x.shape
    nt = n // tm
    nseg = tm // CHUNK
    ff = wg_bf.shape[2]
    wrt = w_router.T
    wrh = wrt.astype(BF16)
    wrl = (wrt - wrh.astype(F32)).astype(BF16)
    seg_spec = pl.BlockSpec((1, nseg, d), lambda t, e: (t, 0, 0))
    row_spec = pl.BlockSpec((1, d), lambda t, e: (0, 0))
    return pl.pallas_call(
        functools.partial(_moe_kernel, final_norm=final_norm),
        grid=(nt, N_EXPERTS),
        in_specs=[
            pl.BlockSpec(memory_space=pltpu.SMEM),
            pl.BlockSpec((tm, d), lambda t, e: (t, 0)),
            row_spec, seg_spec, seg_spec, seg_spec,
            pl.BlockSpec((N_EXPERTS, d), lambda t, e: (0, 0)),
            pl.BlockSpec((N_EXPERTS, d), lambda t, e: (0, 0)),
            pl.BlockSpec((1, d, ff), lambda t, e: (e, 0, 0)),
            pl.BlockSpec((1, d, ff), lambda t, e: (e, 0, 0)),
            pl.BlockSpec((1, ff, d), lambda t, e: (e, 0, 0)),
            row_spec,
        ],
        out_specs=pl.BlockSpec((tm, d), lambda t, e: (t, 0)),
        out_shape=jax.ShapeDtypeStruct((n, d), F32),
        scratch_shapes=[
            pltpu.VMEM((tm, d), BF16),
            pltpu.VMEM((N_EXPERTS, tm), F32),
            pltpu.VMEM((LANES, tm), F32),
            pltpu.VMEM((tm, LANES), F32),
            pltpu.VMEM((tm, d), F32),
        ],
        compiler_params=pltpu.CompilerParams(vmem_limit_bytes=VMEM_LIMIT),
        name="moe",
    )(router_bias, x, g.reshape(1, d), shift.reshape(nt, nseg, d), scale.reshape(nt, nseg, d),
      gate.reshape(nt, nseg, d), wrh, wrl, wg_bf, wu_bf, wd_bf, g_final.reshape(1, d))


def _dsa_in_weights(w):
    d = w.shape[0]
    o1 = DSA_HEADS * DSA_HEAD_DIM
    o2 = o1 + DSA_KV_HEADS * DSA_HEAD_DIM
    o3 = o2 + DSA_KV_HEADS * DSA_HEAD_DIM
    o4 = o3 + IDX_HEADS * IDX_DIM
    o5 = o4 + IDX_DIM
    wq = w[:, :o1].reshape(d, DSA_HEADS, 1, DSA_HEAD_DIM)
    half = ((np.arange(DSA_HEADS) // DSA_GROUP) % 2).reshape(1, DSA_HEADS, 1, 1)
    wq = jnp.concatenate([jnp.where(half == 0, wq, 0.0), jnp.where(half == 1, wq, 0.0)], axis=2)
    wki = w[:, o4:o5]
    zk = jnp.zeros_like(wki)
    wwi = jnp.pad(w[:, o5:], ((0, 0), (0, LANES - IDX_HEADS)))
    return jnp.concatenate([wq.reshape(d, 2 * o1), w[:, o1:o4], wki, zk, zk, wki, wwi], axis=1).astype(BF16)


def _dsa_out_weights(w):
    d = w.shape[1]
    w4 = w.reshape(DSA_HEADS, 1, DSA_HEAD_DIM, d)
    half = ((np.arange(DSA_HEADS) // DSA_GROUP) % 2).reshape(DSA_HEADS, 1, 1, 1)
    w4 = jnp.concatenate([jnp.where(half == 0, w4, 0.0), jnp.where(half == 1, w4, 0.0)], axis=1)
    return w4.reshape(2 * DSA_HEADS * DSA_HEAD_DIM, d).astype(BF16)


def kernel(x_prompt, x_sample, cache_k_diff, cache_v_diff, cache_k_dsa, cache_v_dsa, cache_kidx_dsa,
           c_prompt, c_sample, rel_bias, w_in_diff, w_out_diff, lam_diff, subln_g_diff, w_in_dsa, w_out_dsa,
           w_ada, b_ada, g_norm_mix, g_norm_ffn, w_router, router_bias, w_gate, w_up, w_down, g_final):
    b, s, d = x_prompt.shape
    bs, ss, _ = x_sample.shape
    past = cache_k_diff.shape[2]
    depth = w_ada.shape[0]
    assert d == D_MODEL and ss == CHUNK and s % 512 == 0 and past % CHUNK == 0 and depth == 2
    l_s = past + ss
    lp_s = -(-l_s // LANES) * LANES
    nkb_s = lp_s // LANES

    streams = []
    for x3, rows in ((x_prompt, np.repeat(np.arange(b), s // CHUNK)), (x_sample, b + np.arange(bs))):
        n = x3.shape[0] * x3.shape[1]
        streams.append((x3.reshape(n, d), x3.shape[:2], rows,
                        _pick_tile(n, (512, 256, 128, 64)), _pick_tile(n, (1024, 512, 256, 128, 64))))
    mod = _ada_mod(jnp.concatenate([c_prompt, c_sample], axis=0), w_ada, b_ada)

    def mods(layer, rows):
        m = mod[layer][rows]
        return [m[:, i * d:(i + 1) * d] for i in range(6)]

    def pad_keys(x):
        return jnp.pad(x, ((0, 0), (0, lp_s - x.shape[1]), (0, 0)))

    bias_s = _bias_tiles(rel_bias, [jb * LANES - past for jb in range(nkb_s)], [past] * nkb_s, l_s, ss, LANES)
    var_s = [list(range(nkb_s))]
    xs = [st[0] for st in streams]
    outs_kv = []

    qk = DIFF_HEADS * 2 * DIFF_HEAD_DIM
    lam_init = 0.8 - 0.6 * float(np.exp(-0.3 * 0))
    q_scale = DIFF_HEAD_DIM ** -0.5 * LOG2E
    w_in0, w_out0 = w_in_diff[0].astype(BF16), w_out_diff[0].astype(BF16)
    wt_k0 = w_in_diff[0][:, qk:2 * qk].T.astype(BF16)
    moe_w0 = (w_gate[0].astype(BF16), w_up[0].astype(BF16), w_down[0].astype(BF16))
    for si, (_, (nb, ns), rows, tm, tm_moe) in enumerate(streams):
        m = mods(0, rows)
        if si == 0:
            cols = ((0, qk, ((0, 0, q_scale),)), (qk, qk, ((2, 0, 1.0),)), (2 * qk, qk, ((1, 0, 1.0), (3, 0, 1.0))))
            outs = ((qk, BF16), (qk, F32), (qk, BF16), (qk, BF16))
            q_bf, v_f, k_bf, v_bf, k_t = _in_proj(xs[si], g_norm_mix[0], m[0], m[1], w_in0, cols, outs, tm,
                                                  wt_k0, ((0, qk),), ns)
            k_f = jnp.transpose(k_t.reshape(nb, DIFF_HEADS, 2, DIFF_HEAD_DIM, ns), (0, 4, 1, 2, 3))
        else:
            cols = ((0, qk, ((0, 0, q_scale),)), (qk, qk, ((1, 0, 1.0), (3, 0, 1.0))),
                    (2 * qk, qk, ((2, 0, 1.0), (4, 0, 1.0))))
            outs = ((qk, BF16), (qk, F32), (qk, F32), (qk, BF16), (qk, BF16))
            q_bf, k_f, v_f, k_bf, v_bf = _in_proj(xs[si], g_norm_mix[0], m[0], m[1], w_in0, cols, outs, tm)
        outs_kv.append((k_f, v_f))
        q3, k3, v3 = (a.reshape(nb, ns, qk) for a in (q_bf, k_bf, v_bf))
        if si == 0:
            o = _diff_prompt(q3, k3, v3, rel_bias, lam_diff[0], subln_g_diff[0], lam_init)
        else:
            ka = pad_keys(jnp.concatenate([cache_k_diff[0].reshape(nb, past, qk).astype(BF16), k3], axis=1))
            va = pad_keys(jnp.concatenate([cache_v_diff[0].reshape(nb, past, qk).astype(BF16), v3], axis=1))
            o = _diff_attn(q3, ka, va, bias_s, [(0, 0)], var_s, lam_diff[0], subln_g_diff[0], lam_init, ns, lp_s)
        x = _out_proj(o.reshape(nb * ns, qk), xs[si], m[2], w_out0, tm)
        xs[si] = _moe(x, g_norm_ffn[0], m[3], m[4], m[5], w_router, router_bias, *moe_w0, g_final, False, tm_moe)

    nq_w = 2 * DSA_HEADS * DSA_HEAD_DIM
    kv_w = DSA_KV_HEADS * DSA_HEAD_DIM
    qi_w = IDX_HEADS * IDX_DIM
    c_k, c_v, c_qi = nq_w, nq_w + kv_w, nq_w + 2 * kv_w
    c_ki, c_wi = c_qi + qi_w, c_qi + qi_w + 2 * LANES
    half_q = nq_w // 2
    q_scale = DSA_HEAD_DIM ** -0.5 * LOG2E
    cols_q = ((0, half_q, ((0, 0, q_scale),)), (half_q, half_q, ((0, half_q, q_scale),)))
    w_in1, w_out1 = _dsa_in_weights(w_in_dsa[0]), _dsa_out_weights(w_out_dsa[0])
    o_k = DSA_HEADS * DSA_HEAD_DIM
    o_v, o_ki = o_k + kv_w, o_k + 2 * kv_w + qi_w
    wt_1 = jnp.concatenate([w_in_dsa[0][:, o_k:o_v], w_in_dsa[0][:, o_v:o_v + kv_w],
                            w_in_dsa[0][:, o_ki:o_ki + IDX_DIM]], axis=1).T.astype(BF16)
    moe_w1 = (w_gate[1].astype(BF16), w_up[1].astype(BF16), w_down[1].astype(BF16))
    for si, (_, (nb, ns), rows, tm, tm_moe) in enumerate(streams):
        m = mods(1, rows)
        if si == 0:
            cols = cols_q + ((c_k, kv_w, ((1, 0, 1.0),)), (c_v, kv_w, ((2, 0, 1.0),)), (c_qi, qi_w, ((3, 0, 1.0),)),
                             (c_ki, 2 * LANES, ((4, 0, 1.0),)), (c_wi, LANES, ((5, 0, 1.0),)))
            outs = ((nq_w, BF16), (kv_w, BF16), (kv_w, BF16), (qi_w, BF16), (2 * LANES, BF16), (LANES, F32))
            q2, k_bf, v_bf, qi_bf, ki2_bf, wi_f, k_t, v_t, ki_t = _in_proj(
                xs[si], g_norm_mix[1], m[0], m[1], w_in1, cols, outs, tm, wt_1,
                ((0, kv_w), (kv_w, kv_w), (2 * kv_w, IDX_DIM)), ns)
            k_f = jnp.transpose(k_t.reshape(nb, DSA_KV_HEADS, DSA_HEAD_DIM, ns), (0, 3, 1, 2))
            v_f = jnp.transpose(v_t.reshape(nb, DSA_KV_HEADS, DSA_HEAD_DIM, ns), (0, 3, 1, 2))
            ki_f = jnp.transpose(ki_t, (0, 2, 1))
        else:
            cols = cols_q + ((c_k, kv_w, ((1, 0, 1.0), (3, 0, 1.0))), (c_v, kv_w, ((2, 0, 1.0), (4, 0, 1.0))),
                             (c_qi, qi_w, ((5, 0, 1.0),)), (c_ki, 2 * LANES, ((7, 0, 1.0),)),
                             (c_ki, IDX_DIM, ((6, 0, 1.0),)), (c_wi, LANES, ((8, 0, 1.0),)))
            outs = ((nq_w, BF16), (kv_w, F32), (kv_w, F32), (kv_w, BF16), (kv_w, BF16), (qi_w, BF16),
                    (IDX_DIM, F32), (2 * LANES, BF16), (LANES, F32))
            q2, k_f, v_f, k_bf, v_bf, qi_bf, ki_f, ki2_bf, wi_f = _in_proj(
                xs[si], g_norm_mix[1], m[0], m[1], w_in1, cols, outs, tm)
        outs_kv.append((k_f, v_f, ki_f))
        q3, qi3, wi3 = q2.reshape(nb, ns, nq_w), qi_bf.reshape(nb, ns, qi_w), wi_f.reshape(nb, ns, LANES)
        k3, v3, ki3 = k_bf.reshape(nb, ns, kv_w), v_bf.reshape(nb, ns, kv_w), ki2_bf.reshape(nb, ns, 2 * LANES)
        if si == 0:
            o = _dsa_prompt(q3, qi3, wi3, ki3, k3, v3, rel_bias)
        else:
            cki = cache_kidx_dsa[0].astype(BF16)
            zki = jnp.zeros_like(cki)
            ka = pad_keys(jnp.concatenate([cache_k_dsa[0].reshape(nb, past, kv_w).astype(BF16), k3], axis=1))
            va = pad_keys(jnp.concatenate([cache_v_dsa[0].reshape(nb, past, kv_w).astype(BF16), v3], axis=1))
            kia = pad_keys(jnp.concatenate([jnp.concatenate([cki, zki, zki, cki], axis=2), ki3], axis=1))
            o = _dsa_attn(q3, qi3, wi3, kia, ka, va, bias_s, [nkb_s], var_s, past, l_s,
                          min(IDX_TOPK, l_s // 4), ns, LANES)
        x = _out_proj(o.reshape(nb * ns, nq_w), xs[si], m[2], w_out1, tm)
        xs[si] = _moe(x, g_norm_ffn[1], m[3], m[4], m[5], w_router, router_bias, *moe_w1, g_final, True, tm_moe)

    h2 = (DIFF_HEADS, 2, DIFF_HEAD_DIM)
    hv = (DIFF_HEADS, 2 * DIFF_HEAD_DIM)
    hk = (DSA_KV_HEADS, DSA_HEAD_DIM)
    (kd_p, vd_p), (kd_s, vd_s), (ks_p, vs_p, ki_p), (ks_s, vs_s, ki_s) = outs_kv
    return (xs[0].reshape(b, s, d), xs[1].reshape(bs, ss, d),
            kd_p.reshape(1, b, s, *h2), vd_p.reshape(1, b, s, *hv),
            ks_p.reshape(1, b, s, *hk), vs_p.reshape(1, b, s, *hk), ki_p.reshape(1, b, s, IDX_DIM),
            kd_s.reshape(1, bs, ss, *h2), vd_s.reshape(1, bs, ss, *hv),
            ks_s.reshape(1, bs, ss, *hk), vs_s.reshape(1, bs, ss, *hk), ki_s.reshape(1, bs, ss, IDX_DIM))
```

```python
import functools

import jax
import jax.numpy as jnp
import numpy as np
from jax import lax
from jax.experimental import pallas as pl
from jax.experimental.pallas import tpu as pltpu

F32 = jnp.float32
BF16 = jnp.bfloat16
I32 = jnp.int32
I16 = jnp.int16

D_MODEL = 1024
CHUNK = 64
DIFF_HEADS = 8
DIFF_HEAD_DIM = 64
DSA_HEADS = 16
DSA_KV_HEADS = 4
DSA_GROUP = DSA_HEADS // DSA_KV_HEADS
DSA_HEAD_DIM = 64
IDX_HEADS = 8
IDX_DIM = 64
IDX_TOPK = 256
REL_BUCKETS = 32
BIAS_HEADS = 16
N_EXPERTS = 16
N_GROUPS = 4
EXPERTS_PER_GROUP = N_EXPERTS // N_GROUPS
EXPERT_FF = 512
RMS_EPS = 1e-6

LANES = 128
NEG = -1e30
LOG2E = 1.4426950408889634
INT_MIN = -(2 ** 31)
NEG_INF_KEY = -0x7F800000
VMEM_LIMIT = 56 * 1024 * 1024

_BUCKET_STEPS = (12, 16, 23, 32, 46, 64, 91)
FAR_BLOCKS = 2

_NT = (((1,), (1,)), ((), ()))


def _pick_tile(n, prefs):
    for t in prefs:
        if n % t == 0:
            return t
    raise ValueError(f"no tile in {prefs} divides {n}")


def _rms(x, g):
    ms = jnp.mean(x * x, axis=-1, keepdims=True)
    return x * lax.rsqrt(ms + RMS_EPS) * g


def _per_segment(x, seg_vals, op):
    tm, d = x.shape
    nseg = tm // CHUNK
    return op(x.reshape(nseg, CHUNK, d), seg_vals[:, None, :]).reshape(tm, d)


def _norm_mod(x, g, shift, scale):
    y = _rms(x, g)
    y = _per_segment(y, scale, lambda a, s: a * (1.0 + s))
    return _per_segment(y, shift, lambda a, s: a + s)


def _ada_kernel(c_ref, w_ref, b_ref, o_ref):
    c = c_ref[...]
    a = (c * jax.nn.sigmoid(c)).astype(BF16)
    o_ref[0] = jnp.dot(a, w_ref[0].astype(BF16), preferred_element_type=F32) + b_ref[0]


def _ada_mod(c_all, w_ada, b_ada):
    depth, d, n6 = w_ada.shape
    bc = c_all.shape[0]
    tn = _pick_tile(n6, (1536, 1024, 512, 128))
    return pl.pallas_call(
        _ada_kernel,
        grid=(depth, n6 // tn),
        in_specs=[
            pl.BlockSpec((bc, d), lambda l, j: (0, 0)),
            pl.BlockSpec((1, d, tn), lambda l, j: (l, 0, j)),
            pl.BlockSpec((1, 1, tn), lambda l, j: (l, 0, j)),
        ],
        out_specs=pl.BlockSpec((1, bc, tn), lambda l, j: (l, 0, j)),
        out_shape=jax.ShapeDtypeStruct((depth, bc, n6), F32),
        compiler_params=pltpu.CompilerParams(vmem_limit_bytes=VMEM_LIMIT),
        name="ada_mod",
    )(c_all, w_ada, b_ada.reshape(depth, 1, n6))


def _bias_kernel(off_ref, q0_ref, len_ref, rb_ref, o_ref, *, tq, tk):
    v = pl.program_id(0)
    h = pl.program_id(1)
    r = lax.broadcasted_iota(I32, (tq, tk), 0)
    c = lax.broadcasted_iota(I32, (tq, tk), 1)
    rel = off_ref[v] + c - r
    qpos = q0_ref[v] + r
    kpos = qpos + rel
    n = jnp.abs(rel)
    big = jnp.full((tq, tk), 8, I32)
    for t in _BUCKET_STEPS:
        big = big + jnp.where(n >= t, 1, 0)
    bucket = jnp.where(n < 8, n, big) + jnp.where(rel > 0, REL_BUCKETS // 2, 0)
    val = jnp.zeros((tq, tk), F32)
    for b in range(REL_BUCKETS):
        val = jnp.where(bucket == b, rb_ref[b, h], val)
    vis = jnp.where((kpos >> 6) <= (qpos >> 6), 1, 0) * jnp.where(kpos < len_ref[0], 1, 0)
    o_ref[0, 0] = jnp.where(vis > 0, val * LOG2E, NEG)


def _bias_tiles(rel_bias, offs, q0s, length, tq, tk):
    nvar = len(offs)
    return pl.pallas_call(
        functools.partial(_bias_kernel, tq=tq, tk=tk),
        grid_spec=pltpu.PrefetchScalarGridSpec(
            num_scalar_prefetch=3,
            grid=(nvar, BIAS_HEADS),
            in_specs=[pl.BlockSpec(memory_space=pltpu.SMEM)],
            out_specs=pl.BlockSpec((1, 1, tq, tk), lambda v, h, *_: (v, h, 0, 0)),
        ),
        out_shape=jax.ShapeDtypeStruct((nvar, BIAS_HEADS, tq, tk), F32),
        name="bias_tiles",
    )(jnp.asarray(offs, I32), jnp.asarray(q0s, I32), jnp.asarray([length], I32), rel_bias)


def _inproj_kernel(x_ref, g_ref, shift_ref, scale_ref, w_ref, wt_ref, *out_refs, cols, t_cols):
    h = _norm_mod(x_ref[...], g_ref[...], shift_ref[0], scale_ref[0]).astype(BF16)
    for start, width, dests in cols:
        p = jnp.dot(h, w_ref[:, start:start + width], preferred_element_type=F32)
        for out_idx, out_off, mult in dests:
            o_ref = out_refs[out_idx]
            val = p if mult == 1.0 else p * mult
            o_ref[:, out_off:out_off + width] = val.astype(o_ref.dtype)
    t_refs = out_refs[len(out_refs) - len(t_cols):]
    for (start, width), t_ref in zip(t_cols, t_refs):
        t_ref[0] = lax.dot_general(wt_ref[start:start + width, :], h, _NT, preferred_element_type=F32)


def _in_proj(x, g, shift, scale, w_bf, cols, out_defs, tm, wt_bf=None, t_cols=(), seq=None):
    n, d = x.shape
    nt = n // tm
    nseg = tm // CHUNK
    nout = w_bf.shape[1]
    if wt_bf is None:
        wt_bf = jnp.zeros((16, d), BF16)
    seg_spec = pl.BlockSpec((1, nseg, d), lambda t: (t, 0, 0))
    tpb = (seq or n) // tm
    return pl.pallas_call(
        functools.partial(_inproj_kernel, cols=cols, t_cols=t_cols),
        grid=(nt,),
        in_specs=[
            pl.BlockSpec((tm, d), lambda t: (t, 0)),
            pl.BlockSpec((1, d), lambda t: (0, 0)),
            seg_spec, seg_spec,
            pl.BlockSpec((d, nout), lambda t: (0, 0)),
            pl.BlockSpec(wt_bf.shape, lambda t: (0, 0)),
        ],
        out_specs=([pl.BlockSpec((tm, w), lambda t: (t, 0)) for w, _ in out_defs]
                   + [pl.BlockSpec((1, w, tm), lambda t: (t // tpb, 0, t % tpb)) for _, w in t_cols]),
        out_shape=([jax.ShapeDtypeStruct((n, w), dt) for w, dt in out_defs]
                   + [jax.ShapeDtypeStruct((n // (seq or n), w, seq or n), F32) for _, w in t_cols]),
        compiler_params=pltpu.CompilerParams(vmem_limit_bytes=VMEM_LIMIT),
        name="in_proj",
    )(x, g.reshape(1, d), shift.reshape(nt, nseg, d), scale.reshape(nt, nseg, d), w_bf, wt_bf)


def _outproj_kernel(o_ref, x_ref, gate_ref, w_ref, out_ref):
    r = jnp.dot(o_ref[...], w_ref[...], preferred_element_type=F32)
    out_ref[...] = x_ref[...] + _per_segment(r, gate_ref[0], lambda a, s: a * s)


def _out_proj(o, x, gate, w_bf, tm):
    n, d = x.shape
    kdim = o.shape[1]
    nt = n // tm
    nseg = tm // CHUNK
    return pl.pallas_call(
        _outproj_kernel,
        grid=(nt,),
        in_specs=[
            pl.BlockSpec((tm, kdim), lambda t: (t, 0)),
            pl.BlockSpec((tm, d), lambda t: (t, 0)),
            pl.BlockSpec((1, nseg, d), lambda t: (t, 0, 0)),
            pl.BlockSpec((kdim, d), lambda t: (0, 0)),
        ],
        out_specs=pl.BlockSpec((tm, d), lambda t: (t, 0)),
        out_shape=jax.ShapeDtypeStruct((n, d), F32),
        compiler_params=pltpu.CompilerParams(vmem_limit_bytes=VMEM_LIMIT),
        name="out_proj",
    )(o, x, gate.reshape(nt, nseg, d), w_bf)


def _diff_attn_kernel(it_ref, jt_ref, var_ref, fin_ref, lam_ref, q_ref, k_ref, v_ref, bias_ref, g_ref,
                      o_ref, qb_s, p_s, al_s, m_s, acc_s, *, tq, tk, rc, nkb, lam_init):
    t = pl.program_id(1)
    nchunk = 2 * tq // rc
    ncb = tk // LANES
    bvar = [var_ref[it_ref[t] * nkb + jt_ref[t] * ncb + cb] for cb in range(ncb)]

    @pl.when(jt_ref[t] == 0)
    def _():
        m_s[...] = jnp.full(m_s.shape, NEG, F32)
        acc_s[...] = jnp.zeros(acc_s.shape, F32)
        lane = lax.broadcasted_iota(I32, (tq, LANES), 1)
        for h in range(DIFF_HEADS):
            qh = q_ref[0, :, h * LANES:(h + 1) * LANES].astype(F32)
            qb_s[h, 0:tq, :] = jnp.where(lane < DIFF_HEAD_DIM, qh, 0.0).astype(BF16)
            qb_s[h, tq:2 * tq, :] = jnp.where(lane >= DIFF_HEAD_DIM, qh, 0.0).astype(BF16)

    def logits(h):
        return lax.dot_general(qb_s[h], k_ref[0, :, h * LANES:(h + 1) * LANES], _NT, preferred_element_type=F32)

    s_next = logits(0)
    for h in range(DIFF_HEADS):
        s_all = s_next
        if h + 1 < DIFF_HEADS:
            s_next = logits(h + 1)
        for c in range(nchunk):
            rows = slice(c * rc, (c + 1) * rc)
            brow = (c * rc) % tq
            hm = 2 * h + (c * rc) // tq
            s = jnp.concatenate(
                [s_all[rows, cb * LANES:(cb + 1) * LANES] + bias_ref[bvar[cb], hm, brow:brow + rc, :]
                 for cb in range(ncb)], axis=1)
            m_prev = m_s[h, rows]
            m_new = jnp.maximum(m_prev, jnp.max(s, axis=1, keepdims=True))
            p = jnp.exp2(s - jnp.concatenate([m_new] * ncb, axis=1))
            p_s[h, rows] = p.astype(BF16)
            al_s[h, rows] = jnp.exp2(m_prev - m_new)
            m_s[h, rows] = m_new
        vh = v_ref[0, :, h * LANES:(h + 1) * LANES]
        v1 = jnp.concatenate([vh, jnp.ones_like(vh)], axis=1)
        alpha = al_s[h]
        acc_s[h] = (jnp.concatenate([alpha, alpha], axis=1) * acc_s[h]
                    + jnp.dot(p_s[h], v1, preferred_element_type=F32))

    @pl.when(fin_ref[t] == 1)
    def _():
        lp = lam_ref[...]
        lam = (jnp.exp(jnp.sum(lp[0:1] * lp[1:2], axis=1, keepdims=True))
               - jnp.exp(jnp.sum(lp[2:3] * lp[3:4], axis=1, keepdims=True)) + lam_init)
        g = g_ref[...]
        for h in range(DIFF_HEADS):
            o0 = acc_s[h, 0:tq, 0:LANES] / acc_s[h, 0:tq, LANES:2 * LANES]
            o1 = acc_s[h, tq:2 * tq, 0:LANES] / acc_s[h, tq:2 * tq, LANES:2 * LANES]
            o = _rms(o0 - lam * o1, g) * (1.0 - lam_init)
            o_ref[0, :, h * LANES:(h + 1) * LANES] = o.astype(o_ref.dtype)


def _diff_attn(q, k, v, bias, pairs, var, lam_p, subln_g, lam_init, tq, tk):
    b, sq, dq = q.shape
    nvar = bias.shape[0]
    nkb = k.shape[1] // LANES
    it = np.asarray([p[0] for p in pairs], np.int32)
    jt = np.asarray([p[1] for p in pairs], np.int32)
    vt = np.asarray(var, np.int32).reshape(-1)
    fin = np.asarray([1 if (n + 1 == len(pairs) or pairs[n + 1][0] != p[0]) else 0
                      for n, p in enumerate(pairs)], np.int32)
    rc = min(tq, 64)
    return pl.pallas_call(
        functools.partial(_diff_attn_kernel, tq=tq, tk=tk, rc=rc, nkb=nkb, lam_init=lam_init),
        grid_spec=pltpu.PrefetchScalarGridSpec(
            num_scalar_prefetch=4,
            grid=(b, len(pairs)),
            in_specs=[
                pl.BlockSpec((4, DIFF_HEAD_DIM), lambda bb, t, *_: (0, 0)),
                pl.BlockSpec((1, tq, dq), lambda bb, t, it_, jt_, vt_, fin_: (bb, it_[t], 0)),
                pl.BlockSpec((1, tk, dq), lambda bb, t, it_, jt_, vt_, fin_: (bb, jt_[t], 0)),
                pl.BlockSpec((1, tk, dq), lambda bb, t, it_, jt_, vt_, fin_: (bb, jt_[t], 0)),
                pl.BlockSpec((nvar, BIAS_HEADS, tq, LANES), lambda bb, t, *_: (0, 0, 0, 0),
                             pipeline_mode=pl.Buffered(1)),
                pl.BlockSpec((1, 2 * DIFF_HEAD_DIM), lambda bb, t, *_: (0, 0)),
            ],
            out_specs=pl.BlockSpec((1, tq, dq), lambda bb, t, it_, jt_, vt_, fin_: (bb, it_[t], 0)),
            scratch_shapes=[
                pltpu.VMEM((DIFF_HEADS, 2 * tq, LANES), BF16),
                pltpu.VMEM((DIFF_HEADS, 2 * tq, tk), BF16),
                pltpu.VMEM((DIFF_HEADS, 2 * tq, LANES), F32),
                pltpu.VMEM((DIFF_HEADS, 2 * tq, LANES), F32),
                pltpu.VMEM((DIFF_HEADS, 2 * tq, 2 * LANES), F32),
            ],
        ),
        out_shape=jax.ShapeDtypeStruct((b, sq, dq), BF16),
        compiler_params=pltpu.CompilerParams(vmem_limit_bytes=VMEM_LIMIT),
        name="diff_attn",
    )(jnp.asarray(it), jnp.asarray(jt), jnp.asarray(vt), jnp.asarray(fin), lam_p, q, k, v, bias,
      subln_g.reshape(1, -1))


def _block_bias(rel_bias, nq, nkb, tq):
    bpq = tq // LANES
    lo, hi = -bpq, FAR_BLOCKS
    bias = _bias_tiles(rel_bias, [-LANES * d for d in range(lo, hi + 1)], [4 * tq] * (hi - lo + 1), 1 << 30, tq, LANES)
    var = [[min(max(bpq * i - jb, lo), hi) - lo for jb in range(nkb)] for i in range(nq)]
    return bias, var


def _diff_prompt(q, k, v, rel_bias, lam_p, subln_g, lam_init):
    s = q.shape[1]
    tq, tk = 256, 512
    nq = s // tq
    pairs = [(i, j) for i in range(nq) for j in range((i * tq) // tk + 1)]
    bias, var = _block_bias(rel_bias, nq, s // LANES, tq)
    return _diff_attn(q, k, v, bias, pairs, var, lam_p, subln_g, lam_init, tq, tk)


def _sort_key(x):
    bits = pltpu.bitcast(x, I32)
    return jnp.where(bits < 0, INT_MIN - bits, bits)


def _dsa_kernel(nvis_ref, var_ref, q_ref, qi_ref, wi_ref, ki_ref, k_ref, v_ref, bias_ref, o_ref,
                keys_s, khi_s, klo_s, qs_s, p_s, al_s, m_s, acc_s, *, tq, tk, nk, rc, q0, length, topk, nbits, search16):
    i = pl.program_id(1)
    nvis = nvis_ref[i]
    npair = DSA_KV_HEADS // 2
    hpp = 2 * DSA_GROUP
    nchunk = hpp * tq // rc

    for hd in range(DSA_HEADS):
        qs_s[hd // hpp, (hd % hpp) * tq:(hd % hpp + 1) * tq, :] = q_ref[0, :, hd * LANES:(hd + 1) * LANES]
    qi_stack = jnp.concatenate([qi_ref[0, :, u * LANES:(u + 1) * LANES] for u in range(IDX_HEADS // 2)], axis=0)
    wcols = [wi_ref[0, :, h:h + 1] for h in range(IDX_HEADS)]

    r_io = lax.broadcasted_iota(I32, (tq, tk), 0)
    c_io = lax.broadcasted_iota(I32, (tq, tk), 1)
    q_chunk = (q0 + i * tq + r_io) >> 6
    score_scale = IDX_DIM ** -0.5 * IDX_HEADS ** -0.5

    def score_tile(j):
        off = pl.multiple_of(j * tk, tk)
        ki_lo = ki_ref[0, pl.ds(off, tk), 0:LANES]
        ki_hi = ki_ref[0, pl.ds(off, tk), LANES:2 * LANES]
        s_even = lax.dot_general(qi_stack, ki_lo, _NT, preferred_element_type=F32)
        s_odd = lax.dot_general(qi_stack, ki_hi, _NT, preferred_element_type=F32)
        score = jnp.zeros((tq, tk), F32)
        for u in range(IDX_HEADS // 2):
            score = score + jnp.maximum(s_even[u * tq:(u + 1) * tq], 0.0) * wcols[2 * u]
            score = score + jnp.maximum(s_odd[u * tq:(u + 1) * tq], 0.0) * wcols[2 * u + 1]
        key = _sort_key(score * score_scale)
        kpos = j * tk + c_io
        vis = jnp.where((kpos >> 6) <= q_chunk, 1, 0) * jnp.where(kpos < length, 1, 0)
        key = jnp.where(vis > 0, key, INT_MIN)
        keys_s[j] = key
        if search16:
            khi_s[j] = (key >> 16).astype(I16)
            klo_s[j] = ((key & 0xFFFF) - 0x8000).astype(I16)

    def score_body(jj, carry):
        score_tile(2 * jj)
        score_tile(jnp.minimum(2 * jj + 1, nvis - 1))
        return carry

    lax.fori_loop(0, (nvis + 1) // 2, score_body, 0)

    def count16_gt(ref, below):
        b16 = jnp.broadcast_to(below, (tq, LANES)).astype(I16)

        def body(j, acc):
            kk = ref[j]
            for u in range(tk // LANES):
                acc = acc + jnp.where(kk[:, u * LANES:(u + 1) * LANES] > b16, jnp.int16(1), jnp.int16(0))
            return acc

        acc = lax.fori_loop(0, nvis, body, jnp.zeros((tq, LANES), I16))
        return jnp.sum(acc.astype(F32), axis=1, keepdims=True)

    def kth_largest16(ref, kth):
        def bit_body(it, ans):
            cand = ans | jnp.left_shift(jnp.int32(1), 15 - it)
            return jnp.where(count16_gt(ref, cand - 0x8001) >= kth, cand, ans)
        return lax.fori_loop(0, 16, bit_body, jnp.zeros((tq, 1), I32)) - 0x8000

    def count_ge(cand):
        below = jnp.broadcast_to(cand - 1, (tq, LANES))

        def body(j, acc):
            kk = keys_s[j]
            for u in range(tk // LANES):
                acc = acc + jnp.where(kk[:, u * LANES:(u + 1) * LANES] > below, 1.0, 0.0)
            return acc

        acc = lax.fori_loop(0, nvis, body, jnp.zeros((tq, LANES), F32))
        return jnp.sum(acc, axis=1, keepdims=True)

    if search16:
        thr_hi = kth_largest16(khi_s, jnp.full((tq, 1), float(topk), F32))
        n_above = count16_gt(khi_s, thr_hi)
        hi16 = jnp.broadcast_to(thr_hi, (tq, LANES)).astype(I16)

        def keep_low(j, carry):
            for u in range(tk // LANES):
                blk = slice(u * LANES, (u + 1) * LANES)
                klo_s[j, :, blk] = jnp.where(khi_s[j, :, blk] == hi16, klo_s[j, :, blk], jnp.int16(-0x8000))
            return carry

        lax.fori_loop(0, nvis, keep_low, 0)
        thr_lo = kth_largest16(klo_s, topk - n_above)
        thr = jnp.left_shift(thr_hi, 16) | (thr_lo + 0x8000)
    else:
        def bit_body(it, ans):
            cand = ans | jnp.left_shift(jnp.int32(1), 31 - it)
            return jnp.where(count_ge(cand ^ INT_MIN) >= topk, cand, ans)

        thr = lax.fori_loop(0, 32, bit_body, jnp.zeros((tq, 1), I32)) ^ INT_MIN
    thr_sel = jnp.maximum(thr, NEG_INF_KEY + 1)
    row_full = thr >= thr_sel

    cnt_ge = count_ge(thr_sel)
    cnt_gt = count_ge(thr_sel + 1)
    excess = jnp.max(jnp.where(row_full, cnt_ge - topk, 0.0))

    @pl.when(excess > 0.0)
    def _():
        quota = topk - cnt_gt

        def count_tied_before(pos):
            def body(j, acc):
                kpos = j * tk + c_io
                hit = jnp.where(keys_s[j] == thr, 1.0, 0.0) * jnp.where(kpos < pos, 1.0, 0.0)
                return acc + hit
            acc = lax.fori_loop(0, nvis, body, jnp.zeros((tq, tk), F32))
            return jnp.sum(acc, axis=1, keepdims=True)

        def pos_body(it, p):
            cand = p | jnp.left_shift(jnp.int32(1), nbits - 1 - it)
            return jnp.where(count_tied_before(cand) < quota, cand, p)

        last = lax.fori_loop(0, nbits, pos_body, jnp.zeros((tq, 1), I32))
        last = jnp.where(row_full, last, nk * tk)

        def demote(j, carry):
            kpos = j * tk + c_io
            kk = keys_s[j]
            drop = jnp.where(kk == thr, 1, 0) * jnp.where(kpos > last, 1, 0)
            keys_s[j] = jnp.where(drop > 0, thr - 1, kk)
            return carry

        lax.fori_loop(0, nvis, demote, 0)

    m_s[...] = jnp.full(m_s.shape, NEG, F32)
    acc_s[...] = jnp.zeros(acc_s.shape, F32)

    ncb = tk // LANES

    def attn_body(j, carry):
        off = pl.multiple_of(j * tk, tk)
        bvar = [var_ref[(i * nk + j) * ncb + cb] for cb in range(ncb)]
        maskb = jnp.where(keys_s[j] >= thr_sel, 0.0, NEG)
        s_pair = [lax.dot_general(qs_s[pr], k_ref[0, pl.ds(off, tk), pr * LANES:(pr + 1) * LANES], _NT,
                                  preferred_element_type=F32) for pr in range(npair)]
        for pr in range(npair):
            for c in range(nchunk):
                rows = slice(c * rc, (c + 1) * rc)
                hd = pr * hpp + (c * rc) // tq
                brow = (c * rc) % tq
                s = jnp.concatenate(
                    [s_pair[pr][rows, cb * LANES:(cb + 1) * LANES] + bias_ref[bvar[cb], hd, brow:brow + rc, :]
                     for cb in range(ncb)], axis=1) + maskb[brow:brow + rc]
                m_prev = m_s[pr, rows]
                m_new = jnp.maximum(m_prev, jnp.max(s, axis=1, keepdims=True))
                p = jnp.exp2(s - jnp.concatenate([m_new] * ncb, axis=1))
                p_s[pr, rows] = p.astype(BF16)
                al_s[pr, rows] = jnp.exp2(m_prev - m_new)
                m_s[pr, rows] = m_new
            vp = v_ref[0, pl.ds(off, tk), pr * LANES:(pr + 1) * LANES]
            v1 = jnp.concatenate([vp, jnp.ones_like(vp)], axis=1)
            alpha = al_s[pr]
            acc_s[pr] = (jnp.concatenate([alpha, alpha], axis=1) * acc_s[pr]
                         + jnp.dot(p_s[pr], v1, preferred_element_type=F32))
        return carry

    lax.fori_loop(0, nvis, attn_body, 0)

    for hd in range(DSA_HEADS):
        rows = slice((hd % hpp) * tq, (hd % hpp + 1) * tq)
        o = acc_s[hd // hpp, rows, 0:LANES] / acc_s[hd // hpp, rows, LANES:2 * LANES]
        o_ref[0, :, hd * LANES:(hd + 1) * LANES] = o.astype(o_ref.dtype)


def _dsa_attn(q, qi, wi, ki2, k, v, bias, nvis, var, q0, length, topk, tq, tk, search16, single_buffer):
    b, sq, dq = q.shape
    lp = k.shape[1]
    nq = sq // tq
    nk = lp // tk
    nvar = bias.shape[0]
    npair = DSA_KV_HEADS // 2
    prow = 2 * DSA_GROUP * tq
    kernel = functools.partial(_dsa_kernel, tq=tq, tk=tk, nk=nk, rc=min(tq, 64), q0=q0, length=length,
                               topk=topk, nbits=max(1, (lp - 1).bit_length()), search16=search16)
    kv_mode = dict(pipeline_mode=pl.Buffered(1)) if single_buffer else {}
    full = lambda bb, ii, *_: (bb, 0, 0)
    tile = lambda bb, ii, *_: (bb, ii, 0)
    return pl.pallas_call(
        kernel,
        grid_spec=pltpu.PrefetchScalarGridSpec(
            num_scalar_prefetch=2,
            grid=(b, nq),
            in_specs=[
                pl.BlockSpec((1, tq, dq), tile),
                pl.BlockSpec((1, tq, qi.shape[2]), tile),
                pl.BlockSpec((1, tq, LANES), tile),
                pl.BlockSpec((1, lp, 2 * LANES), full, **kv_mode),
                pl.BlockSpec((1, lp, k.shape[2]), full, **kv_mode),
                pl.BlockSpec((1, lp, v.shape[2]), full, **kv_mode),
                pl.BlockSpec((nvar, BIAS_HEADS, tq, LANES), lambda bb, ii, *_: (0, 0, 0, 0),
                             pipeline_mode=pl.Buffered(1)),
            ],
            out_specs=pl.BlockSpec((1, tq, dq), tile),
            scratch_shapes=[
                pltpu.VMEM((nk, tq, tk), I32),
                pltpu.VMEM((nk, tq, tk), I16),
                pltpu.VMEM((nk, tq, tk), I16),
                pltpu.VMEM((npair, prow, LANES), BF16),
                pltpu.VMEM((npair, prow, tk), BF16),
                pltpu.VMEM((npair, prow, LANES), F32),
                pltpu.VMEM((npair, prow, LANES), F32),
                pltpu.VMEM((npair, prow, 2 * LANES), F32),
            ],
        ),
        out_shape=jax.ShapeDtypeStruct((b, sq, dq), BF16),
        compiler_params=pltpu.CompilerParams(vmem_limit_bytes=VMEM_LIMIT),
        name="dsa_attn",
    )(jnp.asarray(nvis, I32), jnp.asarray(var, I32).reshape(-1), q, qi, wi, ki2, k, v, bias)


def _dsa_prompt(q, qi, wi, ki2, k, v, rel_bias):
    s = q.shape[1]
    tq, tk = 128, 512
    nq = s // tq
    nvis = [(i * tq) // tk + 1 for i in range(nq)]
    bias, var = _block_bias(rel_bias, nq, s // LANES, tq)
    return _dsa_attn(q, qi, wi, ki2, k, v, bias, nvis, var, 0, s, min(IDX_TOPK, s // 4), tq, tk, True, False)


def _route(lt_s, rb_ref, gt_s):
    rows = [lt_s[e:e + 1, :] for e in range(N_EXPERTS)]
    mx = functools.reduce(jnp.maximum, rows)
    ex = [jnp.exp(r - mx) for r in rows]
    z = functools.reduce(lambda a, b: a + b, ex)
    probs = [x / z for x in ex]
    sel = [probs[e] + rb_ref[e] for e in range(N_EXPERTS)]
    best_val, best_grp = None, None
    for gidx in range(N_GROUPS):
        a, b, c, d = sel[EXPERTS_PER_GROUP * gidx:EXPERTS_PER_GROUP * (gidx + 1)]
        hi1, lo1, hi2, lo2 = jnp.maximum(a, b), jnp.minimum(a, b), jnp.maximum(c, d), jnp.minimum(c, d)
        top2 = jnp.maximum(hi1, hi2) + jnp.maximum(jnp.minimum(hi1, hi2), jnp.maximum(lo1, lo2))
        if gidx == 0:
            best_val, best_grp = top2, jnp.zeros_like(top2, dtype=I32)
        else:
            better = top2 > best_val
            best_grp = jnp.where(better, gidx, best_grp)
            best_val = jnp.where(better, top2, best_val)
    masked = [jnp.where(best_grp == e // EXPERTS_PER_GROUP, sel[e], -jnp.inf) for e in range(N_EXPERTS)]

    def first_argmax(vals):
        bv, bi = vals[0], jnp.zeros_like(best_grp)
        for e in range(1, N_EXPERTS):
            better = vals[e] > bv
            bi = jnp.where(better, e, bi)
            bv = jnp.where(better, vals[e], bv)
        return bi

    i1 = first_argmax(masked)
    i2 = first_argmax([jnp.where(i1 == e, -jnp.inf, masked[e]) for e in range(N_EXPERTS)])
    w1 = functools.reduce(lambda a, b: a + b, [jnp.where(i1 == e, probs[e], 0.0) for e in range(N_EXPERTS)])
    w2 = functools.reduce(lambda a, b: a + b, [jnp.where(i2 == e, probs[e], 0.0) for e in range(N_EXPERTS)])
    wsum = w1 + w2
    w1, w2 = w1 / wsum, w2 / wsum
    for e in range(N_EXPERTS):
        gt_s[e:e + 1, :] = jnp.where(i1 == e, w1, 0.0) + jnp.where(i2 == e, w2, 0.0)


def _moe_kernel(rb_ref, x_ref, g_ref, shift_ref, scale_ref, gate_ref, wrh_ref, wrl_ref, wg_ref, wu_ref, wd_ref,
                gfin_ref, out_ref, hb_s, lt_s, gt_s, gates_s, acc_s, *, final_norm):
    e = pl.program_id(1)

    @pl.when(e == 0)
    def _():
        h = _norm_mod(x_ref[...], g_ref[...], shift_ref[0], scale_ref[0])
        hb = h.astype(BF16)
        hb_s[...] = hb
        hl = (h - hb.astype(F32)).astype(BF16)
        lt_s[...] = (lax.dot_general(wrh_ref[...], hb, _NT, preferred_element_type=F32)
                     + lax.dot_general(wrl_ref[...], hb, _NT, preferred_element_type=F32)
                     + lax.dot_general(wrh_ref[...], hl, _NT, preferred_element_type=F32))
        gt_s[...] = jnp.zeros(gt_s.shape, F32)
        _route(lt_s, rb_ref, gt_s)
        gates_s[...] = gt_s[...].T
        acc_s[...] = jnp.zeros(acc_s.shape, F32)

    hb = hb_s[...]
    up = jnp.dot(hb, wu_ref[0], preferred_element_type=F32)
    gt = jnp.dot(hb, wg_ref[0], preferred_element_type=F32)
    lane = lax.broadcasted_iota(I32, gates_s.shape, 1)
    gcol = jnp.sum(jnp.where(lane == e, gates_s[...], 0.0), axis=1, keepdims=True)
    a = (gt * jax.nn.sigmoid(gt) * up * gcol).astype(BF16)
    acc_s[...] += jnp.dot(a, wd_ref[0], preferred_element_type=F32)

    @pl.when(e == N_EXPERTS - 1)
    def _():
        y = x_ref[...] + _per_segment(acc_s[...], gate_ref[0], lambda v, s: v * s)
        if final_norm:
            y = _rms(y, gfin_ref[...])
        out_ref[...] = y


def _moe(x, g, shift, scale, gate, w_router, router_bias, wg_bf, wu_bf, wd_bf, g_final, final_norm, tm):
    n, d = x.shape
    nt = n // tm
    nseg = tm // CHUNK
    ff = wg_bf.shape[2]
    wrt = w_router.T
    wrh = wrt.astype(BF16)
    wrl = (wrt - wrh.astype(F32)).astype(BF16)
    seg_spec = pl.BlockSpec((1, nseg, d), lambda t, e: (t, 0, 0))
    row_spec = pl.BlockSpec((1, d), lambda t, e: (0, 0))
    return pl.pallas_call(
        functools.partial(_moe_kernel, final_norm=final_norm),
        grid=(nt, N_EXPERTS),
        in_specs=[
            pl.BlockSpec(memory_space=pltpu.SMEM),
            pl.BlockSpec((tm, d), lambda t, e: (t, 0)),
            row_spec, seg_spec, seg_spec, seg_spec,
            pl.BlockSpec((N_EXPERTS, d), lambda t, e: (0, 0)),
            pl.BlockSpec((N_EXPERTS, d), lambda t, e: (0, 0)),
            pl.BlockSpec((1, d, ff), lambda t, e: (e, 0, 0)),
            pl.BlockSpec((1, d, ff), lambda t, e: (e, 0, 0)),
            pl.BlockSpec((1, ff, d), lambda t, e: (e, 0, 0)),
            row_spec,
        ],
        out_specs=pl.BlockSpec((tm, d), lambda t, e: (t, 0)),
        out_shape=jax.ShapeDtypeStruct((n, d), F32),
        scratch_shapes=[
            pltpu.VMEM((tm, d), BF16),
            pltpu.VMEM((N_EXPERTS, tm), F32),
            pltpu.VMEM((LANES, tm), F32),
            pltpu.VMEM((tm, LANES), F32),
            pltpu.VMEM((tm, d), F32),
        ],
        compiler_params=pltpu.CompilerParams(vmem_limit_bytes=VMEM_LIMIT),
        name="moe",
    )(router_bias, x, g.reshape(1, d), shift.reshape(nt, nseg, d), scale.reshape(nt, nseg, d),
      gate.reshape(nt, nseg, d), wrh, wrl, wg_bf, wu_bf, wd_bf, g_final.reshape(1, d))


def _dsa_in_weights(w):
    d = w.shape[0]
    o1 = DSA_HEADS * DSA_HEAD_DIM
    o2 = o1 + DSA_KV_HEADS * DSA_HEAD_DIM
    o3 = o2 + DSA_KV_HEADS * DSA_HEAD_DIM
    o4 = o3 + IDX_HEADS * IDX_DIM
    o5 = o4 + IDX_DIM
    wq = w[:, :o1].reshape(d, DSA_HEADS, 1, DSA_HEAD_DIM)
    half = ((np.arange(DSA_HEADS) // DSA_GROUP) % 2).reshape(1, DSA_HEADS, 1, 1)
    wq = jnp.concatenate([jnp.where(half == 0, wq, 0.0), jnp.where(half == 1, wq, 0.0)], axis=2)
    wki = w[:, o4:o5]
    zk = jnp.zeros_like(wki)
    wwi = jnp.pad(w[:, o5:], ((0, 0), (0, LANES - IDX_HEADS)))
    return jnp.concatenate([wq.reshape(d, 2 * o1), w[:, o1:o4], wki, zk, zk, wki, wwi], axis=1).astype(BF16)


def _dsa_out_weights(w):
    d = w.shape[1]
    w4 = w.reshape(DSA_HEADS, 1, DSA_HEAD_DIM, d)
    half = ((np.arange(DSA_HEADS) // DSA_GROUP) % 2).reshape(DSA_HEADS, 1, 1, 1)
    w4 = jnp.concatenate([jnp.where(half == 0, w4, 0.0), jnp.where(half == 1, w4, 0.0)], axis=1)
    return w4.reshape(2 * DSA_HEADS * DSA_HEAD_DIM, d).astype(BF16)


def kernel(x_prompt, x_sample, cache_k_diff, cache_v_diff, cache_k_dsa, cache_v_dsa, cache_kidx_dsa,
           c_prompt, c_sample, rel_bias, w_in_diff, w_out_diff, lam_diff, subln_g_diff, w_in_dsa, w_out_dsa,
           w_ada, b_ada, g_norm_mix, g_norm_ffn, w_router, router_bias, w_gate, w_up, w_down, g_final):
    b, s, d = x_prompt.shape
    bs, ss, _ = x_sample.shape
    past = cache_k_diff.shape[2]
    depth = w_ada.shape[0]
    assert d == D_MODEL and ss == CHUNK and s % 512 == 0 and past % CHUNK == 0 and depth == 2
    l_s = past + ss
    lp_s = -(-l_s // LANES) * LANES
    nkb_s = lp_s // LANES

    streams = []
    for x3, rows in ((x_prompt, np.repeat(np.arange(b), s // CHUNK)), (x_sample, b + np.arange(bs))):
        n = x3.shape[0] * x3.shape[1]
        streams.append((x3.reshape(n, d), x3.shape[:2], rows,
                        _pick_tile(n, (512, 256, 128, 64)), _pick_tile(n, (1024, 512, 256, 128, 64))))
    mod = _ada_mod(jnp.concatenate([c_prompt, c_sample], axis=0), w_ada, b_ada)

    def mods(layer, rows):
        m = mod[layer][rows]
        return [m[:, i * d:(i + 1) * d] for i in range(6)]

    def pad_keys(x):
        return jnp.pad(x, ((0, 0), (0, lp_s - x.shape[1]), (0, 0)))

    bias_s = _bias_tiles(rel_bias, [jb * LANES - past for jb in range(nkb_s)], [past] * nkb_s, l_s, ss, LANES)
    var_s = [list(range(nkb_s))]
    xs = [st[0] for st in streams]
    outs_kv = []

    qk = DIFF_HEADS * 2 * DIFF_HEAD_DIM
    lam_init = 0.8 - 0.6 * float(np.exp(-0.3 * 0))
    q_scale = DIFF_HEAD_DIM ** -0.5 * LOG2E
    w_in0, w_out0 = w_in_diff[0].astype(BF16), w_out_diff[0].astype(BF16)
    wt_k0 = w_in_diff[0][:, qk:2 * qk].T.astype(BF16)
    moe_w0 = (w_gate[0].astype(BF16), w_up[0].astype(BF16), w_down[0].astype(BF16))
    for si, (_, (nb, ns), rows, tm, tm_moe) in enumerate(streams):
        m = mods(0, rows)
        if si == 0:
            cols = ((0, qk, ((0, 0, q_scale),)), (qk, qk, ((2, 0, 1.0),)), (2 * qk, qk, ((1, 0, 1.0), (3, 0, 1.0))))
            outs = ((qk, BF16), (qk, F32), (qk, BF16), (qk, BF16))
            q_bf, v_f, k_bf, v_bf, k_t = _in_proj(xs[si], g_norm_mix[0], m[0], m[1], w_in0, cols, outs, tm,
                                                  wt_k0, ((0, qk),), ns)
            k_f = jnp.transpose(k_t.reshape(nb, DIFF_HEADS, 2, DIFF_HEAD_DIM, ns), (0, 4, 1, 2, 3))
        else:
            cols = ((0, qk, ((0, 0, q_scale),)), (qk, qk, ((1, 0, 1.0), (3, 0, 1.0))),
                    (2 * qk, qk, ((2, 0, 1.0), (4, 0, 1.0))))
            outs = ((qk, BF16), (qk, F32), (qk, F32), (qk, BF16), (qk, BF16))
            q_bf, k_f, v_f, k_bf, v_bf = _in_proj(xs[si], g_norm_mix[0], m[0], m[1], w_in0, cols, outs, tm)
        outs_kv.append((k_f, v_f))
        q3, k3, v3 = (a.reshape(nb, ns, qk) for a in (q_bf, k_bf, v_bf))
        if si == 0:
            o = _diff_prompt(q3, k3, v3, rel_bias, lam_diff[0], subln_g_diff[0], lam_init)
        else:
            ka = pad_keys(jnp.concatenate([cache_k_diff[0].reshape(nb, past, qk).astype(BF16), k3], axis=1))
            va = pad_keys(jnp.concatenate([cache_v_diff[0].reshape(nb, past, qk).astype(BF16), v3], axis=1))
            o = _diff_attn(q3, ka, va, bias_s, [(0, 0)], var_s, lam_diff[0], subln_g_diff[0], lam_init, ns, lp_s)
        x = _out_proj(o.reshape(nb * ns, qk), xs[si], m[2], w_out0, tm)
        xs[si] = _moe(x, g_norm_ffn[0], m[3], m[4], m[5], w_router, router_bias, *moe_w0, g_final, False, tm_moe)

    nq_w = 2 * DSA_HEADS * DSA_HEAD_DIM
    kv_w = DSA_KV_HEADS * DSA_HEAD_DIM
    qi_w = IDX_HEADS * IDX_DIM
    c_k, c_v, c_qi = nq_w, nq_w + kv_w, nq_w + 2 * kv_w
    c_ki, c_wi = c_qi + qi_w, c_qi + qi_w + 2 * LANES
    half_q = nq_w // 2
    q_scale = DSA_HEAD_DIM ** -0.5 * LOG2E
    cols_q = ((0, half_q, ((0, 0, q_scale),)), (half_q, half_q, ((0, half_q, q_scale),)))
    w_in1, w_out1 = _dsa_in_weights(w_in_dsa[0]), _dsa_out_weights(w_out_dsa[0])
    o_k = DSA_HEADS * DSA_HEAD_DIM
    o_v, o_ki = o_k + kv_w, o_k + 2 * kv_w + qi_w
    wt_1 = jnp.concatenate([w_in_dsa[0][:, o_k:o_v], w_in_dsa[0][:, o_v:o_v + kv_w],
                            w_in_dsa[0][:, o_ki:o_ki + IDX_DIM]], axis=1).T.astype(BF16)
    moe_w1 = (w_gate[1].astype(BF16), w_up[1].astype(BF16), w_down[1].astype(BF16))
    for si, (_, (nb, ns), rows, tm, tm_moe) in enumerate(streams):
        m = mods(1, rows)
        if si == 0:
            cols = cols_q + ((c_k, kv_w, ((1, 0, 1.0),)), (c_v, kv_w, ((2, 0, 1.0),)), (c_qi, qi_w, ((3, 0, 1.0),)),
                             (c_ki, 2 * LANES, ((4, 0, 1.0),)), (c_wi, LANES, ((5, 0, 1.0),)))
            outs = ((nq_w, BF16), (kv_w, BF16), (kv_w, BF16), (qi_w, BF16), (2 * LANES, BF16), (LANES, F32))
            q2, k_bf, v_bf, qi_bf, ki2_bf, wi_f, k_t, v_t, ki_t = _in_proj(
                xs[si], g_norm_mix[1], m[0], m[1], w_in1, cols, outs, tm, wt_1,
                ((0, kv_w), (kv_w, kv_w), (2 * kv_w, IDX_DIM)), ns)
            k_f = jnp.transpose(k_t.reshape(nb, DSA_KV_HEADS, DSA_HEAD_DIM, ns), (0, 3, 1, 2))
            v_f = jnp.transpose(v_t.reshape(nb, DSA_KV_HEADS, DSA_HEAD_DIM, ns), (0, 3, 1, 2))
            ki_f = jnp.transpose(ki_t, (0, 2, 1))
        else:
            cols = cols_q + ((c_k, kv_w, ((1, 0, 1.0), (3, 0, 1.0))), (c_v, kv_w, ((2, 0, 1.0), (4, 0, 1.0))),
                             (c_qi, qi_w, ((5, 0, 1.0),)), (c_ki, 2 * LANES, ((7, 0, 1.0),)),
                             (c_ki, IDX_DIM, ((6, 0, 1.0),)), (c_wi, LANES, ((8, 0, 1.0),)))
            outs = ((nq_w, BF16), (kv_w, F32), (kv_w, F32), (kv_w, BF16), (kv_w, BF16), (qi_w, BF16),
                    (IDX_DIM, F32), (2 * LANES, BF16), (LANES, F32))
            q2, k_f, v_f, k_bf, v_bf, qi_bf, ki_f, ki2_bf, wi_f = _in_proj(
                xs[si], g_norm_mix[1], m[0], m[1], w_in1, cols, outs, tm)
        outs_kv.append((k_f, v_f, ki_f))
        q3, qi3, wi3 = q2.reshape(nb, ns, nq_w), qi_bf.reshape(nb, ns, qi_w), wi_f.reshape(nb, ns, LANES)
        k3, v3, ki3 = k_bf.reshape(nb, ns, kv_w), v_bf.reshape(nb, ns, kv_w), ki2_bf.reshape(nb, ns, 2 * LANES)
        if si == 0:
            o = _dsa_prompt(q3, qi3, wi3, ki3, k3, v3, rel_bias)
        else:
            cki = cache_kidx_dsa[0].astype(BF16)
            zki = jnp.zeros_like(cki)
            ka = pad_keys(jnp.concatenate([cache_k_dsa[0].reshape(nb, past, kv_w).astype(BF16), k3], axis=1))
            va = pad_keys(jnp.concatenate([cache_v_dsa[0].reshape(nb, past, kv_w).astype(BF16), v3], axis=1))
            kia = pad_keys(jnp.concatenate([jnp.concatenate([cki, zki, zki, cki], axis=2), ki3], axis=1))
            o = _dsa_attn(q3, qi3, wi3, kia, ka, va, bias_s, [nkb_s], var_s, past, l_s,
                          min(IDX_TOPK, l_s // 4), ns, LANES, False, True)
        x = _out_proj(o.reshape(nb * ns, nq_w), xs[si], m[2], w_out1, tm)
        xs[si] = _moe(x, g_norm_ffn[1], m[3], m[4], m[5], w_router, router_bias, *moe_w1, g_final, True, tm_moe)

    h2 = (DIFF_HEADS, 2, DIFF_HEAD_DIM)
    hv = (DIFF_HEADS, 2 * DIFF_HEAD_DIM)
    hk = (DSA_KV_HEADS, DSA_HEAD_DIM)
    (kd_p, vd_p), (kd_s, vd_s), (ks_p, vs_p, ki_p), (ks_s, vs_s, ki_s) = outs_kv
    return (xs[0].reshape(b, s, d), xs[1].reshape(bs, ss, d),
            kd_p.reshape(1, b, s, *h2), vd_p.reshape(1, b, s, *hv),
            ks_p.reshape(1, b, s, *hk), vs_p.reshape(1, b, s, *hk), ki_p.reshape(1, b, s, IDX_DIM),
            kd_s.reshape(1, bs, ss, *h2), vd_s.reshape(1, bs, ss, *hv),
            ks_s.reshape(1, bs, ss, *hk), vs_s.reshape(1, bs, ss, *hk), ki_s.reshape(1, bs, ss, IDX_DIM))
```

```python
import functools

import jax
import jax.numpy as jnp
import numpy as np
from jax import lax
from jax.experimental import pallas as pl
from jax.experimental.pallas import tpu as pltpu

F32 = jnp.float32
BF16 = jnp.bfloat16
I32 = jnp.int32

D_MODEL = 1024
CHUNK = 64
DIFF_HEADS = 8
DIFF_HEAD_DIM = 64
DSA_HEADS = 16
DSA_KV_HEADS = 4
DSA_GROUP = DSA_HEADS // DSA_KV_HEADS
DSA_HEAD_DIM = 64
IDX_HEADS = 8
IDX_DIM = 64
IDX_TOPK = 256
REL_BUCKETS = 32
BIAS_HEADS = 16
N_EXPERTS = 16
N_GROUPS = 4
EXPERTS_PER_GROUP = N_EXPERTS // N_GROUPS
EXPERT_FF = 512
RMS_EPS = 1e-6

LANES = 128
NEG = -1e30
LOG2E = 1.4426950408889634
INT_MIN = -(2 ** 31)
NEG_INF_KEY = -0x7F800000
VMEM_LIMIT = 56 * 1024 * 1024

_BUCKET_STEPS = (12, 16, 23, 32, 46, 64, 91)
FAR_BLOCKS = 2

_NT = (((1,), (1,)), ((), ()))


def _pick_tile(n, prefs):
    for t in prefs:
        if n % t == 0:
            return t
    raise ValueError(f"no tile in {prefs} divides {n}")


def _rms(x, g):
    ms = jnp.mean(x * x, axis=-1, keepdims=True)
    return x * lax.rsqrt(ms + RMS_EPS) * g


def _per_segment(x, seg_vals, op):
    tm, d = x.shape
    nseg = tm // CHUNK
    return op(x.reshape(nseg, CHUNK, d), seg_vals[:, None, :]).reshape(tm, d)


def _norm_mod(x, g, shift, scale):
    y = _rms(x, g)
    y = _per_segment(y, scale, lambda a, s: a * (1.0 + s))
    return _per_segment(y, shift, lambda a, s: a + s)


def _ada_kernel(c_ref, w_ref, b_ref, o_ref):
    c = c_ref[...]
    a = (c * jax.nn.sigmoid(c)).astype(BF16)
    o_ref[0] = jnp.dot(a, w_ref[0].astype(BF16), preferred_element_type=F32) + b_ref[0]


def _ada_mod(c_all, w_ada, b_ada):
    depth, d, n6 = w_ada.shape
    bc = c_all.shape[0]
    tn = _pick_tile(n6, (1536, 1024, 512, 128))
    return pl.pallas_call(
        _ada_kernel,
        grid=(depth, n6 // tn),
        in_specs=[
            pl.BlockSpec((bc, d), lambda l, j: (0, 0)),
            pl.BlockSpec((1, d, tn), lambda l, j: (l, 0, j)),
            pl.BlockSpec((1, 1, tn), lambda l, j: (l, 0, j)),
        ],
        out_specs=pl.BlockSpec((1, bc, tn), lambda l, j: (l, 0, j)),
        out_shape=jax.ShapeDtypeStruct((depth, bc, n6), F32),
        compiler_params=pltpu.CompilerParams(vmem_limit_bytes=VMEM_LIMIT),
        name="ada_mod",
    )(c_all, w_ada, b_ada.reshape(depth, 1, n6))


def _bias_kernel(off_ref, q0_ref, len_ref, rb_ref, o_ref, *, tq, tk):
    v = pl.program_id(0)
    h = pl.program_id(1)
    r = lax.broadcasted_iota(I32, (tq, tk), 0)
    c = lax.broadcasted_iota(I32, (tq, tk), 1)
    rel = off_ref[v] + c - r
    qpos = q0_ref[v] + r
    kpos = qpos + rel
    n = jnp.abs(rel)
    big = jnp.full((tq, tk), 8, I32)
    for t in _BUCKET_STEPS:
        big = big + jnp.where(n >= t, 1, 0)
    bucket = jnp.where(n < 8, n, big) + jnp.where(rel > 0, REL_BUCKETS // 2, 0)
    val = jnp.zeros((tq, tk), F32)
    for b in range(REL_BUCKETS):
        val = jnp.where(bucket == b, rb_ref[b, h], val)
    vis = jnp.where((kpos >> 6) <= (qpos >> 6), 1, 0) * jnp.where(kpos < len_ref[0], 1, 0)
    o_ref[0, 0] = jnp.where(vis > 0, val * LOG2E, NEG)


def _bias_tiles(rel_bias, offs, q0s, length, tq, tk):
    nvar = len(offs)
    return pl.pallas_call(
        functools.partial(_bias_kernel, tq=tq, tk=tk),
        grid_spec=pltpu.PrefetchScalarGridSpec(
            num_scalar_prefetch=3,
            grid=(nvar, BIAS_HEADS),
            in_specs=[pl.BlockSpec(memory_space=pltpu.SMEM)],
            out_specs=pl.BlockSpec((1, 1, tq, tk), lambda v, h, *_: (v, h, 0, 0)),
        ),
        out_shape=jax.ShapeDtypeStruct((nvar, BIAS_HEADS, tq, tk), F32),
        name="bias_tiles",
    )(jnp.asarray(offs, I32), jnp.asarray(q0s, I32), jnp.asarray([length], I32), rel_bias)


def _inproj_kernel(x_ref, g_ref, shift_ref, scale_ref, w_ref, wt_ref, *out_refs, cols, t_cols):
    h = _norm_mod(x_ref[...], g_ref[...], shift_ref[0], scale_ref[0]).astype(BF16)
    for start, width, dests in cols:
        p = jnp.dot(h, w_ref[:, start:start + width], preferred_element_type=F32)
        for out_idx, out_off, mult in dests:
            o_ref = out_refs[out_idx]
            val = p if mult == 1.0 else p * mult
            o_ref[:, out_off:out_off + width] = val.astype(o_ref.dtype)
    t_refs = out_refs[len(out_refs) - len(t_cols):]
    for (start, width), t_ref in zip(t_cols, t_refs):
        t_ref[0] = lax.dot_general(wt_ref[start:start + width, :], h, _NT, preferred_element_type=F32)


def _in_proj(x, g, shift, scale, w_bf, cols, out_defs, tm, wt_bf=None, t_cols=(), seq=None):
    n, d = x.shape
    nt = n // tm
    nseg = tm // CHUNK
    nout = w_bf.shape[1]
    if wt_bf is None:
        wt_bf = jnp.zeros((16, d), BF16)
    seg_spec = pl.BlockSpec((1, nseg, d), lambda t: (t, 0, 0))
    tpb = (seq or n) // tm
    return pl.pallas_call(
        functools.partial(_inproj_kernel, cols=cols, t_cols=t_cols),
        grid=(nt,),
        in_specs=[
            pl.BlockSpec((tm, d), lambda t: (t, 0)),
            pl.BlockSpec((1, d), lambda t: (0, 0)),
            seg_spec, seg_spec,
            pl.BlockSpec((d, nout), lambda t: (0, 0)),
            pl.BlockSpec(wt_bf.shape, lambda t: (0, 0)),
        ],
        out_specs=([pl.BlockSpec((tm, w), lambda t: (t, 0)) for w, _ in out_defs]
                   + [pl.BlockSpec((1, w, tm), lambda t: (t // tpb, 0, t % tpb)) for _, w in t_cols]),
        out_shape=([jax.ShapeDtypeStruct((n, w), dt) for w, dt in out_defs]
                   + [jax.ShapeDtypeStruct((n // (seq or n), w, seq or n), F32) for _, w in t_cols]),
        compiler_params=pltpu.CompilerParams(vmem_limit_bytes=VMEM_LIMIT),
        name="in_proj",
    )(x, g.reshape(1, d), shift.reshape(nt, nseg, d), scale.reshape(nt, nseg, d), w_bf, wt_bf)


def _outproj_kernel(o_ref, x_ref, gate_ref, w_ref, out_ref):
    r = jnp.dot(o_ref[...], w_ref[...], preferred_element_type=F32)
    out_ref[...] = x_ref[...] + _per_segment(r, gate_ref[0], lambda a, s: a * s)


def _out_proj(o, x, gate, w_bf, tm):
    n, d = x.shape
    kdim = o.shape[1]
    nt = n // tm
    nseg = tm // CHUNK
    return pl.pallas_call(
        _outproj_kernel,
        grid=(nt,),
        in_specs=[
            pl.BlockSpec((tm, kdim), lambda t: (t, 0)),
            pl.BlockSpec((tm, d), lambda t: (t, 0)),
            pl.BlockSpec((1, nseg, d), lambda t: (t, 0, 0)),
            pl.BlockSpec((kdim, d), lambda t: (0, 0)),
        ],
        out_specs=pl.BlockSpec((tm, d), lambda t: (t, 0)),
        out_shape=jax.ShapeDtypeStruct((n, d), F32),
        compiler_params=pltpu.CompilerParams(vmem_limit_bytes=VMEM_LIMIT),
        name="out_proj",
    )(o, x, gate.reshape(nt, nseg, d), w_bf)


def _diff_attn_kernel(it_ref, jt_ref, var_ref, fin_ref, lam_ref, q_ref, k_ref, v_ref, bias_ref, g_ref,
                      o_ref, qb_s, p_s, al_s, m_s, acc_s, *, tq, tk, rc, nkb, lam_init):
    t = pl.program_id(1)
    nchunk = 2 * tq // rc
    ncb = tk // LANES
    bvar = [var_ref[it_ref[t] * nkb + jt_ref[t] * ncb + cb] for cb in range(ncb)]

    @pl.when(jt_ref[t] == 0)
    def _():
        m_s[...] = jnp.full(m_s.shape, NEG, F32)
        acc_s[...] = jnp.zeros(acc_s.shape, F32)
        lane = lax.broadcasted_iota(I32, (tq, LANES), 1)
        for h in range(DIFF_HEADS):
            qh = q_ref[0, :, h * LANES:(h + 1) * LANES].astype(F32)
            qb_s[h, 0:tq, :] = jnp.where(lane < DIFF_HEAD_DIM, qh, 0.0).astype(BF16)
            qb_s[h, tq:2 * tq, :] = jnp.where(lane >= DIFF_HEAD_DIM, qh, 0.0).astype(BF16)

    def logits(h):
        return lax.dot_general(qb_s[h], k_ref[0, :, h * LANES:(h + 1) * LANES], _NT, preferred_element_type=F32)

    s_next = logits(0)
    for h in range(DIFF_HEADS):
        s_all = s_next
        if h + 1 < DIFF_HEADS:
            s_next = logits(h + 1)
        for c in range(nchunk):
            rows = slice(c * rc, (c + 1) * rc)
            brow = (c * rc) % tq
            hm = 2 * h + (c * rc) // tq
            s = jnp.concatenate(
                [s_all[rows, cb * LANES:(cb + 1) * LANES] + bias_ref[bvar[cb], hm, brow:brow + rc, :]
                 for cb in range(ncb)], axis=1)
            m_prev = m_s[h, rows]
            m_new = jnp.maximum(m_prev, jnp.max(s, axis=1, keepdims=True))
            p = jnp.exp2(s - jnp.concatenate([m_new] * ncb, axis=1))
            p_s[h, rows] = p.astype(BF16)
            al_s[h, rows] = jnp.exp2(m_prev - m_new)
            m_s[h, rows] = m_new
        vh = v_ref[0, :, h * LANES:(h + 1) * LANES]
        v1 = jnp.concatenate([vh, jnp.ones_like(vh)], axis=1)
        alpha = al_s[h]
        acc_s[h] = (jnp.concatenate([alpha, alpha], axis=1) * acc_s[h]
                    + jnp.dot(p_s[h], v1, preferred_element_type=F32))

    @pl.when(fin_ref[t] == 1)
    def _():
        lp = lam_ref[...]
        lam = (jnp.exp(jnp.sum(lp[0:1] * lp[1:2], axis=1, keepdims=True))
               - jnp.exp(jnp.sum(lp[2:3] * lp[3:4], axis=1, keepdims=True)) + lam_init)
        g = g_ref[...]
        for h in range(DIFF_HEADS):
            o0 = acc_s[h, 0:tq, 0:LANES] / acc_s[h, 0:tq, LANES:2 * LANES]
            o1 = acc_s[h, tq:2 * tq, 0:LANES] / acc_s[h, tq:2 * tq, LANES:2 * LANES]
            o = _rms(o0 - lam * o1, g) * (1.0 - lam_init)
            o_ref[0, :, h * LANES:(h + 1) * LANES] = o.astype(o_ref.dtype)


def _diff_attn(q, k, v, bias, pairs, var, lam_p, subln_g, lam_init, tq, tk):
    b, sq, dq = q.shape
    nvar = bias.shape[0]
    nkb = k.shape[1] // LANES
    it = np.asarray([p[0] for p in pairs], np.int32)
    jt = np.asarray([p[1] for p in pairs], np.int32)
    vt = np.asarray(var, np.int32).reshape(-1)
    fin = np.asarray([1 if (n + 1 == len(pairs) or pairs[n + 1][0] != p[0]) else 0
                      for n, p in enumerate(pairs)], np.int32)
    rc = min(tq, 64)
    return pl.pallas_call(
        functools.partial(_diff_attn_kernel, tq=tq, tk=tk, rc=rc, nkb=nkb, lam_init=lam_init),
        grid_spec=pltpu.PrefetchScalarGridSpec(
            num_scalar_prefetch=4,
            grid=(b, len(pairs)),
            in_specs=[
                pl.BlockSpec((4, DIFF_HEAD_DIM), lambda bb, t, *_: (0, 0)),
                pl.BlockSpec((1, tq, dq), lambda bb, t, it_, jt_, vt_, fin_: (bb, it_[t], 0)),
                pl.BlockSpec((1, tk, dq), lambda bb, t, it_, jt_, vt_, fin_: (bb, jt_[t], 0)),
                pl.BlockSpec((1, tk, dq), lambda bb, t, it_, jt_, vt_, fin_: (bb, jt_[t], 0)),
                pl.BlockSpec((nvar, BIAS_HEADS, tq, LANES), lambda bb, t, *_: (0, 0, 0, 0),
                             pipeline_mode=pl.Buffered(1)),
                pl.BlockSpec((1, 2 * DIFF_HEAD_DIM), lambda bb, t, *_: (0, 0)),
            ],
            out_specs=pl.BlockSpec((1, tq, dq), lambda bb, t, it_, jt_, vt_, fin_: (bb, it_[t], 0)),
            scratch_shapes=[
                pltpu.VMEM((DIFF_HEADS, 2 * tq, LANES), BF16),
                pltpu.VMEM((DIFF_HEADS, 2 * tq, tk), BF16),
                pltpu.VMEM((DIFF_HEADS, 2 * tq, LANES), F32),
                pltpu.VMEM((DIFF_HEADS, 2 * tq, LANES), F32),
                pltpu.VMEM((DIFF_HEADS, 2 * tq, 2 * LANES), F32),
            ],
        ),
        out_shape=jax.ShapeDtypeStruct((b, sq, dq), BF16),
        compiler_params=pltpu.CompilerParams(vmem_limit_bytes=VMEM_LIMIT),
        name="diff_attn",
    )(jnp.asarray(it), jnp.asarray(jt), jnp.asarray(vt), jnp.asarray(fin), lam_p, q, k, v, bias,
      subln_g.reshape(1, -1))


def _block_bias(rel_bias, nq, nkb, tq):
    bpq = tq // LANES
    lo, hi = -bpq, FAR_BLOCKS
    bias = _bias_tiles(rel_bias, [-LANES * d for d in range(lo, hi + 1)], [4 * tq] * (hi - lo + 1), 1 << 30, tq, LANES)
    var = [[min(max(bpq * i - jb, lo), hi) - lo for jb in range(nkb)] for i in range(nq)]
    return bias, var


def _diff_prompt(q, k, v, rel_bias, lam_p, subln_g, lam_init):
    s = q.shape[1]
    tq, tk = 256, 512
    nq = s // tq
    pairs = [(i, j) for i in range(nq) for j in range((i * tq) // tk + 1)]
    bias, var = _block_bias(rel_bias, nq, s // LANES, tq)
    return _diff_attn(q, k, v, bias, pairs, var, lam_p, subln_g, lam_init, tq, tk)


def _sort_key(x):
    bits = pltpu.bitcast(x, I32)
    return jnp.where(bits < 0, INT_MIN - bits, bits)


def _dsa_kernel(nvis_ref, var_ref, q_ref, qi_ref, wi_ref, ki_ref, k_ref, v_ref, bias_ref, o_ref,
                keys_s, qs_s, p_s, al_s, m_s, acc_s, *, tq, tk, nk, rc, q0, length, topk, nbits):
    i = pl.program_id(1)
    nvis = nvis_ref[i]
    npair = DSA_KV_HEADS // 2
    hpp = 2 * DSA_GROUP
    nchunk = hpp * tq // rc

    for hd in range(DSA_HEADS):
        qs_s[hd // hpp, (hd % hpp) * tq:(hd % hpp + 1) * tq, :] = q_ref[0, :, hd * LANES:(hd + 1) * LANES]
    qi_stack = jnp.concatenate([qi_ref[0, :, u * LANES:(u + 1) * LANES] for u in range(IDX_HEADS // 2)], axis=0)
    wcols = [wi_ref[0, :, h:h + 1] for h in range(IDX_HEADS)]

    r_io = lax.broadcasted_iota(I32, (tq, tk), 0)
    c_io = lax.broadcasted_iota(I32, (tq, tk), 1)
    q_chunk = (q0 + i * tq + r_io) >> 6
    score_scale = IDX_DIM ** -0.5 * IDX_HEADS ** -0.5

    def score_tile(j):
        off = pl.multiple_of(j * tk, tk)
        ki_lo = ki_ref[0, pl.ds(off, tk), 0:LANES]
        ki_hi = ki_ref[0, pl.ds(off, tk), LANES:2 * LANES]
        s_even = lax.dot_general(qi_stack, ki_lo, _NT, preferred_element_type=F32)
        s_odd = lax.dot_general(qi_stack, ki_hi, _NT, preferred_element_type=F32)
        score = jnp.zeros((tq, tk), F32)
        for u in range(IDX_HEADS // 2):
            score = score + jnp.maximum(s_even[u * tq:(u + 1) * tq], 0.0) * wcols[2 * u]
            score = score + jnp.maximum(s_odd[u * tq:(u + 1) * tq], 0.0) * wcols[2 * u + 1]
        key = _sort_key(score * score_scale)
        kpos = j * tk + c_io
        vis = jnp.where((kpos >> 6) <= q_chunk, 1, 0) * jnp.where(kpos < length, 1, 0)
        keys_s[j] = jnp.where(vis > 0, key, INT_MIN)

    def score_body(jj, carry):
        score_tile(2 * jj)
        score_tile(2 * jj + 1)
        return carry

    lax.fori_loop(0, nvis // 2, score_body, 0)

    @pl.when(nvis % 2 == 1)
    def _():
        score_tile(nvis - 1)

    def count_ge(cand):
        below = jnp.broadcast_to(cand - 1, (tq, LANES))

        def body(j, acc):
            kk = keys_s[j]
            for u in range(tk // LANES):
                acc = acc + jnp.where(kk[:, u * LANES:(u + 1) * LANES] > below, 1.0, 0.0)
            return acc

        acc = lax.fori_loop(0, nvis, body, jnp.zeros((tq, LANES), F32))
        return jnp.sum(acc, axis=1, keepdims=True)

    def bit_body(it, ans):
        cand = ans | jnp.left_shift(jnp.int32(1), 31 - it)
        return jnp.where(count_ge(cand ^ INT_MIN) >= topk, cand, ans)

    thr = lax.fori_loop(0, 32, bit_body, jnp.zeros((tq, 1), I32)) ^ INT_MIN
    thr_sel = jnp.maximum(thr, NEG_INF_KEY + 1)
    row_full = thr >= thr_sel

    cnt_ge = count_ge(thr_sel)
    cnt_gt = count_ge(thr_sel + 1)
    excess = jnp.max(jnp.where(row_full, cnt_ge - topk, 0.0))

    @pl.when(excess > 0.0)
    def _():
        quota = topk - cnt_gt

        def count_tied_before(pos):
            def body(j, acc):
                kpos = j * tk + c_io
                hit = jnp.where(keys_s[j] == thr, 1.0, 0.0) * jnp.where(kpos < pos, 1.0, 0.0)
                return acc + hit
            acc = lax.fori_loop(0, nvis, body, jnp.zeros((tq, tk), F32))
            return jnp.sum(acc, axis=1, keepdims=True)

        def pos_body(it, p):
            cand = p | jnp.left_shift(jnp.int32(1), nbits - 1 - it)
            return jnp.where(count_tied_before(cand) < quota, cand, p)

        last = lax.fori_loop(0, nbits, pos_body, jnp.zeros((tq, 1), I32))
        last = jnp.where(row_full, last, nk * tk)

        def demote(j, carry):
            kpos = j * tk + c_io
            kk = keys_s[j]
            drop = jnp.where(kk == thr, 1, 0) * jnp.where(kpos > last, 1, 0)
            keys_s[j] = jnp.where(drop > 0, thr - 1, kk)
            return carry

        lax.fori_loop(0, nvis, demote, 0)

    m_s[...] = jnp.full(m_s.shape, NEG, F32)
    acc_s[...] = jnp.zeros(acc_s.shape, F32)

    ncb = tk // LANES

    def attn_body(j, carry):
        off = pl.multiple_of(j * tk, tk)
        bvar = [var_ref[(i * nk + j) * ncb + cb] for cb in range(ncb)]
        maskb = jnp.where(keys_s[j] >= thr_sel, 0.0, NEG)
        s_pair = [lax.dot_general(qs_s[pr], k_ref[0, pl.ds(off, tk), pr * LANES:(pr + 1) * LANES], _NT,
                                  preferred_element_type=F32) for pr in range(npair)]
        for pr in range(npair):
            for c in range(nchunk):
                rows = slice(c * rc, (c + 1) * rc)
                hd = pr * hpp + (c * rc) // tq
                brow = (c * rc) % tq
                s = jnp.concatenate(
                    [s_pair[pr][rows, cb * LANES:(cb + 1) * LANES] + bias_ref[bvar[cb], hd, brow:brow + rc, :]
                     for cb in range(ncb)], axis=1) + maskb[brow:brow + rc]
                m_prev = m_s[pr, rows]
                m_new = jnp.maximum(m_prev, jnp.max(s, axis=1, keepdims=True))
                p = jnp.exp2(s - jnp.concatenate([m_new] * ncb, axis=1))
                p_s[pr, rows] = p.astype(BF16)
                al_s[pr, rows] = jnp.exp2(m_prev - m_new)
                m_s[pr, rows] = m_new
            vp = v_ref[0, pl.ds(off, tk), pr * LANES:(pr + 1) * LANES]
            v1 = jnp.concatenate([vp, jnp.ones_like(vp)], axis=1)
            alpha = al_s[pr]
            acc_s[pr] = (jnp.concatenate([alpha, alpha], axis=1) * acc_s[pr]
                         + jnp.dot(p_s[pr], v1, preferred_element_type=F32))
        return carry

    lax.fori_loop(0, nvis, attn_body, 0)

    for hd in range(DSA_HEADS):
        rows = slice((hd % hpp) * tq, (hd % hpp + 1) * tq)
        o = acc_s[hd // hpp, rows, 0:LANES] / acc_s[hd // hpp, rows, LANES:2 * LANES]
        o_ref[0, :, hd * LANES:(hd + 1) * LANES] = o.astype(o_ref.dtype)


def _dsa_attn(q, qi, wi, ki2, k, v, bias, nvis, var, q0, length, topk, tq, tk):
    b, sq, dq = q.shape
    lp = k.shape[1]
    nq = sq // tq
    nk = lp // tk
    nvar = bias.shape[0]
    npair = DSA_KV_HEADS // 2
    prow = 2 * DSA_GROUP * tq
    kernel = functools.partial(_dsa_kernel, tq=tq, tk=tk, nk=nk, rc=min(tq, 64), q0=q0, length=length,
                               topk=topk, nbits=max(1, (lp - 1).bit_length()))
    once = dict(pipeline_mode=pl.Buffered(1))
    full = lambda bb, ii, *_: (bb, 0, 0)
    tile = lambda bb, ii, *_: (bb, ii, 0)
    return pl.pallas_call(
        kernel,
        grid_spec=pltpu.PrefetchScalarGridSpec(
            num_scalar_prefetch=2,
            grid=(b, nq),
            in_specs=[
                pl.BlockSpec((1, tq, dq), tile),
                pl.BlockSpec((1, tq, qi.shape[2]), tile),
                pl.BlockSpec((1, tq, LANES), tile),
                pl.BlockSpec((1, lp, 2 * LANES), full, **once),
                pl.BlockSpec((1, lp, k.shape[2]), full, **once),
                pl.BlockSpec((1, lp, v.shape[2]), full, **once),
                pl.BlockSpec((nvar, BIAS_HEADS, tq, LANES), lambda bb, ii, *_: (0, 0, 0, 0), **once),
            ],
            out_specs=pl.BlockSpec((1, tq, dq), tile),
            scratch_shapes=[
                pltpu.VMEM((nk, tq, tk), I32),
                pltpu.VMEM((npair, prow, LANES), BF16),
                pltpu.VMEM((npair, prow, tk), BF16),
                pltpu.VMEM((npair, prow, LANES), F32),
                pltpu.VMEM((npair, prow, LANES), F32),
                pltpu.VMEM((npair, prow, 2 * LANES), F32),
            ],
        ),
        out_shape=jax.ShapeDtypeStruct((b, sq, dq), BF16),
        compiler_params=pltpu.CompilerParams(vmem_limit_bytes=VMEM_LIMIT),
        name="dsa_attn",
    )(jnp.asarray(nvis, I32), jnp.asarray(var, I32).reshape(-1), q, qi, wi, ki2, k, v, bias)


def _dsa_prompt(q, qi, wi, ki2, k, v, rel_bias):
    s = q.shape[1]
    tq, tk = 128, 512
    nq = s // tq
    nvis = [(i * tq) // tk + 1 for i in range(nq)]
    bias, var = _block_bias(rel_bias, nq, s // LANES, tq)
    return _dsa_attn(q, qi, wi, ki2, k, v, bias, nvis, var, 0, s, min(IDX_TOPK, s // 4), tq, tk)


def _route(lt_s, rb_ref, gt_s):
    rows = [lt_s[e:e + 1, :] for e in range(N_EXPERTS)]
    mx = functools.reduce(jnp.maximum, rows)
    ex = [jnp.exp(r - mx) for r in rows]
    z = functools.reduce(lambda a, b: a + b, ex)
    probs = [x / z for x in ex]
    sel = [probs[e] + rb_ref[e] for e in range(N_EXPERTS)]
    best_val, best_grp = None, None
    for gidx in range(N_GROUPS):
        a, b, c, d = sel[EXPERTS_PER_GROUP * gidx:EXPERTS_PER_GROUP * (gidx + 1)]
        hi1, lo1, hi2, lo2 = jnp.maximum(a, b), jnp.minimum(a, b), jnp.maximum(c, d), jnp.minimum(c, d)
        top2 = jnp.maximum(hi1, hi2) + jnp.maximum(jnp.minimum(hi1, hi2), jnp.maximum(lo1, lo2))
        if gidx == 0:
            best_val, best_grp = top2, jnp.zeros_like(top2, dtype=I32)
        else:
            better = top2 > best_val
            best_grp = jnp.where(better, gidx, best_grp)
            best_val = jnp.where(better, top2, best_val)
    masked = [jnp.where(best_grp == e // EXPERTS_PER_GROUP, sel[e], -jnp.inf) for e in range(N_EXPERTS)]

    def first_argmax(vals):
        bv, bi = vals[0], jnp.zeros_like(best_grp)
        for e in range(1, N_EXPERTS):
            better = vals[e] > bv
            bi = jnp.where(better, e, bi)
            bv = jnp.where(better, vals[e], bv)
        return bi

    i1 = first_argmax(masked)
    i2 = first_argmax([jnp.where(i1 == e, -jnp.inf, masked[e]) for e in range(N_EXPERTS)])
    w1 = functools.reduce(lambda a, b: a + b, [jnp.where(i1 == e, probs[e], 0.0) for e in range(N_EXPERTS)])
    w2 = functools.reduce(lambda a, b: a + b, [jnp.where(i2 == e, probs[e], 0.0) for e in range(N_EXPERTS)])
    wsum = w1 + w2
    w1, w2 = w1 / wsum, w2 / wsum
    for e in range(N_EXPERTS):
        gt_s[e:e + 1, :] = jnp.where(i1 == e, w1, 0.0) + jnp.where(i2 == e, w2, 0.0)


def _moe_kernel(rb_ref, x_ref, g_ref, shift_ref, scale_ref, gate_ref, wrh_ref, wrl_ref, wg_ref, wu_ref, wd_ref,
                gfin_ref, out_ref, hb_s, lt_s, gt_s, gates_s, acc_s, *, final_norm):
    e = pl.program_id(1)

    @pl.when(e == 0)
    def _():
        h = _norm_mod(x_ref[...], g_ref[...], shift_ref[0], scale_ref[0])
        hb = h.astype(BF16)
        hb_s[...] = hb
        hl = (h - hb.astype(F32)).astype(BF16)
        lt_s[...] = (lax.dot_general(wrh_ref[...], hb, _NT, preferred_element_type=F32)
                     + lax.dot_general(wrl_ref[...], hb, _NT, preferred_element_type=F32)
                     + lax.dot_general(wrh_ref[...], hl, _NT, preferred_element_type=F32))
        gt_s[...] = jnp.zeros(gt_s.shape, F32)
        _route(lt_s, rb_ref, gt_s)
        gates_s[...] = gt_s[...].T
        acc_s[...] = jnp.zeros(acc_s.shape, F32)

    hb = hb_s[...]
    up = jnp.dot(hb, wu_ref[0], preferred_element_type=F32)
    gt = jnp.dot(hb, wg_ref[0], preferred_element_type=F32)
    lane = lax.broadcasted_iota(I32, gates_s.shape, 1)
    gcol = jnp.sum(jnp.where(lane == e, gates_s[...], 0.0), axis=1, keepdims=True)
    a = (gt * jax.nn.sigmoid(gt) * up * gcol).astype(BF16)
    acc_s[...] += jnp.dot(a, wd_ref[0], preferred_element_type=F32)

    @pl.when(e == N_EXPERTS - 1)
    def _():
        y = x_ref[...] + _per_segment(acc_s[...], gate_ref[0], lambda v, s: v * s)
        if final_norm:
            y = _rms(y, gfin_ref[...])
        out_ref[...] = y


def _moe(x, g, shift, scale, gate, w_router, router_bias, wg_bf, wu_bf, wd_bf, g_final, final_norm, tm):
    n, d = x.shape
    nt = n // tm
    nseg = tm // CHUNK
    ff = wg_bf.shape[2]
    wrt = w_router.T
    wrh = wrt.astype(BF16)
    wrl = (wrt - wrh.astype(F32)).astype(BF16)
    seg_spec = pl.BlockSpec((1, nseg, d), lambda t, e: (t, 0, 0))
    row_spec = pl.BlockSpec((1, d), lambda t, e: (0, 0))
    return pl.pallas_call(
        functools.partial(_moe_kernel, final_norm=final_norm),
        grid=(nt, N_EXPERTS),
        in_specs=[
            pl.BlockSpec(memory_space=pltpu.SMEM),
            pl.BlockSpec((tm, d), lambda t, e: (t, 0)),
            row_spec, seg_spec, seg_spec, seg_spec,
            pl.BlockSpec((N_EXPERTS, d), lambda t, e: (0, 0)),
            pl.BlockSpec((N_EXPERTS, d), lambda t, e: (0, 0)),
            pl.BlockSpec((1, d, ff), lambda t, e: (e, 0, 0)),
            pl.BlockSpec((1, d, ff), lambda t, e: (e, 0, 0)),
            pl.BlockSpec((1, ff, d), lambda t, e: (e, 0, 0)),
            row_spec,
        ],
        out_specs=pl.BlockSpec((tm, d), lambda t, e: (t, 0)),
        out_shape=jax.ShapeDtypeStruct((n, d), F32),
        scratch_shapes=[
            pltpu.VMEM((tm, d), BF16),
            pltpu.VMEM((N_EXPERTS, tm), F32),
            pltpu.VMEM((LANES, tm), F32),
            pltpu.VMEM((tm, LANES), F32),
            pltpu.VMEM((tm, d), F32),
        ],
        compiler_params=pltpu.CompilerParams(vmem_limit_bytes=VMEM_LIMIT),
        name="moe",
    )(router_bias, x, g.reshape(1, d), shift.reshape(nt, nseg, d), scale.reshape(nt, nseg, d),
      gate.reshape(nt, nseg, d), wrh, wrl, wg_bf, wu_bf, wd_bf, g_final.reshape(1, d))


def _dsa_in_weights(w):
    d = w.shape[0]
    o1 = DSA_HEADS * DSA_HEAD_DIM
    o2 = o1 + DSA_KV_HEADS * DSA_HEAD_DIM
    o3 = o2 + DSA_KV_HEADS * DSA_HEAD_DIM
    o4 = o3 + IDX_HEADS * IDX_DIM
    o5 = o4 + IDX_DIM
    wq = w[:, :o1].reshape(d, DSA_HEADS, 1, DSA_HEAD_DIM)
    half = ((np.arange(DSA_HEADS) // DSA_GROUP) % 2).reshape(1, DSA_HEADS, 1, 1)
    wq = jnp.concatenate([jnp.where(half == 0, wq, 0.0), jnp.where(half == 1, wq, 0.0)], axis=2)
    wki = w[:, o4:o5]
    zk = jnp.zeros_like(wki)
    wwi = jnp.pad(w[:, o5:], ((0, 0), (0, LANES - IDX_HEADS)))
    return jnp.concatenate([wq.reshape(d, 2 * o1), w[:, o1:o4], wki, zk, zk, wki, wwi], axis=1).astype(BF16)


def _dsa_out_weights(w):
    d = w.shape[1]
    w4 = w.reshape(DSA_HEADS, 1, DSA_HEAD_DIM, d)
    half = ((np.arange(DSA_HEADS) // DSA_GROUP) % 2).reshape(DSA_HEADS, 1, 1, 1)
    w4 = jnp.concatenate([jnp.where(half == 0, w4, 0.0), jnp.where(half == 1, w4, 0.0)], axis=1)
    return w4.reshape(2 * DSA_HEADS * DSA_HEAD_DIM, d).astype(BF16)


def kernel(x_prompt, x_sample, cache_k_diff, cache_v_diff, cache_k_dsa, cache_v_dsa, cache_kidx_dsa,
           c_prompt, c_sample, rel_bias, w_in_diff, w_out_diff, lam_diff, subln_g_diff, w_in_dsa, w_out_dsa,
           w_ada, b_ada, g_norm_mix, g_norm_ffn, w_router, router_bias, w_gate, w_up, w_down, g_final):
    b, s, d = x_prompt.shape
    bs, ss, _ = x_sample.shape
    past = cache_k_diff.shape[2]
    depth = w_ada.shape[0]
    assert d == D_MODEL and ss == CHUNK and s % 512 == 0 and past % CHUNK == 0 and depth == 2
    l_s = past + ss
    lp_s = -(-l_s // LANES) * LANES
    nkb_s = lp_s // LANES

    streams = []
    for x3, rows in ((x_prompt, np.repeat(np.arange(b), s // CHUNK)), (x_sample, b + np.arange(bs))):
        n = x3.shape[0] * x3.shape[1]
        streams.append((x3.reshape(n, d), x3.shape[:2], rows,
                        _pick_tile(n, (512, 256, 128, 64)), _pick_tile(n, (1024, 512, 256, 128, 64))))
    mod = _ada_mod(jnp.concatenate([c_prompt, c_sample], axis=0), w_ada, b_ada)

    def mods(layer, rows):
        m = mod[layer][rows]
        return [m[:, i * d:(i + 1) * d] for i in range(6)]

    def pad_keys(x):
        return jnp.pad(x, ((0, 0), (0, lp_s - x.shape[1]), (0, 0)))

    bias_s = _bias_tiles(rel_bias, [jb * LANES - past for jb in range(nkb_s)], [past] * nkb_s, l_s, ss, LANES)
    var_s = [list(range(nkb_s))]
    xs = [st[0] for st in streams]
    outs_kv = []

    qk = DIFF_HEADS * 2 * DIFF_HEAD_DIM
    lam_init = 0.8 - 0.6 * float(np.exp(-0.3 * 0))
    q_scale = DIFF_HEAD_DIM ** -0.5 * LOG2E
    w_in0, w_out0 = w_in_diff[0].astype(BF16), w_out_diff[0].astype(BF16)
    wt_k0 = w_in_diff[0][:, qk:2 * qk].T.astype(BF16)
    moe_w0 = (w_gate[0].astype(BF16), w_up[0].astype(BF16), w_down[0].astype(BF16))
    for si, (_, (nb, ns), rows, tm, tm_moe) in enumerate(streams):
        m = mods(0, rows)
        if si == 0:
            cols = ((0, qk, ((0, 0, q_scale),)), (qk, qk, ((2, 0, 1.0),)), (2 * qk, qk, ((1, 0, 1.0), (3, 0, 1.0))))
            outs = ((qk, BF16), (qk, F32), (qk, BF16), (qk, BF16))
            q_bf, v_f, k_bf, v_bf, k_t = _in_proj(xs[si], g_norm_mix[0], m[0], m[1], w_in0, cols, outs, tm,
                                                  wt_k0, ((0, qk),), ns)
            k_f = jnp.transpose(k_t.reshape(nb, DIFF_HEADS, 2, DIFF_HEAD_DIM, ns), (0, 4, 1, 2, 3))
        else:
            cols = ((0, qk, ((0, 0, q_scale),)), (qk, qk, ((1, 0, 1.0), (3, 0, 1.0))),
                    (2 * qk, qk, ((2, 0, 1.0), (4, 0, 1.0))))
            outs = ((qk, BF16), (qk, F32), (qk, F32), (qk, BF16), (qk, BF16))
            q_bf, k_f, v_f, k_bf, v_bf = _in_proj(xs[si], g_norm_mix[0], m[0], m[1], w_in0, cols, outs, tm)
        outs_kv.append((k_f, v_f))
        q3, k3, v3 = (a.reshape(nb, ns, qk) for a in (q_bf, k_bf, v_bf))
        if si == 0:
            o = _diff_prompt(q3, k3, v3, rel_bias, lam_diff[0], subln_g_diff[0], lam_init)
        else:
            ka = pad_keys(jnp.concatenate([cache_k_diff[0].reshape(nb, past, qk).astype(BF16), k3], axis=1))
            va = pad_keys(jnp.concatenate([cache_v_diff[0].reshape(nb, past, qk).astype(BF16), v3], axis=1))
            o = _diff_attn(q3, ka, va, bias_s, [(0, 0)], var_s, lam_diff[0], subln_g_diff[0], lam_init, ns, lp_s)
        x = _out_proj(o.reshape(nb * ns, qk), xs[si], m[2], w_out0, tm)
        xs[si] = _moe(x, g_norm_ffn[0], m[3], m[4], m[5], w_router, router_bias, *moe_w0, g_final, False, tm_moe)

    nq_w = 2 * DSA_HEADS * DSA_HEAD_DIM
    kv_w = DSA_KV_HEADS * DSA_HEAD_DIM
    qi_w = IDX_HEADS * IDX_DIM
    c_k, c_v, c_qi = nq_w, nq_w + kv_w, nq_w + 2 * kv_w
    c_ki, c_wi = c_qi + qi_w, c_qi + qi_w + 2 * LANES
    half_q = nq_w // 2
    q_scale = DSA_HEAD_DIM ** -0.5 * LOG2E
    cols_q = ((0, half_q, ((0, 0, q_scale),)), (half_q, half_q, ((0, half_q, q_scale),)))
    w_in1, w_out1 = _dsa_in_weights(w_in_dsa[0]), _dsa_out_weights(w_out_dsa[0])
    o_k = DSA_HEADS * DSA_HEAD_DIM
    o_v, o_ki = o_k + kv_w, o_k + 2 * kv_w + qi_w
    wt_1 = jnp.concatenate([w_in_dsa[0][:, o_k:o_v], w_in_dsa[0][:, o_v:o_v + kv_w],
                            w_in_dsa[0][:, o_ki:o_ki + IDX_DIM]], axis=1).T.astype(BF16)
    moe_w1 = (w_gate[1].astype(BF16), w_up[1].astype(BF16), w_down[1].astype(BF16))
    for si, (_, (nb, ns), rows, tm, tm_moe) in enumerate(streams):
        m = mods(1, rows)
        if si == 0:
            cols = cols_q + ((c_k, kv_w, ((1, 0, 1.0),)), (c_v, kv_w, ((2, 0, 1.0),)), (c_qi, qi_w, ((3, 0, 1.0),)),
                             (c_ki, 2 * LANES, ((4, 0, 1.0),)), (c_wi, LANES, ((5, 0, 1.0),)))
            outs = ((nq_w, BF16), (kv_w, BF16), (kv_w, BF16), (qi_w, BF16), (2 * LANES, BF16), (LANES, F32))
            q2, k_bf, v_bf, qi_bf, ki2_bf, wi_f, k_t, v_t, ki_t = _in_proj(
                xs[si], g_norm_mix[1], m[0], m[1], w_in1, cols, outs, tm, wt_1,
                ((0, kv_w), (kv_w, kv_w), (2 * kv_w, IDX_DIM)), ns)
            k_f = jnp.transpose(k_t.reshape(nb, DSA_KV_HEADS, DSA_HEAD_DIM, ns), (0, 3, 1, 2))
            v_f = jnp.transpose(v_t.reshape(nb, DSA_KV_HEADS, DSA_HEAD_DIM, ns), (0, 3, 1, 2))
            ki_f = jnp.transpose(ki_t, (0, 2, 1))
        else:
            cols = cols_q + ((c_k, kv_w, ((1, 0, 1.0), (3, 0, 1.0))), (c_v, kv_w, ((2, 0, 1.0), (4, 0, 1.0))),
                             (c_qi, qi_w, ((5, 0, 1.0),)), (c_ki, 2 * LANES, ((7, 0, 1.0),)),
                             (c_ki, IDX_DIM, ((6, 0, 1.0),)), (c_wi, LANES, ((8, 0, 1.0),)))
            outs = ((nq_w, BF16), (kv_w, F32), (kv_w, F32), (kv_w, BF16), (kv_w, BF16), (qi_w, BF16),
                    (IDX_DIM, F32), (2 * LANES, BF16), (LANES, F32))
            q2, k_f, v_f, k_bf, v_bf, qi_bf, ki_f, ki2_bf, wi_f = _in_proj(
                xs[si], g_norm_mix[1], m[0], m[1], w_in1, cols, outs, tm)
        outs_kv.append((k_f, v_f, ki_f))
        q3, qi3, wi3 = q2.reshape(nb, ns, nq_w), qi_bf.reshape(nb, ns, qi_w), wi_f.reshape(nb, ns, LANES)
        k3, v3, ki3 = k_bf.reshape(nb, ns, kv_w), v_bf.reshape(nb, ns, kv_w), ki2_bf.reshape(nb, ns, 2 * LANES)
        if si == 0:
            o = _dsa_prompt(q3, qi3, wi3, ki3, k3, v3, rel_bias)
        else:
            cki = cache_kidx_dsa[0].astype(BF16)
            zki = jnp.zeros_like(cki)
            ka = pad_keys(jnp.concatenate([cache_k_dsa[0].reshape(nb, past, kv_w).astype(BF16), k3], axis=1))
            va = pad_keys(jnp.concatenate([cache_v_dsa[0].reshape(nb, past, kv_w).astype(BF16), v3], axis=1))
            kia = pad_keys(jnp.concatenate([jnp.concatenate([cki, zki, zki, cki], axis=2), ki3], axis=1))
            o = _dsa_attn(q3, qi3, wi3, kia, ka, va, bias_s, [nkb_s], var_s, past, l_s,
                          min(IDX_TOPK, l_s // 4), ns, LANES)
        x = _out_proj(o.reshape(nb * ns, nq_w), xs[si], m[2], w_out1, tm)
        xs[si] = _moe(x, g_norm_ffn[1], m[3], m[4], m[5], w_router, router_bias, *moe_w1, g_final, True, tm_moe)

    h2 = (DIFF_HEADS, 2, DIFF_HEAD_DIM)
    hv = (DIFF_HEADS, 2 * DIFF_HEAD_DIM)
    hk = (DSA_KV_HEADS, DSA_HEAD_DIM)
    (kd_p, vd_p), (kd_s, vd_s), (ks_p, vs_p, ki_p), (ks_s, vs_s, ki_s) = outs_kv
    return (xs[0].reshape(b, s, d), xs[1].reshape(bs, ss, d),
            kd_p.reshape(1, b, s, *h2), vd_p.reshape(1, b, s, *hv),
            ks_p.reshape(1, b, s, *hk), vs_p.reshape(1, b, s, *hk), ki_p.reshape(1, b, s, IDX_DIM),
            kd_s.reshape(1, bs, ss, *h2), vd_s.reshape(1, bs, ss, *hv),
            ks_s.reshape(1, bs, ss, *hk), vs_s.reshape(1, bs, ss, *hk), ki_s.reshape(1, bs, ss, IDX_DIM))
```

```python
import functools

import jax
import jax.numpy as jnp
import numpy as np
from jax import lax
from jax.experimental import pallas as pl
from jax.experimental.pallas import tpu as pltpu

F32 = jnp.float32
BF16 = jnp.bfloat16
I32 = jnp.int32

D_MODEL = 1024
CHUNK = 64
DIFF_HEADS = 8
DIFF_HEAD_DIM = 64
DSA_HEADS = 16
DSA_KV_HEADS = 4
DSA_GROUP = DSA_HEADS // DSA_KV_HEADS
DSA_HEAD_DIM = 64
IDX_HEADS = 8
IDX_DIM = 64
IDX_TOPK = 256
REL_BUCKETS = 32
BIAS_HEADS = 16
N_EXPERTS = 16
N_GROUPS = 4
EXPERTS_PER_GROUP = N_EXPERTS // N_GROUPS
EXPERT_FF = 512
RMS_EPS = 1e-6

LANES = 128
NEG = -1e30
LOG2E = 1.4426950408889634
INT_MIN = -(2 ** 31)
NEG_INF_KEY = -0x7F800000
VMEM_LIMIT = 56 * 1024 * 1024

_BUCKET_STEPS = (12, 16, 23, 32, 46, 64, 91)
SUM_ROWS = 16
FAR_BLOCKS = 2

_NT = (((1,), (1,)), ((), ()))


def _pick_tile(n, prefs):
    for t in prefs:
        if n % t == 0:
            return t
    raise ValueError(f"no tile in {prefs} divides {n}")


def _rms(x, g):
    ms = jnp.mean(x * x, axis=-1, keepdims=True)
    return x * lax.rsqrt(ms + RMS_EPS) * g


def _per_segment(x, seg_vals, op):
    tm, d = x.shape
    nseg = tm // CHUNK
    return op(x.reshape(nseg, CHUNK, d), seg_vals[:, None, :]).reshape(tm, d)


def _norm_mod(x, g, shift, scale):
    y = _rms(x, g)
    y = _per_segment(y, scale, lambda a, s: a * (1.0 + s))
    return _per_segment(y, shift, lambda a, s: a + s)


def _ada_kernel(c_ref, w_ref, b_ref, o_ref):
    c = c_ref[...]
    a = (c * jax.nn.sigmoid(c)).astype(BF16)
    o_ref[0] = jnp.dot(a, w_ref[0].astype(BF16), preferred_element_type=F32) + b_ref[0]


def _ada_mod(c_all, w_ada, b_ada):
    depth, d, n6 = w_ada.shape
    bc = c_all.shape[0]
    tn = _pick_tile(n6, (1536, 1024, 512, 128))
    return pl.pallas_call(
        _ada_kernel,
        grid=(depth, n6 // tn),
        in_specs=[
            pl.BlockSpec((bc, d), lambda l, j: (0, 0)),
            pl.BlockSpec((1, d, tn), lambda l, j: (l, 0, j)),
            pl.BlockSpec((1, 1, tn), lambda l, j: (l, 0, j)),
        ],
        out_specs=pl.BlockSpec((1, bc, tn), lambda l, j: (l, 0, j)),
        out_shape=jax.ShapeDtypeStruct((depth, bc, n6), F32),
        compiler_params=pltpu.CompilerParams(vmem_limit_bytes=VMEM_LIMIT),
        name="ada_mod",
    )(c_all, w_ada, b_ada.reshape(depth, 1, n6))


def _bias_kernel(off_ref, q0_ref, len_ref, rb_ref, o_ref, *, tq, tk, keys_major):
    v = pl.program_id(0)
    h = pl.program_id(1)
    shape = (tk, tq) if keys_major else (tq, tk)
    r = lax.broadcasted_iota(I32, shape, 1 if keys_major else 0)
    c = lax.broadcasted_iota(I32, shape, 0 if keys_major else 1)
    rel = off_ref[v] + c - r
    qpos = q0_ref[v] + r
    kpos = qpos + rel
    n = jnp.abs(rel)
    big = jnp.full(shape, 8, I32)
    for t in _BUCKET_STEPS:
        big = big + jnp.where(n >= t, 1, 0)
    bucket = jnp.where(n < 8, n, big) + jnp.where(rel > 0, REL_BUCKETS // 2, 0)
    val = jnp.zeros(shape, F32)
    for b in range(REL_BUCKETS):
        val = jnp.where(bucket == b, rb_ref[b, h], val)
    vis = jnp.where((kpos >> 6) <= (qpos >> 6), 1, 0) * jnp.where(kpos < len_ref[0], 1, 0)
    o_ref[0, 0] = jnp.where(vis > 0, val * LOG2E, NEG)


def _bias_tiles(rel_bias, offs, q0s, length, tq, tk, keys_major=False):
    nvar = len(offs)
    shape = (tk, tq) if keys_major else (tq, tk)
    return pl.pallas_call(
        functools.partial(_bias_kernel, tq=tq, tk=tk, keys_major=keys_major),
        grid_spec=pltpu.PrefetchScalarGridSpec(
            num_scalar_prefetch=3,
            grid=(nvar, BIAS_HEADS),
            in_specs=[pl.BlockSpec(memory_space=pltpu.SMEM)],
            out_specs=pl.BlockSpec((1, 1) + shape, lambda v, h, *_: (v, h, 0, 0)),
        ),
        out_shape=jax.ShapeDtypeStruct((nvar, BIAS_HEADS) + shape, F32),
        name="bias_tiles",
    )(jnp.asarray(offs, I32), jnp.asarray(q0s, I32), jnp.asarray([length], I32), rel_bias)


def _inproj_kernel(x_ref, g_ref, shift_ref, scale_ref, w_ref, wt_ref, *out_refs, cols, t_cols):
    h = _norm_mod(x_ref[...], g_ref[...], shift_ref[0], scale_ref[0]).astype(BF16)
    for start, width, dests in cols:
        p = jnp.dot(h, w_ref[:, start:start + width], preferred_element_type=F32)
        for out_idx, out_off, mult in dests:
            o_ref = out_refs[out_idx]
            val = p if mult == 1.0 else p * mult
            o_ref[:, out_off:out_off + width] = val.astype(o_ref.dtype)
    t_refs = out_refs[len(out_refs) - len(t_cols):]
    for (start, width, _), t_ref in zip(t_cols, t_refs):
        t_ref[0] = lax.dot_general(wt_ref[start:start + width, :], h, _NT,
                                   preferred_element_type=F32).astype(t_ref.dtype)


def _in_proj(x, g, shift, scale, w_bf, cols, out_defs, tm, wt_bf=None, t_cols=(), seq=None):
    n, d = x.shape
    nt = n // tm
    nseg = tm // CHUNK
    nout = w_bf.shape[1]
    if wt_bf is None:
        wt_bf = jnp.zeros((16, d), BF16)
    seg_spec = pl.BlockSpec((1, nseg, d), lambda t: (t, 0, 0))
    tpb = (seq or n) // tm
    return pl.pallas_call(
        functools.partial(_inproj_kernel, cols=cols, t_cols=t_cols),
        grid=(nt,),
        in_specs=[
            pl.BlockSpec((tm, d), lambda t: (t, 0)),
            pl.BlockSpec((1, d), lambda t: (0, 0)),
            seg_spec, seg_spec,
            pl.BlockSpec((d, nout), lambda t: (0, 0)),
            pl.BlockSpec(wt_bf.shape, lambda t: (0, 0)),
        ],
        out_specs=([pl.BlockSpec((tm, w), lambda t: (t, 0)) for w, _ in out_defs]
                   + [pl.BlockSpec((1, w, tm), lambda t: (t // tpb, 0, t % tpb)) for _, w, _ in t_cols]),
        out_shape=([jax.ShapeDtypeStruct((n, w), dt) for w, dt in out_defs]
                   + [jax.ShapeDtypeStruct((n // (seq or n), w, seq or n), dt) for _, w, dt in t_cols]),
        compiler_params=pltpu.CompilerParams(vmem_limit_bytes=VMEM_LIMIT),
        name="in_proj",
    )(x, g.reshape(1, d), shift.reshape(nt, nseg, d), scale.reshape(nt, nseg, d), w_bf, wt_bf)


def _outproj_kernel(o_ref, x_ref, gate_ref, w_ref, out_ref):
    r = jnp.dot(o_ref[...], w_ref[...], preferred_element_type=F32)
    out_ref[...] = x_ref[...] + _per_segment(r, gate_ref[0], lambda a, s: a * s)


def _out_proj(o, x, gate, w_bf, tm):
    n, d = x.shape
    kdim = o.shape[1]
    nt = n // tm
    nseg = tm // CHUNK
    return pl.pallas_call(
        _outproj_kernel,
        grid=(nt,),
        in_specs=[
            pl.BlockSpec((tm, kdim), lambda t: (t, 0)),
            pl.BlockSpec((tm, d), lambda t: (t, 0)),
            pl.BlockSpec((1, nseg, d), lambda t: (t, 0, 0)),
            pl.BlockSpec((kdim, d), lambda t: (0, 0)),
        ],
        out_specs=pl.BlockSpec((tm, d), lambda t: (t, 0)),
        out_shape=jax.ShapeDtypeStruct((n, d), F32),
        compiler_params=pltpu.CompilerParams(vmem_limit_bytes=VMEM_LIMIT),
        name="out_proj",
    )(o, x, gate.reshape(nt, nseg, d), w_bf)


def _diff_attn_kernel(it_ref, jt_ref, var_ref, fin_ref, lam_ref, q_ref, k_ref, v_ref, bias_ref, g_ref,
                      o_ref, qb_s, p_s, al_s, m_s, acc_s, *, tq, tk, rc, nkb, lam_init):
    t = pl.program_id(1)
    nchunk = 2 * tq // rc
    ncb = tk // LANES
    bvar = [var_ref[it_ref[t] * nkb + jt_ref[t] * ncb + cb] for cb in range(ncb)]

    @pl.when(jt_ref[t] == 0)
    def _():
        m_s[...] = jnp.full(m_s.shape, NEG, F32)
        acc_s[...] = jnp.zeros(acc_s.shape, F32)
        lane = lax.broadcasted_iota(I32, (tq, LANES), 1)
        for h in range(DIFF_HEADS):
            qh = q_ref[0, :, h * LANES:(h + 1) * LANES].astype(F32)
            qb_s[h, 0:tq, :] = jnp.where(lane < DIFF_HEAD_DIM, qh, 0.0).astype(BF16)
            qb_s[h, tq:2 * tq, :] = jnp.where(lane >= DIFF_HEAD_DIM, qh, 0.0).astype(BF16)

    def logits(h):
        return lax.dot_general(qb_s[h], k_ref[0, :, h * LANES:(h + 1) * LANES], _NT, preferred_element_type=F32)

    s_next = logits(0)
    for h in range(DIFF_HEADS):
        s_all = s_next
        if h + 1 < DIFF_HEADS:
            s_next = logits(h + 1)
        for c in range(nchunk):
            rows = slice(c * rc, (c + 1) * rc)
            brow = (c * rc) % tq
            hm = 2 * h + (c * rc) // tq
            s = jnp.concatenate(
                [s_all[rows, cb * LANES:(cb + 1) * LANES] + bias_ref[bvar[cb], hm, brow:brow + rc, :]
                 for cb in range(ncb)], axis=1)
            m_prev = m_s[h, rows]
            m_new = jnp.maximum(m_prev, jnp.max(s, axis=1, keepdims=True))
            p = jnp.exp2(s - jnp.concatenate([m_new] * ncb, axis=1))
            p_s[h, rows] = p.astype(BF16)
            al_s[h, rows] = jnp.exp2(m_prev - m_new)
            m_s[h, rows] = m_new
        vh = v_ref[0, :, h * LANES:(h + 1) * LANES]
        v1 = jnp.concatenate([vh, jnp.ones_like(vh)], axis=1)
        alpha = al_s[h]
        acc_s[h] = (jnp.concatenate([alpha, alpha], axis=1) * acc_s[h]
                    + jnp.dot(p_s[h], v1, preferred_element_type=F32))

    @pl.when(fin_ref[t] == 1)
    def _():
        lp = lam_ref[...]
        lam = (jnp.exp(jnp.sum(lp[0:1] * lp[1:2], axis=1, keepdims=True))
               - jnp.exp(jnp.sum(lp[2:3] * lp[3:4], axis=1, keepdims=True)) + lam_init)
        g = g_ref[...]
        for h in range(DIFF_HEADS):
            o0 = acc_s[h, 0:tq, 0:LANES] / acc_s[h, 0:tq, LANES:2 * LANES]
            o1 = acc_s[h, tq:2 * tq, 0:LANES] / acc_s[h, tq:2 * tq, LANES:2 * LANES]
            o = _rms(o0 - lam * o1, g) * (1.0 - lam_init)
            o_ref[0, :, h * LANES:(h + 1) * LANES] = o.astype(o_ref.dtype)


def _diff_attn(q, k, v, bias, pairs, var, lam_p, subln_g, lam_init, tq, tk):
    b, sq, dq = q.shape
    nvar = bias.shape[0]
    nkb = k.shape[1] // LANES
    it = np.asarray([p[0] for p in pairs], np.int32)
    jt = np.asarray([p[1] for p in pairs], np.int32)
    vt = np.asarray(var, np.int32).reshape(-1)
    fin = np.asarray([1 if (n + 1 == len(pairs) or pairs[n + 1][0] != p[0]) else 0
                      for n, p in enumerate(pairs)], np.int32)
    rc = min(tq, 64)
    return pl.pallas_call(
        functools.partial(_diff_attn_kernel, tq=tq, tk=tk, rc=rc, nkb=nkb, lam_init=lam_init),
        grid_spec=pltpu.PrefetchScalarGridSpec(
            num_scalar_prefetch=4,
            grid=(b, len(pairs)),
            in_specs=[
                pl.BlockSpec((4, DIFF_HEAD_DIM), lambda bb, t, *_: (0, 0)),
                pl.BlockSpec((1, tq, dq), lambda bb, t, it_, jt_, vt_, fin_: (bb, it_[t], 0)),
                pl.BlockSpec((1, tk, dq), lambda bb, t, it_, jt_, vt_, fin_: (bb, jt_[t], 0)),
                pl.BlockSpec((1, tk, dq), lambda bb, t, it_, jt_, vt_, fin_: (bb, jt_[t], 0)),
                pl.BlockSpec((nvar, BIAS_HEADS, tq, LANES), lambda bb, t, *_: (0, 0, 0, 0),
                             pipeline_mode=pl.Buffered(1)),
                pl.BlockSpec((1, 2 * DIFF_HEAD_DIM), lambda bb, t, *_: (0, 0)),
            ],
            out_specs=pl.BlockSpec((1, tq, dq), lambda bb, t, it_, jt_, vt_, fin_: (bb, it_[t], 0)),
            scratch_shapes=[
                pltpu.VMEM((DIFF_HEADS, 2 * tq, LANES), BF16),
                pltpu.VMEM((DIFF_HEADS, 2 * tq, tk), BF16),
                pltpu.VMEM((DIFF_HEADS, 2 * tq, LANES), F32),
                pltpu.VMEM((DIFF_HEADS, 2 * tq, LANES), F32),
                pltpu.VMEM((DIFF_HEADS, 2 * tq, 2 * LANES), F32),
            ],
        ),
        out_shape=jax.ShapeDtypeStruct((b, sq, dq), BF16),
        compiler_params=pltpu.CompilerParams(vmem_limit_bytes=VMEM_LIMIT),
        name="diff_attn",
    )(jnp.asarray(it), jnp.asarray(jt), jnp.asarray(vt), jnp.asarray(fin), lam_p, q, k, v, bias,
      subln_g.reshape(1, -1))


def _block_bias(rel_bias, nq, nkb, tq, keys_major=False):
    bpq = tq // LANES
    lo, hi = -bpq, FAR_BLOCKS
    bias = _bias_tiles(rel_bias, [-LANES * d for d in range(lo, hi + 1)], [4 * tq] * (hi - lo + 1), 1 << 30, tq, LANES,
                       keys_major)
    var = [[min(max(bpq * i - jb, lo), hi) - lo for jb in range(nkb)] for i in range(nq)]
    return bias, var


def _diff_kmajor_kernel(it_ref, jt_ref, var_ref, fin_ref, lam_ref, q_ref, k_ref, vt_ref, bias_ref, g_ref,
                        o_ref, qb_s, s_s, p_s, m_s, acc_s, *, tq, tk, nkb, lam_init):
    t = pl.program_id(1)
    ncb = tk // LANES
    bvar = [var_ref[it_ref[t] * nkb + jt_ref[t] * ncb + cb] for cb in range(ncb)]

    @pl.when(jt_ref[t] == 0)
    def _():
        m_s[...] = jnp.full(m_s.shape, NEG, F32)
        acc_s[...] = jnp.zeros(acc_s.shape, F32)
        lane = lax.broadcasted_iota(I32, (tq, LANES), 1)
        for h in range(DIFF_HEADS):
            qh = q_ref[0, :, h * LANES:(h + 1) * LANES].astype(F32)
            qb_s[h, 0:tq, :] = jnp.where(lane < DIFF_HEAD_DIM, qh, 0.0).astype(BF16)
            qb_s[h, tq:2 * tq, :] = jnp.where(lane >= DIFF_HEAD_DIM, qh, 0.0).astype(BF16)

    def logits(h):
        return lax.dot_general(k_ref[0, :, h * LANES:(h + 1) * LANES], qb_s[h], _NT, preferred_element_type=F32)

    s_next = logits(0)
    for h in range(DIFF_HEADS):
        s_all = s_next
        if h + 1 < DIFF_HEADS:
            s_next = logits(h + 1)
        slot = h % 2
        for cb in range(ncb):
            blk = slice(cb * LANES, (cb + 1) * LANES)
            s_s[slot, blk, 0:tq] = s_all[blk, 0:tq] + bias_ref[bvar[cb], 2 * h]
            s_s[slot, blk, tq:2 * tq] = s_all[blk, tq:2 * tq] + bias_ref[bvar[cb], 2 * h + 1]
        m_prev = m_s[h]
        m_new = jnp.maximum(m_prev, jnp.max(s_s[slot], axis=0, keepdims=True))
        alpha = jnp.exp2(m_prev - m_new)
        p_s[slot] = jnp.exp2(s_s[slot] - m_new).astype(BF16)
        vth = vt_ref[0, h * LANES:(h + 1) * LANES, :]
        v1 = jnp.concatenate([vth, jnp.ones((SUM_ROWS, tk), BF16)], axis=0)
        acc_s[h] = alpha * acc_s[h] + jnp.dot(v1, p_s[slot], preferred_element_type=F32)
        m_s[h] = m_new

    @pl.when(fin_ref[t] == 1)
    def _():
        lp = lam_ref[...]
        lam = (jnp.exp(jnp.sum(lp[0:1] * lp[1:2], axis=1, keepdims=True))
               - jnp.exp(jnp.sum(lp[2:3] * lp[3:4], axis=1, keepdims=True)) + lam_init)
        g = g_ref[...]
        for h in range(DIFF_HEADS):
            o = acc_s[h, 0:LANES] / acc_s[h, LANES:LANES + 1]
            o = o[:, 0:tq] - lam * o[:, tq:2 * tq]
            ms = jnp.mean(o * o, axis=0, keepdims=True)
            o = o * lax.rsqrt(ms + RMS_EPS) * g * (1.0 - lam_init)
            o_ref[0, :, h * LANES:(h + 1) * LANES] = o.T.astype(o_ref.dtype)


def _diff_prompt(q, k, vt, rel_bias, lam_p, subln_g, lam_init):
    b, s, dq = q.shape
    tq, tk = 256, 512
    nq, nkb = s // tq, s // LANES
    pairs = [(i, j) for i in range(nq) for j in range((i * tq) // tk + 1)]
    bias, var = _block_bias(rel_bias, nq, nkb, tq, keys_major=True)
    nvar = bias.shape[0]
    it = np.asarray([p[0] for p in pairs], np.int32)
    jt = np.asarray([p[1] for p in pairs], np.int32)
    fin = np.asarray([1 if (n + 1 == len(pairs) or pairs[n + 1][0] != p[0]) else 0
                      for n, p in enumerate(pairs)], np.int32)
    return pl.pallas_call(
        functools.partial(_diff_kmajor_kernel, tq=tq, tk=tk, nkb=nkb, lam_init=lam_init),
        grid_spec=pltpu.PrefetchScalarGridSpec(
            num_scalar_prefetch=4,
            grid=(b, len(pairs)),
            in_specs=[
                pl.BlockSpec((4, DIFF_HEAD_DIM), lambda bb, t, *_: (0, 0)),
                pl.BlockSpec((1, tq, dq), lambda bb, t, it_, jt_, vt_, fin_: (bb, it_[t], 0)),
                pl.BlockSpec((1, tk, dq), lambda bb, t, it_, jt_, vt_, fin_: (bb, jt_[t], 0)),
                pl.BlockSpec((1, dq, tk), lambda bb, t, it_, jt_, vt_, fin_: (bb, 0, jt_[t])),
                pl.BlockSpec((nvar, BIAS_HEADS, LANES, tq), lambda bb, t, *_: (0, 0, 0, 0),
                             pipeline_mode=pl.Buffered(1)),
                pl.BlockSpec((2 * DIFF_HEAD_DIM, 1), lambda bb, t, *_: (0, 0)),
            ],
            out_specs=pl.BlockSpec((1, tq, dq), lambda bb, t, it_, jt_, vt_, fin_: (bb, it_[t], 0)),
            scratch_shapes=[
                pltpu.VMEM((DIFF_HEADS, 2 * tq, LANES), BF16),
                pltpu.VMEM((2, tk, 2 * tq), F32),
                pltpu.VMEM((2, tk, 2 * tq), BF16),
                pltpu.VMEM((DIFF_HEADS, 1, 2 * tq), F32),
                pltpu.VMEM((DIFF_HEADS, LANES + SUM_ROWS, 2 * tq), F32),
            ],
        ),
        out_shape=jax.ShapeDtypeStruct((b, s, dq), BF16),
        compiler_params=pltpu.CompilerParams(vmem_limit_bytes=VMEM_LIMIT),
        name="diff_attn_kmajor",
    )(jnp.asarray(it), jnp.asarray(jt), jnp.asarray(np.asarray(var, np.int32).reshape(-1)), jnp.asarray(fin),
      lam_p, q, k, vt, bias, subln_g.reshape(-1, 1))


def _sort_key(x):
    bits = pltpu.bitcast(x, I32)
    return jnp.where(bits < 0, INT_MIN - bits, bits)


def _dsa_kernel(nvis_ref, var_ref, q_ref, qi_ref, wi_ref, ki_ref, k_ref, v_ref, bias_ref, o_ref,
                keys_s, qs_s, p_s, al_s, m_s, acc_s, *, tq, tk, nk, rc, q0, length, topk, nbits):
    i = pl.program_id(1)
    nvis = nvis_ref[i]
    npair = DSA_KV_HEADS // 2
    hpp = 2 * DSA_GROUP
    nchunk = hpp * tq // rc

    for hd in range(DSA_HEADS):
        qs_s[hd // hpp, (hd % hpp) * tq:(hd % hpp + 1) * tq, :] = q_ref[0, :, hd * LANES:(hd + 1) * LANES]
    qi_stack = jnp.concatenate([qi_ref[0, :, u * LANES:(u + 1) * LANES] for u in range(IDX_HEADS // 2)], axis=0)
    wi_t = wi_ref[0].T
    wrows = [wi_t[h:h + 1, :] for h in range(IDX_HEADS)]

    k_io = lax.broadcasted_iota(I32, (tk, tq), 0)
    q_chunk = (q0 + i * tq + lax.broadcasted_iota(I32, (tk, tq), 1)) >> 6
    score_scale = IDX_DIM ** -0.5 * IDX_HEADS ** -0.5

    def score_tile(j):
        off = pl.multiple_of(j * tk, tk)
        ki_lo = ki_ref[0, pl.ds(off, tk), 0:LANES]
        ki_hi = ki_ref[0, pl.ds(off, tk), LANES:2 * LANES]
        s_even = lax.dot_general(ki_lo, qi_stack, _NT, preferred_element_type=F32)
        s_odd = lax.dot_general(ki_hi, qi_stack, _NT, preferred_element_type=F32)
        score = jnp.zeros((tk, tq), F32)
        for u in range(IDX_HEADS // 2):
            score = score + jnp.maximum(s_even[:, u * tq:(u + 1) * tq], 0.0) * wrows[2 * u]
            score = score + jnp.maximum(s_odd[:, u * tq:(u + 1) * tq], 0.0) * wrows[2 * u + 1]
        key = _sort_key(score * score_scale)
        kpos = j * tk + k_io
        vis = jnp.where((kpos >> 6) <= q_chunk, 1, 0) * jnp.where(kpos < length, 1, 0)
        keys_s[j] = jnp.where(vis > 0, key, INT_MIN)

    def score_body(jj, carry):
        score_tile(2 * jj)
        score_tile(2 * jj + 1)
        return carry

    lax.fori_loop(0, nvis // 2, score_body, 0)

    @pl.when(nvis % 2 == 1)
    def _():
        score_tile(nvis - 1)

    nacc = 4

    def column_count(hit_fn):
        def body(j, accs):
            kk = keys_s[j]
            accs = list(accs)
            for r in range(tk // 8):
                accs[r % nacc] = accs[r % nacc] + jnp.where(hit_fn(j, r, kk[8 * r:8 * r + 8]), 1.0, 0.0)
            return tuple(accs)
        accs = lax.fori_loop(0, nvis, body, (jnp.zeros((8, tq), F32),) * nacc)
        return jnp.sum(functools.reduce(lambda a, b: a + b, accs), axis=0, keepdims=True)

    def count_ge(cand):
        below = jnp.broadcast_to(cand - 1, (8, tq))
        return column_count(lambda j, r, kk: kk > below)

    def bit_body(it, ans):
        cand = ans | jnp.left_shift(jnp.int32(1), 31 - it)
        return jnp.where(count_ge(cand ^ INT_MIN) >= topk, cand, ans)

    thr = lax.fori_loop(0, 32, bit_body, jnp.zeros((1, tq), I32)) ^ INT_MIN
    thr_sel = jnp.maximum(thr, NEG_INF_KEY + 1)
    row_full = thr >= thr_sel

    cnt_ge = count_ge(thr_sel)
    cnt_gt = count_ge(thr_sel + 1)
    excess = jnp.max(jnp.where(row_full, cnt_ge - topk, 0.0))

    @pl.when(excess > 0.0)
    def _():
        quota = topk - cnt_gt
        thr8 = jnp.broadcast_to(thr, (8, tq))
        k_io8 = k_io[0:8]

        def count_tied_before(pos):
            pos8 = jnp.broadcast_to(pos, (8, tq))
            return column_count(lambda j, r, kk: jnp.where(kk == thr8, 1, 0)
                                * jnp.where(j * tk + 8 * r + k_io8 < pos8, 1, 0) > 0)

        def pos_body(it, p):
            cand = p | jnp.left_shift(jnp.int32(1), nbits - 1 - it)
            return jnp.where(count_tied_before(cand) < quota, cand, p)

        last = lax.fori_loop(0, nbits, pos_body, jnp.zeros((1, tq), I32))
        last = jnp.where(row_full, last, nk * tk)

        def demote(j, carry):
            kpos = j * tk + k_io
            kk = keys_s[j]
            drop = jnp.where(kk == thr, 1, 0) * jnp.where(kpos > last, 1, 0)
            keys_s[j] = jnp.where(drop > 0, thr - 1, kk)
            return carry

        lax.fori_loop(0, nvis, demote, 0)

    m_s[...] = jnp.full(m_s.shape, NEG, F32)
    acc_s[...] = jnp.zeros(acc_s.shape, F32)

    ncb = tk // LANES

    def attn_body(j, carry):
        off = pl.multiple_of(j * tk, tk)
        bvar = [var_ref[(i * nk + j) * ncb + cb] for cb in range(ncb)]
        maskb = jnp.where(keys_s[j] >= thr_sel, 0.0, NEG).T
        s_pair = [lax.dot_general(qs_s[pr], k_ref[0, pl.ds(off, tk), pr * LANES:(pr + 1) * LANES], _NT,
                                  preferred_element_type=F32) for pr in range(npair)]
        for pr in range(npair):
            for c in range(nchunk):
                rows = slice(c * rc, (c + 1) * rc)
                hd = pr * hpp + (c * rc) // tq
                brow = (c * rc) % tq
                s = jnp.concatenate(
                    [s_pair[pr][rows, cb * LANES:(cb + 1) * LANES] + bias_ref[bvar[cb], hd, brow:brow + rc, :]
                     for cb in range(ncb)], axis=1) + maskb[brow:brow + rc]
                m_prev = m_s[pr, rows]
                m_new = jnp.maximum(m_prev, jnp.max(s, axis=1, keepdims=True))
                p = jnp.exp2(s - jnp.concatenate([m_new] * ncb, axis=1))
                p_s[pr, rows] = p.astype(BF16)
                al_s[pr, rows] = jnp.exp2(m_prev - m_new)
                m_s[pr, rows] = m_new
            vp = v_ref[0, pl.ds(off, tk), pr * LANES:(pr + 1) * LANES]
            v1 = jnp.concatenate([vp, jnp.ones_like(vp)], axis=1)
            alpha = al_s[pr]
            acc_s[pr] = (jnp.concatenate([alpha, alpha], axis=1) * acc_s[pr]
                         + jnp.dot(p_s[pr], v1, preferred_element_type=F32))
        return carry

    lax.fori_loop(0, nvis, attn_body, 0)

    for hd in range(DSA_HEADS):
        rows = slice((hd % hpp) * tq, (hd % hpp + 1) * tq)
        o = acc_s[hd // hpp, rows, 0:LANES] / acc_s[hd // hpp, rows, LANES:2 * LANES]
        o_ref[0, :, hd * LANES:(hd + 1) * LANES] = o.astype(o_ref.dtype)


def _dsa_attn(q, qi, wi, ki2, k, v, bias, nvis, var, q0, length, topk, tq, tk):
    b, sq, dq = q.shape
    lp = k.shape[1]
    nq = sq // tq
    nk = lp // tk
    nvar = bias.shape[0]
    npair = DSA_KV_HEADS // 2
    prow = 2 * DSA_GROUP * tq
    kernel = functools.partial(_dsa_kernel, tq=tq, tk=tk, nk=nk, rc=min(tq, 64), q0=q0, length=length,
                               topk=topk, nbits=max(1, (lp - 1).bit_length()))
    once = dict(pipeline_mode=pl.Buffered(1))
    full = lambda bb, ii, *_: (bb, 0, 0)
    tile = lambda bb, ii, *_: (bb, ii, 0)
    return pl.pallas_call(
        kernel,
        grid_spec=pltpu.PrefetchScalarGridSpec(
            num_scalar_prefetch=2,
            grid=(b, nq),
            in_specs=[
                pl.BlockSpec((1, tq, dq), tile),
                pl.BlockSpec((1, tq, qi.shape[2]), tile),
                pl.BlockSpec((1, tq, LANES), tile),
                pl.BlockSpec((1, lp, 2 * LANES), full, **once),
                pl.BlockSpec((1, lp, k.shape[2]), full, **once),
                pl.BlockSpec((1, lp, v.shape[2]), full, **once),
                pl.BlockSpec((nvar, BIAS_HEADS, tq, LANES), lambda bb, ii, *_: (0, 0, 0, 0), **once),
            ],
            out_specs=pl.BlockSpec((1, tq, dq), tile),
            scratch_shapes=[
                pltpu.VMEM((nk, tk, tq), I32),
                pltpu.VMEM((npair, prow, LANES), BF16),
                pltpu.VMEM((npair, prow, tk), BF16),
                pltpu.VMEM((npair, prow, LANES), F32),
                pltpu.VMEM((npair, prow, LANES), F32),
                pltpu.VMEM((npair, prow, 2 * LANES), F32),
            ],
        ),
        out_shape=jax.ShapeDtypeStruct((b, sq, dq), BF16),
        compiler_params=pltpu.CompilerParams(vmem_limit_bytes=VMEM_LIMIT),
        name="dsa_attn",
    )(jnp.asarray(nvis, I32), jnp.asarray(var, I32).reshape(-1), q, qi, wi, ki2, k, v, bias)


def _dsa_prompt(q, qi, wi, ki2, k, v, rel_bias):
    s = q.shape[1]
    tq, tk = 128, 512
    nq = s // tq
    nvis = [(i * tq) // tk + 1 for i in range(nq)]
    bias, var = _block_bias(rel_bias, nq, s // LANES, tq)
    return _dsa_attn(q, qi, wi, ki2, k, v, bias, nvis, var, 0, s, min(IDX_TOPK, s // 4), tq, tk)


def _route(lt_s, rb_ref, gt_s):
    rows = [lt_s[e:e + 1, :] for e in range(N_EXPERTS)]
    mx = functools.reduce(jnp.maximum, rows)
    ex = [jnp.exp(r - mx) for r in rows]
    z = functools.reduce(lambda a, b: a + b, ex)
    probs = [x / z for x in ex]
    sel = [probs[e] + rb_ref[e] for e in range(N_EXPERTS)]
    best_val, best_grp = None, None
    for gidx in range(N_GROUPS):
        a, b, c, d = sel[EXPERTS_PER_GROUP * gidx:EXPERTS_PER_GROUP * (gidx + 1)]
        hi1, lo1, hi2, lo2 = jnp.maximum(a, b), jnp.minimum(a, b), jnp.maximum(c, d), jnp.minimum(c, d)
        top2 = jnp.maximum(hi1, hi2) + jnp.maximum(jnp.minimum(hi1, hi2), jnp.maximum(lo1, lo2))
        if gidx == 0:
            best_val, best_grp = top2, jnp.zeros_like(top2, dtype=I32)
        else:
            better = top2 > best_val
            best_grp = jnp.where(better, gidx, best_grp)
            best_val = jnp.where(better, top2, best_val)
    masked = [jnp.where(best_grp == e // EXPERTS_PER_GROUP, sel[e], -jnp.inf) for e in range(N_EXPERTS)]

    def first_argmax(vals):
        bv, bi = vals[0], jnp.zeros_like(best_grp)
        for e in range(1, N_EXPERTS):
            better = vals[e] > bv
            bi = jnp.where(better, e, bi)
            bv = jnp.where(better, vals[e], bv)
        return bi

    i1 = first_argmax(masked)
    i2 = first_argmax([jnp.where(i1 == e, -jnp.inf, masked[e]) for e in range(N_EXPERTS)])
    w1 = functools.reduce(lambda a, b: a + b, [jnp.where(i1 == e, probs[e], 0.0) for e in range(N_EXPERTS)])
    w2 = functools.reduce(lambda a, b: a + b, [jnp.where(i2 == e, probs[e], 0.0) for e in range(N_EXPERTS)])
    wsum = w1 + w2
    w1, w2 = w1 / wsum, w2 / wsum
    for e in range(N_EXPERTS):
        gt_s[e:e + 1, :] = jnp.where(i1 == e, w1, 0.0) + jnp.where(i2 == e, w2, 0.0)


def _moe_kernel(rb_ref, x_ref, g_ref, shift_ref, scale_ref, gate_ref, wrh_ref, wrl_ref, wg_ref, wu_ref, wd_ref,
                gfin_ref, out_ref, hb_s, lt_s, gt_s, gates_s, acc_s, *, final_norm):
    e = pl.program_id(1)

    @pl.when(e == 0)
    def _():
        h = _norm_mod(x_ref[...], g_ref[...], shift_ref[0], scale_ref[0])
        hb = h.astype(BF16)
        hb_s[...] = hb
        hl = (h - hb.astype(F32)).astype(BF16)
        lt_s[...] = (lax.dot_general(wrh_ref[...], hb, _NT, preferred_element_type=F32)
                     + lax.dot_general(wrl_ref[...], hb, _NT, preferred_element_type=F32)
                     + lax.dot_general(wrh_ref[...], hl, _NT, preferred_element_type=F32))
        gt_s[...] = jnp.zeros(gt_s.shape, F32)
        _route(lt_s, rb_ref, gt_s)
        gates_s[...] = gt_s[...].T
        acc_s[...] = jnp.zeros(acc_s.shape, F32)

    hb = hb_s[...]
    up = jnp.dot(hb, wu_ref[0], preferred_element_type=F32)
    gt = jnp.dot(hb, wg_ref[0], preferred_element_type=F32)
    lane = lax.broadcasted_iota(I32, gates_s.shape, 1)
    gcol = jnp.sum(jnp.where(lane == e, gates_s[...], 0.0), axis=1, keepdims=True)
    a = (gt * jax.nn.sigmoid(gt) * up * gcol).astype(BF16)
    acc_s[...] += jnp.dot(a, wd_ref[0], preferred_element_type=F32)

    @pl.when(e == N_EXPERTS - 1)
    def _():
        y = x_ref[...] + _per_segment(acc_s[...], gate_ref[0], lambda v, s: v * s)
        if final_norm:
            y = _rms(y, gfin_ref[...])
        out_ref[...] = y


def _moe(x, g, shift, scale, gate, w_router, router_bias, wg_bf, wu_bf, wd_bf, g_final, final_norm, tm):
    n, d = x.shape
    nt = n // tm
    nseg = tm // CHUNK
    ff = wg_bf.shape[2]
    wrt = w_router.T
    wrh = wrt.astype(BF16)
    wrl = (wrt - wrh.astype(F32)).astype(BF16)
    seg_spec = pl.BlockSpec((1, nseg, d), lambda t, e: (t, 0, 0))
    row_spec = pl.BlockSpec((1, d), lambda t, e: (0, 0))
    return pl.pallas_call(
        functools.partial(_moe_kernel, final_norm=final_norm),
        grid=(nt, N_EXPERTS),
        in_specs=[
            pl.BlockSpec(memory_space=pltpu.SMEM),
            pl.BlockSpec((tm, d), lambda t, e: (t, 0)),
            row_spec, seg_spec, seg_spec, seg_spec,
            pl.BlockSpec((N_EXPERTS, d), lambda t, e: (0, 0)),
            pl.BlockSpec((N_EXPERTS, d), lambda t, e: (0, 0)),
            pl.BlockSpec((1, d, ff), lambda t, e: (e, 0, 0)),
            pl.BlockSpec((1, d, ff), lambda t, e: (e, 0, 0)),
            pl.BlockSpec((1, ff, d), lambda t, e: (e, 0, 0)),
            row_spec,
        ],
        out_specs=pl.BlockSpec((tm, d), lambda t, e: (t, 0)),
        out_shape=jax.ShapeDtypeStruct((n, d), F32),
        scratch_shapes=[
            pltpu.VMEM((tm, d), BF16),
            pltpu.VMEM((N_EXPERTS, tm), F32),
            pltpu.VMEM((LANES, tm), F32),
            pltpu.VMEM((tm, LANES), F32),
            pltpu.VMEM((tm, d), F32),
        ],
        compiler_params=pltpu.CompilerParams(vmem_limit_bytes=VMEM_LIMIT),
        name="moe",
    )(router_bias, x, g.reshape(1, d), shift.reshape(nt, nseg, d), scale.reshape(nt, nseg, d),
      gate.reshape(nt, nseg, d), wrh, wrl, wg_bf, wu_bf, wd_bf, g_final.reshape(1, d))


def _dsa_in_weights(w):
    d = w.shape[0]
    o1 = DSA_HEADS * DSA_HEAD_DIM
    o2 = o1 + DSA_KV_HEADS * DSA_HEAD_DIM
    o3 = o2 + DSA_KV_HEADS * DSA_HEAD_DIM
    o4 = o3 + IDX_HEADS * IDX_DIM
    o5 = o4 + IDX_DIM
    wq = w[:, :o1].reshape(d, DSA_HEADS, 1, DSA_HEAD_DIM)
    half = ((np.arange(DSA_HEADS) // DSA_GROUP) % 2).reshape(1, DSA_HEADS, 1, 1)
    wq = jnp.concatenate([jnp.where(half == 0, wq, 0.0), jnp.where(half == 1, wq, 0.0)], axis=2)
    wki = w[:, o4:o5]
    zk = jnp.zeros_like(wki)
    wwi = jnp.pad(w[:, o5:], ((0, 0), (0, LANES - IDX_HEADS)))
    return jnp.concatenate([wq.reshape(d, 2 * o1), w[:, o1:o4], wki, zk, zk, wki, wwi], axis=1).astype(BF16)


def _dsa_out_weights(w):
    d = w.shape[1]
    w4 = w.reshape(DSA_HEADS, 1, DSA_HEAD_DIM, d)
    half = ((np.arange(DSA_HEADS) // DSA_GROUP) % 2).reshape(DSA_HEADS, 1, 1, 1)
    w4 = jnp.concatenate([jnp.where(half == 0, w4, 0.0), jnp.where(half == 1, w4, 0.0)], axis=1)
    return w4.reshape(2 * DSA_HEADS * DSA_HEAD_DIM, d).astype(BF16)


def kernel(x_prompt, x_sample, cache_k_diff, cache_v_diff, cache_k_dsa, cache_v_dsa, cache_kidx_dsa,
           c_prompt, c_sample, rel_bias, w_in_diff, w_out_diff, lam_diff, subln_g_diff, w_in_dsa, w_out_dsa,
           w_ada, b_ada, g_norm_mix, g_norm_ffn, w_router, router_bias, w_gate, w_up, w_down, g_final):
    b, s, d = x_prompt.shape
    bs, ss, _ = x_sample.shape
    past = cache_k_diff.shape[2]
    depth = w_ada.shape[0]
    assert d == D_MODEL and ss == CHUNK and s % 512 == 0 and past % CHUNK == 0 and depth == 2
    l_s = past + ss
    lp_s = -(-l_s // LANES) * LANES
    nkb_s = lp_s // LANES

    streams = []
    for x3, rows in ((x_prompt, np.repeat(np.arange(b), s // CHUNK)), (x_sample, b + np.arange(bs))):
        n = x3.shape[0] * x3.shape[1]
        streams.append((x3.reshape(n, d), x3.shape[:2], rows,
                        _pick_tile(n, (512, 256, 128, 64)), _pick_tile(n, (1024, 512, 256, 128, 64))))
    mod = _ada_mod(jnp.concatenate([c_prompt, c_sample], axis=0), w_ada, b_ada)

    def mods(layer, rows):
        m = mod[layer][rows]
        return [m[:, i * d:(i + 1) * d] for i in range(6)]

    def pad_keys(x):
        return jnp.pad(x, ((0, 0), (0, lp_s - x.shape[1]), (0, 0)))

    bias_s = _bias_tiles(rel_bias, [jb * LANES - past for jb in range(nkb_s)], [past] * nkb_s, l_s, ss, LANES)
    var_s = [list(range(nkb_s))]
    xs = [st[0] for st in streams]
    outs_kv = []

    qk = DIFF_HEADS * 2 * DIFF_HEAD_DIM
    lam_init = 0.8 - 0.6 * float(np.exp(-0.3 * 0))
    q_scale = DIFF_HEAD_DIM ** -0.5 * LOG2E
    w_in0, w_out0 = w_in_diff[0].astype(BF16), w_out_diff[0].astype(BF16)
    wt_kv0 = w_in_diff[0][:, qk:3 * qk].T.astype(BF16)
    moe_w0 = (w_gate[0].astype(BF16), w_up[0].astype(BF16), w_down[0].astype(BF16))
    for si, (_, (nb, ns), rows, tm, tm_moe) in enumerate(streams):
        m = mods(0, rows)
        if si == 0:
            cols = ((0, qk, ((0, 0, q_scale),)), (qk, qk, ((2, 0, 1.0),)), (2 * qk, qk, ((1, 0, 1.0),)))
            outs = ((qk, BF16), (qk, F32), (qk, BF16))
            q_bf, v_f, k_bf, k_t, v_t = _in_proj(xs[si], g_norm_mix[0], m[0], m[1], w_in0, cols, outs, tm,
                                                 wt_kv0, ((0, qk, F32), (qk, qk, BF16)), ns)
            k_f = jnp.transpose(k_t.reshape(nb, DIFF_HEADS, 2, DIFF_HEAD_DIM, ns), (0, 4, 1, 2, 3))
            v_bf = None
        else:
            cols = ((0, qk, ((0, 0, q_scale),)), (qk, qk, ((1, 0, 1.0), (3, 0, 1.0))),
                    (2 * qk, qk, ((2, 0, 1.0), (4, 0, 1.0))))
            outs = ((qk, BF16), (qk, F32), (qk, F32), (qk, BF16), (qk, BF16))
            q_bf, k_f, v_f, k_bf, v_bf = _in_proj(xs[si], g_norm_mix[0], m[0], m[1], w_in0, cols, outs, tm)
        outs_kv.append((k_f, v_f))
        q3, k3 = q_bf.reshape(nb, ns, qk), k_bf.reshape(nb, ns, qk)
        if si == 0:
            o = _diff_prompt(q3, k3, v_t, rel_bias, lam_diff[0], subln_g_diff[0], lam_init)
        else:
            v3 = v_bf.reshape(nb, ns, qk)
            ka = pad_keys(jnp.concatenate([cache_k_diff[0].reshape(nb, past, qk).astype(BF16), k3], axis=1))
            va = pad_keys(jnp.concatenate([cache_v_diff[0].reshape(nb, past, qk).astype(BF16), v3], axis=1))
            o = _diff_attn(q3, ka, va, bias_s, [(0, 0)], var_s, lam_diff[0], subln_g_diff[0], lam_init, ns, lp_s)
        x = _out_proj(o.reshape(nb * ns, qk), xs[si], m[2], w_out0, tm)
        xs[si] = _moe(x, g_norm_ffn[0], m[3], m[4], m[5], w_router, router_bias, *moe_w0, g_final, False, tm_moe)

    nq_w = 2 * DSA_HEADS * DSA_HEAD_DIM
    kv_w = DSA_KV_HEADS * DSA_HEAD_DIM
    qi_w = IDX_HEADS * IDX_DIM
    c_k, c_v, c_qi = nq_w, nq_w + kv_w, nq_w + 2 * kv_w
    c_ki, c_wi = c_qi + qi_w, c_qi + qi_w + 2 * LANES
    half_q = nq_w // 2
    q_scale = DSA_HEAD_DIM ** -0.5 * LOG2E
    cols_q = ((0, half_q, ((0, 0, q_scale),)), (half_q, half_q, ((0, half_q, q_scale),)))
    w_in1, w_out1 = _dsa_in_weights(w_in_dsa[0]), _dsa_out_weights(w_out_dsa[0])
    o_k = DSA_HEADS * DSA_HEAD_DIM
    o_v, o_ki = o_k + kv_w, o_k + 2 * kv_w + qi_w
    wt_1 = jnp.concatenate([w_in_dsa[0][:, o_k:o_v], w_in_dsa[0][:, o_v:o_v + kv_w],
                            w_in_dsa[0][:, o_ki:o_ki + IDX_DIM]], axis=1).T.astype(BF16)
    moe_w1 = (w_gate[1].astype(BF16), w_up[1].astype(BF16), w_down[1].astype(BF16))
    for si, (_, (nb, ns), rows, tm, tm_moe) in enumerate(streams):
        m = mods(1, rows)
        if si == 0:
            cols = cols_q + ((c_k, kv_w, ((1, 0, 1.0),)), (c_v, kv_w, ((2, 0, 1.0),)), (c_qi, qi_w, ((3, 0, 1.0),)),
                             (c_ki, 2 * LANES, ((4, 0, 1.0),)), (c_wi, LANES, ((5, 0, 1.0),)))
            outs = ((nq_w, BF16), (kv_w, BF16), (kv_w, BF16), (qi_w, BF16), (2 * LANES, BF16), (LANES, F32))
            q2, k_bf, v_bf, qi_bf, ki2_bf, wi_f, k_t, v_t, ki_t = _in_proj(
                xs[si], g_norm_mix[1], m[0], m[1], w_in1, cols, outs, tm, wt_1,
                ((0, kv_w, F32), (kv_w, kv_w, F32), (2 * kv_w, IDX_DIM, F32)), ns)
            k_f = jnp.transpose(k_t.reshape(nb, DSA_KV_HEADS, DSA_HEAD_DIM, ns), (0, 3, 1, 2))
            v_f = jnp.transpose(v_t.reshape(nb, DSA_KV_HEADS, DSA_HEAD_DIM, ns), (0, 3, 1, 2))
            ki_f = jnp.transpose(ki_t, (0, 2, 1))
        else:
            cols = cols_q + ((c_k, kv_w, ((1, 0, 1.0), (3, 0, 1.0))), (c_v, kv_w, ((2, 0, 1.0), (4, 0, 1.0))),
                             (c_qi, qi_w, ((5, 0, 1.0),)), (c_ki, 2 * LANES, ((7, 0, 1.0),)),
                             (c_ki, IDX_DIM, ((6, 0, 1.0),)), (c_wi, LANES, ((8, 0, 1.0),)))
            outs = ((nq_w, BF16), (kv_w, F32), (kv_w, F32), (kv_w, BF16), (kv_w, BF16), (qi_w, BF16),
                    (IDX_DIM, F32), (2 * LANES, BF16), (LANES, F32))
            q2, k_f, v_f, k_bf, v_bf, qi_bf, ki_f, ki2_bf, wi_f = _in_proj(
                xs[si], g_norm_mix[1], m[0], m[1], w_in1, cols, outs, tm)
        outs_kv.append((k_f, v_f, ki_f))
        q3, qi3, wi3 = q2.reshape(nb, ns, nq_w), qi_bf.reshape(nb, ns, qi_w), wi_f.reshape(nb, ns, LANES)
        k3, v3, ki3 = k_bf.reshape(nb, ns, kv_w), v_bf.reshape(nb, ns, kv_w), ki2_bf.reshape(nb, ns, 2 * LANES)
        if si == 0:
            o = _dsa_prompt(q3, qi3, wi3, ki3, k3, v3, rel_bias)
        else:
            cki = cache_kidx_dsa[0].astype(BF16)
            zki = jnp.zeros_like(cki)
            ka = pad_keys(jnp.concatenate([cache_k_dsa[0].reshape(nb, past, kv_w).astype(BF16), k3], axis=1))
            va = pad_keys(jnp.concatenate([cache_v_dsa[0].reshape(nb, past, kv_w).astype(BF16), v3], axis=1))
            kia = pad_keys(jnp.concatenate([jnp.concatenate([cki, zki, zki, cki], axis=2), ki3], axis=1))
            pad_q = lambda a: jnp.pad(a, ((0, 0), (0, LANES - ns), (0, 0)))
            bias_s2 = _bias_tiles(rel_bias, [jb * LANES - past for jb in range(nkb_s)], [past] * nkb_s, l_s,
                                  LANES, LANES)
            o = _dsa_attn(pad_q(q3), pad_q(qi3), pad_q(wi3), kia, ka, va, bias_s2, [nkb_s], var_s, past, l_s,
                          min(IDX_TOPK, l_s // 4), LANES, LANES)[:, :ns]
        x = _out_proj(o.reshape(nb * ns, nq_w), xs[si], m[2], w_out1, tm)
        xs[si] = _moe(x, g_norm_ffn[1], m[3], m[4], m[5], w_router, router_bias, *moe_w1, g_final, True, tm_moe)

    h2 = (DIFF_HEADS, 2, DIFF_HEAD_DIM)
    hv = (DIFF_HEADS, 2 * DIFF_HEAD_DIM)
    hk = (DSA_KV_HEADS, DSA_HEAD_DIM)
    (kd_p, vd_p), (kd_s, vd_s), (ks_p, vs_p, ki_p), (ks_s, vs_s, ki_s) = outs_kv
    return (xs[0].reshape(b, s, d), xs[1].reshape(bs, ss, d),
            kd_p.reshape(1, b, s, *h2), vd_p.reshape(1, b, s, *hv),
            ks_p.reshape(1, b, s, *hk), vs_p.reshape(1, b, s, *hk), ki_p.reshape(1, b, s, IDX_DIM),
            kd_s.reshape(1, bs, ss, *h2), vd_s.reshape(1, bs, ss, *hv),
            ks_s.reshape(1, bs, ss, *hk), vs_s.reshape(1, bs, ss, *hk), ki_s.reshape(1, bs, ss, IDX_DIM))
```

```python
import functools

import jax
import jax.numpy as jnp
import numpy as np
from jax import lax
from jax.experimental import pallas as pl
from jax.experimental.pallas import tpu as pltpu

F32 = jnp.float32
BF16 = jnp.bfloat16
I32 = jnp.int32

D_MODEL = 1024
CHUNK = 64
DIFF_HEADS = 8
DIFF_HEAD_DIM = 64
DSA_HEADS = 16
DSA_KV_HEADS = 4
DSA_GROUP = DSA_HEADS // DSA_KV_HEADS
DSA_HEAD_DIM = 64
IDX_HEADS = 8
IDX_DIM = 64
IDX_TOPK = 256
REL_BUCKETS = 32
BIAS_HEADS = 16
N_EXPERTS = 16
N_GROUPS = 4
EXPERTS_PER_GROUP = N_EXPERTS // N_GROUPS
EXPERT_FF = 512
RMS_EPS = 1e-6

LANES = 128
NEG = -1e30
LOG2E = 1.4426950408889634
INT_MIN = -(2 ** 31)
NEG_INF_KEY = -0x7F800000
VMEM_LIMIT = 56 * 1024 * 1024

_BUCKET_STEPS = (12, 16, 23, 32, 46, 64, 91)
FAR_BLOCKS = 2

_NT = (((1,), (1,)), ((), ()))


def _pick_tile(n, prefs):
    for t in prefs:
        if n % t == 0:
            return t
    raise ValueError(f"no tile in {prefs} divides {n}")


def _rms(x, g):
    ms = jnp.mean(x * x, axis=-1, keepdims=True)
    return x * lax.rsqrt(ms + RMS_EPS) * g


def _per_segment(x, seg_vals, op):
    tm, d = x.shape
    nseg = tm // CHUNK
    return op(x.reshape(nseg, CHUNK, d), seg_vals[:, None, :]).reshape(tm, d)


def _norm_mod(x, g, shift, scale):
    y = _rms(x, g)
    y = _per_segment(y, scale, lambda a, s: a * (1.0 + s))
    return _per_segment(y, shift, lambda a, s: a + s)


def _ada_kernel(c_ref, w_ref, b_ref, o_ref):
    c = c_ref[...]
    a = (c * jax.nn.sigmoid(c)).astype(BF16)
    o_ref[0] = jnp.dot(a, w_ref[0].astype(BF16), preferred_element_type=F32) + b_ref[0]


def _ada_mod(c_all, w_ada, b_ada):
    depth, d, n6 = w_ada.shape
    bc = c_all.shape[0]
    tn = _pick_tile(n6, (1536, 1024, 512, 128))
    return pl.pallas_call(
        _ada_kernel,
        grid=(depth, n6 // tn),
        in_specs=[
            pl.BlockSpec((bc, d), lambda l, j: (0, 0)),
            pl.BlockSpec((1, d, tn), lambda l, j: (l, 0, j)),
            pl.BlockSpec((1, 1, tn), lambda l, j: (l, 0, j)),
        ],
        out_specs=pl.BlockSpec((1, bc, tn), lambda l, j: (l, 0, j)),
        out_shape=jax.ShapeDtypeStruct((depth, bc, n6), F32),
        compiler_params=pltpu.CompilerParams(vmem_limit_bytes=VMEM_LIMIT),
        name="ada_mod",
    )(c_all, w_ada, b_ada.reshape(depth, 1, n6))


def _bias_kernel(off_ref, q0_ref, len_ref, rb_ref, o_ref, *, tq, tk):
    v = pl.program_id(0)
    h = pl.program_id(1)
    shape = (tq, tk)
    r = lax.broadcasted_iota(I32, shape, 0)
    c = lax.broadcasted_iota(I32, shape, 1)
    rel = off_ref[v] + c - r
    qpos = q0_ref[v] + r
    kpos = qpos + rel
    n = jnp.abs(rel)
    big = jnp.full(shape, 8, I32)
    for t in _BUCKET_STEPS:
        big = big + jnp.where(n >= t, 1, 0)
    bucket = jnp.where(n < 8, n, big) + jnp.where(rel > 0, REL_BUCKETS // 2, 0)
    val = jnp.zeros(shape, F32)
    for b in range(REL_BUCKETS):
        val = jnp.where(bucket == b, rb_ref[b, h], val)
    vis = jnp.where((kpos >> 6) <= (qpos >> 6), 1, 0) * jnp.where(kpos < len_ref[0], 1, 0)
    o_ref[0, 0] = jnp.where(vis > 0, val * LOG2E, NEG)


def _bias_tiles(rel_bias, offs, q0s, length, tq, tk):
    nvar = len(offs)
    shape = (tq, tk)
    return pl.pallas_call(
        functools.partial(_bias_kernel, tq=tq, tk=tk),
        grid_spec=pltpu.PrefetchScalarGridSpec(
            num_scalar_prefetch=3,
            grid=(nvar, BIAS_HEADS),
            in_specs=[pl.BlockSpec(memory_space=pltpu.SMEM)],
            out_specs=pl.BlockSpec((1, 1) + shape, lambda v, h, *_: (v, h, 0, 0)),
        ),
        out_shape=jax.ShapeDtypeStruct((nvar, BIAS_HEADS) + shape, F32),
        name="bias_tiles",
    )(jnp.asarray(offs, I32), jnp.asarray(q0s, I32), jnp.asarray([length], I32), rel_bias)


def _inproj_kernel(x_ref, g_ref, shift_ref, scale_ref, w_ref, wt_ref, *out_refs, cols, t_cols):
    h = _norm_mod(x_ref[...], g_ref[...], shift_ref[0], scale_ref[0]).astype(BF16)
    for start, width, dests in cols:
        p = jnp.dot(h, w_ref[:, start:start + width], preferred_element_type=F32)
        for out_idx, out_off, mult in dests:
            o_ref = out_refs[out_idx]
            val = p if mult == 1.0 else p * mult
            o_ref[:, out_off:out_off + width] = val.astype(o_ref.dtype)
    t_refs = out_refs[len(out_refs) - len(t_cols):]
    for (start, width, _), t_ref in zip(t_cols, t_refs):
        t_ref[0] = lax.dot_general(wt_ref[start:start + width, :], h, _NT,
                                   preferred_element_type=F32).astype(t_ref.dtype)


def _in_proj(x, g, shift, scale, w_bf, cols, out_defs, tm, wt_bf=None, t_cols=(), seq=None):
    n, d = x.shape
    nt = n // tm
    nseg = tm // CHUNK
    nout = w_bf.shape[1]
    if wt_bf is None:
        wt_bf = jnp.zeros((16, d), BF16)
    seg_spec = pl.BlockSpec((1, nseg, d), lambda t: (t, 0, 0))
    tpb = (seq or n) // tm
    return pl.pallas_call(
        functools.partial(_inproj_kernel, cols=cols, t_cols=t_cols),
        grid=(nt,),
        in_specs=[
            pl.BlockSpec((tm, d), lambda t: (t, 0)),
            pl.BlockSpec((1, d), lambda t: (0, 0)),
            seg_spec, seg_spec,
            pl.BlockSpec((d, nout), lambda t: (0, 0)),
            pl.BlockSpec(wt_bf.shape, lambda t: (0, 0)),
        ],
        out_specs=([pl.BlockSpec((tm, w), lambda t: (t, 0)) for w, _ in out_defs]
                   + [pl.BlockSpec((1, w, tm), lambda t: (t // tpb, 0, t % tpb)) for _, w, _ in t_cols]),
        out_shape=([jax.ShapeDtypeStruct((n, w), dt) for w, dt in out_defs]
                   + [jax.ShapeDtypeStruct((n // (seq or n), w, seq or n), dt) for _, w, dt in t_cols]),
        compiler_params=pltpu.CompilerParams(vmem_limit_bytes=VMEM_LIMIT),
        name="in_proj",
    )(x, g.reshape(1, d), shift.reshape(nt, nseg, d), scale.reshape(nt, nseg, d), w_bf, wt_bf)


def _outproj_kernel(o_ref, x_ref, gate_ref, w_ref, out_ref):
    r = jnp.dot(o_ref[...], w_ref[...], preferred_element_type=F32)
    out_ref[...] = x_ref[...] + _per_segment(r, gate_ref[0], lambda a, s: a * s)


def _out_proj(o, x, gate, w_bf, tm):
    n, d = x.shape
    kdim = o.shape[1]
    nt = n // tm
    nseg = tm // CHUNK
    return pl.pallas_call(
        _outproj_kernel,
        grid=(nt,),
        in_specs=[
            pl.BlockSpec((tm, kdim), lambda t: (t, 0)),
            pl.BlockSpec((tm, d), lambda t: (t, 0)),
            pl.BlockSpec((1, nseg, d), lambda t: (t, 0, 0)),
            pl.BlockSpec((kdim, d), lambda t: (0, 0)),
        ],
        out_specs=pl.BlockSpec((tm, d), lambda t: (t, 0)),
        out_shape=jax.ShapeDtypeStruct((n, d), F32),
        compiler_params=pltpu.CompilerParams(vmem_limit_bytes=VMEM_LIMIT),
        name="out_proj",
    )(o, x, gate.reshape(nt, nseg, d), w_bf)


def _diff_attn_kernel(it_ref, jt_ref, var_ref, fin_ref, lam_ref, q_ref, k_ref, v_ref, bias_ref, g_ref,
                      o_ref, qb_s, p_s, al_s, m_s, acc_s, *, tq, tk, rc, nkb, lam_init):
    t = pl.program_id(1)
    nchunk = 2 * tq // rc
    ncb = tk // LANES
    bvar = [var_ref[it_ref[t] * nkb + jt_ref[t] * ncb + cb] for cb in range(ncb)]

    @pl.when(jt_ref[t] == 0)
    def _():
        m_s[...] = jnp.full(m_s.shape, NEG, F32)
        acc_s[...] = jnp.zeros(acc_s.shape, F32)
        lane = lax.broadcasted_iota(I32, (tq, LANES), 1)
        for h in range(DIFF_HEADS):
            qh = q_ref[0, :, h * LANES:(h + 1) * LANES].astype(F32)
            qb_s[h, 0:tq, :] = jnp.where(lane < DIFF_HEAD_DIM, qh, 0.0).astype(BF16)
            qb_s[h, tq:2 * tq, :] = jnp.where(lane >= DIFF_HEAD_DIM, qh, 0.0).astype(BF16)

    def logits(h):
        return lax.dot_general(qb_s[h], k_ref[0, :, h * LANES:(h + 1) * LANES], _NT, preferred_element_type=F32)

    s_next = logits(0)
    for h in range(DIFF_HEADS):
        s_all = s_next
        if h + 1 < DIFF_HEADS:
            s_next = logits(h + 1)
        for c in range(nchunk):
            rows = slice(c * rc, (c + 1) * rc)
            brow = (c * rc) % tq
            hm = 2 * h + (c * rc) // tq
            s = jnp.concatenate(
                [s_all[rows, cb * LANES:(cb + 1) * LANES] + bias_ref[bvar[cb], hm, brow:brow + rc, :]
                 for cb in range(ncb)], axis=1)
            m_prev = m_s[h, rows]
            m_new = jnp.maximum(m_prev, jnp.max(s, axis=1, keepdims=True))
            p = jnp.exp2(s - jnp.concatenate([m_new] * ncb, axis=1))
            p_s[h, rows] = p.astype(BF16)
            al_s[h, rows] = jnp.exp2(m_prev - m_new)
            m_s[h, rows] = m_new
        vh = v_ref[0, :, h * LANES:(h + 1) * LANES]
        v1 = jnp.concatenate([vh, jnp.ones_like(vh)], axis=1)
        alpha = al_s[h]
        acc_s[h] = (jnp.concatenate([alpha, alpha], axis=1) * acc_s[h]
                    + jnp.dot(p_s[h], v1, preferred_element_type=F32))

    @pl.when(fin_ref[t] == 1)
    def _():
        lp = lam_ref[...]
        lam = (jnp.exp(jnp.sum(lp[0:1] * lp[1:2], axis=1, keepdims=True))
               - jnp.exp(jnp.sum(lp[2:3] * lp[3:4], axis=1, keepdims=True)) + lam_init)
        g = g_ref[...]
        for h in range(DIFF_HEADS):
            o0 = acc_s[h, 0:tq, 0:LANES] / acc_s[h, 0:tq, LANES:2 * LANES]
            o1 = acc_s[h, tq:2 * tq, 0:LANES] / acc_s[h, tq:2 * tq, LANES:2 * LANES]
            o = _rms(o0 - lam * o1, g) * (1.0 - lam_init)
            o_ref[0, :, h * LANES:(h + 1) * LANES] = o.astype(o_ref.dtype)


def _diff_attn(q, k, v, bias, pairs, var, lam_p, subln_g, lam_init, tq, tk):
    b, sq, dq = q.shape
    nvar = bias.shape[0]
    nkb = k.shape[1] // LANES
    it = np.asarray([p[0] for p in pairs], np.int32)
    jt = np.asarray([p[1] for p in pairs], np.int32)
    vt = np.asarray(var, np.int32).reshape(-1)
    fin = np.asarray([1 if (n + 1 == len(pairs) or pairs[n + 1][0] != p[0]) else 0
                      for n, p in enumerate(pairs)], np.int32)
    rc = min(tq, 64)
    return pl.pallas_call(
        functools.partial(_diff_attn_kernel, tq=tq, tk=tk, rc=rc, nkb=nkb, lam_init=lam_init),
        grid_spec=pltpu.PrefetchScalarGridSpec(
            num_scalar_prefetch=4,
            grid=(b, len(pairs)),
            in_specs=[
                pl.BlockSpec((4, DIFF_HEAD_DIM), lambda bb, t, *_: (0, 0)),
                pl.BlockSpec((1, tq, dq), lambda bb, t, it_, jt_, vt_, fin_: (bb, it_[t], 0)),
                pl.BlockSpec((1, tk, dq), lambda bb, t, it_, jt_, vt_, fin_: (bb, jt_[t], 0)),
                pl.BlockSpec((1, tk, dq), lambda bb, t, it_, jt_, vt_, fin_: (bb, jt_[t], 0)),
                pl.BlockSpec((nvar, BIAS_HEADS, tq, LANES), lambda bb, t, *_: (0, 0, 0, 0),
                             pipeline_mode=pl.Buffered(1)),
                pl.BlockSpec((1, 2 * DIFF_HEAD_DIM), lambda bb, t, *_: (0, 0)),
            ],
            out_specs=pl.BlockSpec((1, tq, dq), lambda bb, t, it_, jt_, vt_, fin_: (bb, it_[t], 0)),
            scratch_shapes=[
                pltpu.VMEM((DIFF_HEADS, 2 * tq, LANES), BF16),
                pltpu.VMEM((DIFF_HEADS, 2 * tq, tk), BF16),
                pltpu.VMEM((DIFF_HEADS, 2 * tq, LANES), F32),
                pltpu.VMEM((DIFF_HEADS, 2 * tq, LANES), F32),
                pltpu.VMEM((DIFF_HEADS, 2 * tq, 2 * LANES), F32),
            ],
        ),
        out_shape=jax.ShapeDtypeStruct((b, sq, dq), BF16),
        compiler_params=pltpu.CompilerParams(vmem_limit_bytes=VMEM_LIMIT),
        name="diff_attn",
    )(jnp.asarray(it), jnp.asarray(jt), jnp.asarray(vt), jnp.asarray(fin), lam_p, q, k, v, bias,
      subln_g.reshape(1, -1))


def _block_bias(rel_bias, nq, nkb, tq):
    bpq = tq // LANES
    lo, hi = -bpq, FAR_BLOCKS
    bias = _bias_tiles(rel_bias, [-LANES * d for d in range(lo, hi + 1)], [4 * tq] * (hi - lo + 1), 1 << 30, tq, LANES)
    var = [[min(max(bpq * i - jb, lo), hi) - lo for jb in range(nkb)] for i in range(nq)]
    return bias, var


def _diff_prompt(q, k, v, rel_bias, lam_p, subln_g, lam_init):
    s = q.shape[1]
    tq, tk = 256, 512
    nq = s // tq
    pairs = [(i, j) for i in range(nq) for j in range((i * tq) // tk + 1)]
    bias, var = _block_bias(rel_bias, nq, s // LANES, tq)
    return _diff_attn(q, k, v, bias, pairs, var, lam_p, subln_g, lam_init, tq, tk)


def _sort_key(x):
    bits = pltpu.bitcast(x, I32)
    return jnp.where(bits < 0, INT_MIN - bits, bits)


def _dsa_kernel(nvis_ref, var_ref, q_ref, qi_ref, wi_ref, ki_ref, k_ref, v_ref, bias_ref, o_ref,
                keys_s, qs_s, p_s, al_s, m_s, acc_s, *, tq, tqa, tk, nk, rc, q0, length, topk, nbits):
    i = pl.program_id(1)
    nvis = nvis_ref[i]
    npair = DSA_KV_HEADS // 2
    hpp = 2 * DSA_GROUP
    nchunk = hpp * tqa // rc

    for hd in range(DSA_HEADS):
        qs_s[hd // hpp, (hd % hpp) * tqa:(hd % hpp + 1) * tqa, :] = q_ref[0, :, hd * LANES:(hd + 1) * LANES]
    qi_stack = jnp.concatenate([qi_ref[0, :, u * LANES:(u + 1) * LANES] for u in range(IDX_HEADS // 2)], axis=0)
    wi_t = wi_ref[0].T
    wrows = [wi_t[h:h + 1, :] for h in range(IDX_HEADS)]

    k_io = lax.broadcasted_iota(I32, (tk, tq), 0)
    q_chunk = (q0 + i * tq + lax.broadcasted_iota(I32, (tk, tq), 1)) >> 6
    score_scale = IDX_DIM ** -0.5 * IDX_HEADS ** -0.5

    def score_tile(j, last):
        off = pl.multiple_of(j * tk, tk)
        ki_lo = ki_ref[0, pl.ds(off, tk), 0:LANES]
        ki_hi = ki_ref[0, pl.ds(off, tk), LANES:2 * LANES]
        s_even = lax.dot_general(ki_lo, qi_stack, _NT, preferred_element_type=F32)
        s_odd = lax.dot_general(ki_hi, qi_stack, _NT, preferred_element_type=F32)
        score = jnp.zeros((tk, tq), F32)
        for u in range(IDX_HEADS // 2):
            score = score + jnp.maximum(s_even[:, u * tq:(u + 1) * tq], 0.0) * wrows[2 * u]
            score = score + jnp.maximum(s_odd[:, u * tq:(u + 1) * tq], 0.0) * wrows[2 * u + 1]
        key = _sort_key(score * score_scale)
        if last:
            kpos = j * tk + k_io
            vis = jnp.where((kpos >> 6) <= q_chunk, 1, 0) * jnp.where(kpos < length, 1, 0)
            key = jnp.where(vis > 0, key, INT_MIN)
        keys_s[j] = key

    def score_body(jj, carry):
        score_tile(2 * jj, False)
        score_tile(2 * jj + 1, False)
        return carry

    lax.fori_loop(0, (nvis - 1) // 2, score_body, 0)

    @pl.when(nvis % 2 == 0)
    def _():
        score_tile(nvis - 2, False)

    score_tile(nvis - 1, True)

    nacc = 4

    def column_count(hit_fn):
        def body(j, accs):
            kk = keys_s[j]
            accs = list(accs)
            for r in range(tk // 8):
                accs[r % nacc] = accs[r % nacc] + jnp.where(hit_fn(j, r, kk[8 * r:8 * r + 8]), 1.0, 0.0)
            return tuple(accs)
        accs = lax.fori_loop(0, nvis, body, (jnp.zeros((8, tq), F32),) * nacc)
        return jnp.sum(functools.reduce(lambda a, b: a + b, accs), axis=0, keepdims=True)

    def count_ge(cand):
        below = jnp.broadcast_to(cand - 1, (8, tq))
        return column_count(lambda j, r, kk: kk > below)

    def bit_body(it, ans):
        cand = ans | jnp.left_shift(jnp.int32(1), 31 - it)
        return jnp.where(count_ge(cand ^ INT_MIN) >= topk, cand, ans)

    thr = lax.fori_loop(0, 32, bit_body, jnp.zeros((1, tq), I32)) ^ INT_MIN
    thr_sel = jnp.maximum(thr, NEG_INF_KEY + 1)
    row_full = thr >= thr_sel

    cnt_ge = count_ge(thr_sel)
    cnt_gt = count_ge(thr_sel + 1)
    excess = jnp.max(jnp.where(row_full, cnt_ge - topk, 0.0))

    @pl.when(excess > 0.0)
    def _():
        quota = topk - cnt_gt
        thr8 = jnp.broadcast_to(thr, (8, tq))
        k_io8 = k_io[0:8]

        def count_tied_before(pos):
            pos8 = jnp.broadcast_to(pos, (8, tq))
            return column_count(lambda j, r, kk: jnp.where(kk == thr8, 1, 0)
                                * jnp.where(j * tk + 8 * r + k_io8 < pos8, 1, 0) > 0)

        def pos_body(it, p):
            cand = p | jnp.left_shift(jnp.int32(1), nbits - 1 - it)
            return jnp.where(count_tied_before(cand) < quota, cand, p)

        last = lax.fori_loop(0, nbits, pos_body, jnp.zeros((1, tq), I32))
        last = jnp.where(row_full, last, nk * tk)

        def demote(j, carry):
            kpos = j * tk + k_io
            kk = keys_s[j]
            drop = jnp.where(kk == thr, 1, 0) * jnp.where(kpos > last, 1, 0)
            keys_s[j] = jnp.where(drop > 0, thr - 1, kk)
            return carry

        lax.fori_loop(0, nvis, demote, 0)

    m_s[...] = jnp.full(m_s.shape, NEG, F32)
    acc_s[...] = jnp.zeros(acc_s.shape, F32)

    ncb = tk // LANES

    def attn_body(j, carry):
        off = pl.multiple_of(j * tk, tk)
        bvar = [var_ref[(i * nk + j) * ncb + cb] for cb in range(ncb)]
        maskb = jnp.where(keys_s[j] >= thr_sel, 0.0, NEG).T[0:tqa]
        s_pair = [lax.dot_general(qs_s[pr], k_ref[0, pl.ds(off, tk), pr * LANES:(pr + 1) * LANES], _NT,
                                  preferred_element_type=F32) for pr in range(npair)]
        for pr in range(npair):
            for c in range(nchunk):
                rows = slice(c * rc, (c + 1) * rc)
                hd = pr * hpp + (c * rc) // tqa
                brow = (c * rc) % tqa
                s = jnp.concatenate(
                    [s_pair[pr][rows, cb * LANES:(cb + 1) * LANES] + bias_ref[bvar[cb], hd, brow:brow + rc, :]
                     for cb in range(ncb)], axis=1) + maskb[brow:brow + rc]
                m_prev = m_s[pr, rows]
                m_new = jnp.maximum(m_prev, jnp.max(s, axis=1, keepdims=True))
                p = jnp.exp2(s - jnp.concatenate([m_new] * ncb, axis=1))
                p_s[pr, rows] = p.astype(BF16)
                al_s[pr, rows] = jnp.exp2(m_prev - m_new)
                m_s[pr, rows] = m_new
            vp = v_ref[0, pl.ds(off, tk), pr * LANES:(pr + 1) * LANES]
            v1 = jnp.concatenate([vp, jnp.ones_like(vp)], axis=1)
            alpha = al_s[pr]
            acc_s[pr] = (jnp.concatenate([alpha, alpha], axis=1) * acc_s[pr]
                         + jnp.dot(p_s[pr], v1, preferred_element_type=F32))
        return carry

    lax.fori_loop(0, nvis, attn_body, 0)

    for hd in range(DSA_HEADS):
        rows = slice((hd % hpp) * tqa, (hd % hpp + 1) * tqa)
        o = acc_s[hd // hpp, rows, 0:LANES] / acc_s[hd // hpp, rows, LANES:2 * LANES]
        o_ref[0, :, hd * LANES:(hd + 1) * LANES] = o.astype(o_ref.dtype)


def _dsa_attn(q, qi, wi, ki2, k, v, bias, nvis, var, q0, length, topk, tq, tk):
    b, sq, dq = q.shape
    lp = k.shape[1]
    nq = qi.shape[1] // tq
    tqa = sq // nq
    nk = lp // tk
    nvar = bias.shape[0]
    npair = DSA_KV_HEADS // 2
    prow = 2 * DSA_GROUP * tqa
    kernel = functools.partial(_dsa_kernel, tq=tq, tqa=tqa, tk=tk, nk=nk, rc=min(tqa, 64), q0=q0, length=length,
                               topk=topk, nbits=max(1, (lp - 1).bit_length()))
    once = dict(pipeline_mode=pl.Buffered(1))
    full = lambda bb, ii, *_: (bb, 0, 0)
    tile = lambda bb, ii, *_: (bb, ii, 0)
    return pl.pallas_call(
        kernel,
        grid_spec=pltpu.PrefetchScalarGridSpec(
            num_scalar_prefetch=2,
            grid=(b, nq),
            in_specs=[
                pl.BlockSpec((1, tqa, dq), tile),
                pl.BlockSpec((1, tq, qi.shape[2]), tile),
                pl.BlockSpec((1, tq, LANES), tile),
                pl.BlockSpec((1, lp, 2 * LANES), full, **once),
                pl.BlockSpec((1, lp, k.shape[2]), full, **once),
                pl.BlockSpec((1, lp, v.shape[2]), full, **once),
                pl.BlockSpec((nvar, BIAS_HEADS, tqa, LANES), lambda bb, ii, *_: (0, 0, 0, 0), **once),
            ],
            out_specs=pl.BlockSpec((1, tqa, dq), tile),
            scratch_shapes=[
                pltpu.VMEM((nk, tk, tq), I32),
                pltpu.VMEM((npair, prow, LANES), BF16),
                pltpu.VMEM((npair, prow, tk), BF16),
                pltpu.VMEM((npair, prow, LANES), F32),
                pltpu.VMEM((npair, prow, LANES), F32),
                pltpu.VMEM((npair, prow, 2 * LANES), F32),
            ],
        ),
        out_shape=jax.ShapeDtypeStruct((b, sq, dq), BF16),
        compiler_params=pltpu.CompilerParams(vmem_limit_bytes=VMEM_LIMIT),
        name="dsa_attn",
    )(jnp.asarray(nvis, I32), jnp.asarray(var, I32).reshape(-1), q, qi, wi, ki2, k, v, bias)


def _dsa_prompt(q, qi, wi, ki2, k, v, rel_bias):
    s = q.shape[1]
    tq, tk = 128, 512
    nq = s // tq
    nvis = [(i * tq) // tk + 1 for i in range(nq)]
    bias, var = _block_bias(rel_bias, nq, s // LANES, tq)
    return _dsa_attn(q, qi, wi, ki2, k, v, bias, nvis, var, 0, s, min(IDX_TOPK, s // 4), tq, tk)


def _route(lt_s, rb_ref, gt_s):
    rows = [lt_s[e:e + 1, :] for e in range(N_EXPERTS)]
    mx = functools.reduce(jnp.maximum, rows)
    ex = [jnp.exp(r - mx) for r in rows]
    z = functools.reduce(lambda a, b: a + b, ex)
    probs = [x / z for x in ex]
    sel = [probs[e] + rb_ref[e] for e in range(N_EXPERTS)]
    best_val, best_grp = None, None
    for gidx in range(N_GROUPS):
        a, b, c, d = sel[EXPERTS_PER_GROUP * gidx:EXPERTS_PER_GROUP * (gidx + 1)]
        hi1, lo1, hi2, lo2 = jnp.maximum(a, b), jnp.minimum(a, b), jnp.maximum(c, d), jnp.minimum(c, d)
        top2 = jnp.maximum(hi1, hi2) + jnp.maximum(jnp.minimum(hi1, hi2), jnp.maximum(lo1, lo2))
        if gidx == 0:
            best_val, best_grp = top2, jnp.zeros_like(top2, dtype=I32)
        else:
            better = top2 > best_val
            best_grp = jnp.where(better, gidx, best_grp)
            best_val = jnp.where(better, top2, best_val)
    masked = [jnp.where(best_grp == e // EXPERTS_PER_GROUP, sel[e], -jnp.inf) for e in range(N_EXPERTS)]

    def first_argmax(vals):
        bv, bi = vals[0], jnp.zeros_like(best_grp)
        for e in range(1, N_EXPERTS):
            better = vals[e] > bv
            bi = jnp.where(better, e, bi)
            bv = jnp.where(better, vals[e], bv)
        return bi

    i1 = first_argmax(masked)
    i2 = first_argmax([jnp.where(i1 == e, -jnp.inf, masked[e]) for e in range(N_EXPERTS)])
    w1 = functools.reduce(lambda a, b: a + b, [jnp.where(i1 == e, probs[e], 0.0) for e in range(N_EXPERTS)])
    w2 = functools.reduce(lambda a, b: a + b, [jnp.where(i2 == e, probs[e], 0.0) for e in range(N_EXPERTS)])
    wsum = w1 + w2
    w1, w2 = w1 / wsum, w2 / wsum
    for e in range(N_EXPERTS):
        gt_s[e:e + 1, :] = jnp.where(i1 == e, w1, 0.0) + jnp.where(i2 == e, w2, 0.0)


def _moe_kernel(rb_ref, x_ref, g_ref, shift_ref, scale_ref, gate_ref, wrh_ref, wrl_ref, wg_ref, wu_ref, wd_ref,
                gfin_ref, out_ref, hb_s, lt_s, gt_s, gates_s, acc_s, *, final_norm):
    e = pl.program_id(1)

    @pl.when(e == 0)
    def _():
        h = _norm_mod(x_ref[...], g_ref[...], shift_ref[0], scale_ref[0])
        hb = h.astype(BF16)
        hb_s[...] = hb
        hl = (h - hb.astype(F32)).astype(BF16)
        lt_s[...] = (lax.dot_general(wrh_ref[...], hb, _NT, preferred_element_type=F32)
                     + lax.dot_general(wrl_ref[...], hb, _NT, preferred_element_type=F32)
                     + lax.dot_general(wrh_ref[...], hl, _NT, preferred_element_type=F32))
        gt_s[...] = jnp.zeros(gt_s.shape, F32)
        _route(lt_s, rb_ref, gt_s)
        gates_s[...] = gt_s[...].T
        acc_s[...] = jnp.zeros(acc_s.shape, F32)

    hb = hb_s[...]
    up = jnp.dot(hb, wu_ref[0], preferred_element_type=F32)
    gt = jnp.dot(hb, wg_ref[0], preferred_element_type=F32)
    lane = lax.broadcasted_iota(I32, gates_s.shape, 1)
    gcol = jnp.sum(jnp.where(lane == e, gates_s[...], 0.0), axis=1, keepdims=True)
    a = (gt * jax.nn.sigmoid(gt) * up * gcol).astype(BF16)
    acc_s[...] += jnp.dot(a, wd_ref[0], preferred_element_type=F32)

    @pl.when(e == N_EXPERTS - 1)
    def _():
        y = x_ref[...] + _per_segment(acc_s[...], gate_ref[0], lambda v, s: v * s)
        if final_norm:
            y = _rms(y, gfin_ref[...])
        out_ref[...] = y


def _moe(x, g, shift, scale, gate, w_router, router_bias, wg_bf, wu_bf, wd_bf, g_final, final_norm, tm):
    n, d = x.shape
    nt = n // tm
    nseg = tm // CHUNK
    ff = wg_bf.shape[2]
    wrt = w_router.T
    wrh = wrt.astype(BF16)
    wrl = (wrt - wrh.astype(F32)).astype(BF16)
    seg_spec = pl.BlockSpec((1, nseg, d), lambda t, e: (t, 0, 0))
    row_spec = pl.BlockSpec((1, d), lambda t, e: (0, 0))
    return pl.pallas_call(
        functools.partial(_moe_kernel, final_norm=final_norm),
        grid=(nt, N_EXPERTS),
        in_specs=[
            pl.BlockSpec(memory_space=pltpu.SMEM),
            pl.BlockSpec((tm, d), lambda t, e: (t, 0)),
            row_spec, seg_spec, seg_spec, seg_spec,
            pl.BlockSpec((N_EXPERTS, d), lambda t, e: (0, 0)),
            pl.BlockSpec((N_EXPERTS, d), lambda t, e: (0, 0)),
            pl.BlockSpec((1, d, ff), lambda t, e: (e, 0, 0)),
            pl.BlockSpec((1, d, ff), lambda t, e: (e, 0, 0)),
            pl.BlockSpec((1, ff, d), lambda t, e: (e, 0, 0)),
            row_spec,
        ],
        out_specs=pl.BlockSpec((tm, d), lambda t, e: (t, 0)),
        out_shape=jax.ShapeDtypeStruct((n, d), F32),
        scratch_shapes=[
            pltpu.VMEM((tm, d), BF16),
            pltpu.VMEM((N_EXPERTS, tm), F32),
            pltpu.VMEM((LANES, tm), F32),
            pltpu.VMEM((tm, LANES), F32),
            pltpu.VMEM((tm, d), F32),
        ],
        compiler_params=pltpu.CompilerParams(vmem_limit_bytes=VMEM_LIMIT),
        name="moe",
    )(router_bias, x, g.reshape(1, d), shift.reshape(nt, nseg, d), scale.reshape(nt, nseg, d),
      gate.reshape(nt, nseg, d), wrh, wrl, wg_bf, wu_bf, wd_bf, g_final.reshape(1, d))


def _dsa_in_weights(w):
    d = w.shape[0]
    o1 = DSA_HEADS * DSA_HEAD_DIM
    o2 = o1 + DSA_KV_HEADS * DSA_HEAD_DIM
    o3 = o2 + DSA_KV_HEADS * DSA_HEAD_DIM
    o4 = o3 + IDX_HEADS * IDX_DIM
    o5 = o4 + IDX_DIM
    wq = w[:, :o1].reshape(d, DSA_HEADS, 1, DSA_HEAD_DIM)
    half = ((np.arange(DSA_HEADS) // DSA_GROUP) % 2).reshape(1, DSA_HEADS, 1, 1)
    wq = jnp.concatenate([jnp.where(half == 0, wq, 0.0), jnp.where(half == 1, wq, 0.0)], axis=2)
    wki = w[:, o4:o5]
    zk = jnp.zeros_like(wki)
    wwi = jnp.pad(w[:, o5:], ((0, 0), (0, LANES - IDX_HEADS)))
    return jnp.concatenate([wq.reshape(d, 2 * o1), w[:, o1:o4], wki, zk, zk, wki, wwi], axis=1).astype(BF16)


def _dsa_out_weights(w):
    d = w.shape[1]
    w4 = w.reshape(DSA_HEADS, 1, DSA_HEAD_DIM, d)
    half = ((np.arange(DSA_HEADS) // DSA_GROUP) % 2).reshape(DSA_HEADS, 1, 1, 1)
    w4 = jnp.concatenate([jnp.where(half == 0, w4, 0.0), jnp.where(half == 1, w4, 0.0)], axis=1)
    return w4.reshape(2 * DSA_HEADS * DSA_HEAD_DIM, d).astype(BF16)


def kernel(x_prompt, x_sample, cache_k_diff, cache_v_diff, cache_k_dsa, cache_v_dsa, cache_kidx_dsa,
           c_prompt, c_sample, rel_bias, w_in_diff, w_out_diff, lam_diff, subln_g_diff, w_in_dsa, w_out_dsa,
           w_ada, b_ada, g_norm_mix, g_norm_ffn, w_router, router_bias, w_gate, w_up, w_down, g_final):
    b, s, d = x_prompt.shape
    bs, ss, _ = x_sample.shape
    past = cache_k_diff.shape[2]
    depth = w_ada.shape[0]
    assert d == D_MODEL and ss == CHUNK and s % 512 == 0 and past % CHUNK == 0 and depth == 2
    l_s = past + ss
    lp_s = -(-l_s // LANES) * LANES
    nkb_s = lp_s // LANES

    streams = []
    for x3, rows in ((x_prompt, np.repeat(np.arange(b), s // CHUNK)), (x_sample, b + np.arange(bs))):
        n = x3.shape[0] * x3.shape[1]
        streams.append((x3.reshape(n, d), x3.shape[:2], rows,
                        _pick_tile(n, (512, 256, 128, 64)), _pick_tile(n, (1024, 512, 256, 128, 64))))
    mod = _ada_mod(jnp.concatenate([c_prompt, c_sample], axis=0), w_ada, b_ada)

    def mods(layer, rows):
        m = mod[layer][rows]
        return [m[:, i * d:(i + 1) * d] for i in range(6)]

    def pad_keys(x):
        return jnp.pad(x, ((0, 0), (0, lp_s - x.shape[1]), (0, 0)))

    bias_s = _bias_tiles(rel_bias, [jb * LANES - past for jb in range(nkb_s)], [past] * nkb_s, l_s, ss, LANES)
    var_s = [list(range(nkb_s))]
    xs = [st[0] for st in streams]
    outs_kv = []

    qk = DIFF_HEADS * 2 * DIFF_HEAD_DIM
    lam_init = 0.8 - 0.6 * float(np.exp(-0.3 * 0))
    q_scale = DIFF_HEAD_DIM ** -0.5 * LOG2E
    w_in0, w_out0 = w_in_diff[0].astype(BF16), w_out_diff[0].astype(BF16)
    wt_k0 = w_in_diff[0][:, qk:2 * qk].T.astype(BF16)
    moe_w0 = (w_gate[0].astype(BF16), w_up[0].astype(BF16), w_down[0].astype(BF16))
    for si, (_, (nb, ns), rows, tm, tm_moe) in enumerate(streams):
        m = mods(0, rows)
        if si == 0:
            cols = ((0, qk, ((0, 0, q_scale),)), (qk, qk, ((2, 0, 1.0),)), (2 * qk, qk, ((1, 0, 1.0), (3, 0, 1.0))))
            outs = ((qk, BF16), (qk, F32), (qk, BF16), (qk, BF16))
            q_bf, v_f, k_bf, v_bf, k_t = _in_proj(xs[si], g_norm_mix[0], m[0], m[1], w_in0, cols, outs, tm,
                                                  wt_k0, ((0, qk, F32),), ns)
            k_f = jnp.transpose(k_t.reshape(nb, DIFF_HEADS, 2, DIFF_HEAD_DIM, ns), (0, 4, 1, 2, 3))
        else:
            cols = ((0, qk, ((0, 0, q_scale),)), (qk, qk, ((1, 0, 1.0), (3, 0, 1.0))),
                    (2 * qk, qk, ((2, 0, 1.0), (4, 0, 1.0))))
            outs = ((qk, BF16), (qk, F32), (qk, F32), (qk, BF16), (qk, BF16))
            q_bf, k_f, v_f, k_bf, v_bf = _in_proj(xs[si], g_norm_mix[0], m[0], m[1], w_in0, cols, outs, tm)
        outs_kv.append((k_f, v_f))
        q3, k3, v3 = (a.reshape(nb, ns, qk) for a in (q_bf, k_bf, v_bf))
        if si == 0:
            o = _diff_prompt(q3, k3, v3, rel_bias, lam_diff[0], subln_g_diff[0], lam_init)
        else:
            ka = pad_keys(jnp.concatenate([cache_k_diff[0].reshape(nb, past, qk).astype(BF16), k3], axis=1))
            va = pad_keys(jnp.concatenate([cache_v_diff[0].reshape(nb, past, qk).astype(BF16), v3], axis=1))
            o = _diff_attn(q3, ka, va, bias_s, [(0, 0)], var_s, lam_diff[0], subln_g_diff[0], lam_init, ns, lp_s)
        x = _out_proj(o.reshape(nb * ns, qk), xs[si], m[2], w_out0, tm)
        xs[si] = _moe(x, g_norm_ffn[0], m[3], m[4], m[5], w_router, router_bias, *moe_w0, g_final, False, tm_moe)

    nq_w = 2 * DSA_HEADS * DSA_HEAD_DIM
    kv_w = DSA_KV_HEADS * DSA_HEAD_DIM
    qi_w = IDX_HEADS * IDX_DIM
    c_k, c_v, c_qi = nq_w, nq_w + kv_w, nq_w + 2 * kv_w
    c_ki, c_wi = c_qi + qi_w, c_qi + qi_w + 2 * LANES
    half_q = nq_w // 2
    q_scale = DSA_HEAD_DIM ** -0.5 * LOG2E
    cols_q = ((0, half_q, ((0, 0, q_scale),)), (half_q, half_q, ((0, half_q, q_scale),)))
    w_in1, w_out1 = _dsa_in_weights(w_in_dsa[0]), _dsa_out_weights(w_out_dsa[0])
    o_k = DSA_HEADS * DSA_HEAD_DIM
    o_v, o_ki = o_k + kv_w, o_k + 2 * kv_w + qi_w
    wt_1 = jnp.concatenate([w_in_dsa[0][:, o_k:o_v], w_in_dsa[0][:, o_v:o_v + kv_w],
                            w_in_dsa[0][:, o_ki:o_ki + IDX_DIM]], axis=1).T.astype(BF16)
    moe_w1 = (w_gate[1].astype(BF16), w_up[1].astype(BF16), w_down[1].astype(BF16))
    for si, (_, (nb, ns), rows, tm, tm_moe) in enumerate(streams):
        m = mods(1, rows)
        if si == 0:
            cols = cols_q + ((c_k, kv_w, ((1, 0, 1.0),)), (c_v, kv_w, ((2, 0, 1.0),)), (c_qi, qi_w, ((3, 0, 1.0),)),
                             (c_ki, 2 * LANES, ((4, 0, 1.0),)), (c_wi, LANES, ((5, 0, 1.0),)))
            outs = ((nq_w, BF16), (kv_w, BF16), (kv_w, BF16), (qi_w, BF16), (2 * LANES, BF16), (LANES, F32))
            q2, k_bf, v_bf, qi_bf, ki2_bf, wi_f, k_t, v_t, ki_t = _in_proj(
                xs[si], g_norm_mix[1], m[0], m[1], w_in1, cols, outs, tm, wt_1,
                ((0, kv_w, F32), (kv_w, kv_w, F32), (2 * kv_w, IDX_DIM, F32)), ns)
            k_f = jnp.transpose(k_t.reshape(nb, DSA_KV_HEADS, DSA_HEAD_DIM, ns), (0, 3, 1, 2))
            v_f = jnp.transpose(v_t.reshape(nb, DSA_KV_HEADS, DSA_HEAD_DIM, ns), (0, 3, 1, 2))
            ki_f = jnp.transpose(ki_t, (0, 2, 1))
        else:
            cols = cols_q + ((c_k, kv_w, ((1, 0, 1.0), (3, 0, 1.0))), (c_v, kv_w, ((2, 0, 1.0), (4, 0, 1.0))),
                             (c_qi, qi_w, ((5, 0, 1.0),)), (c_ki, 2 * LANES, ((7, 0, 1.0),)),
                             (c_ki, IDX_DIM, ((6, 0, 1.0),)), (c_wi, LANES, ((8, 0, 1.0),)))
            outs = ((nq_w, BF16), (kv_w, F32), (kv_w, F32), (kv_w, BF16), (kv_w, BF16), (qi_w, BF16),
                    (IDX_DIM, F32), (2 * LANES, BF16), (LANES, F32))
            q2, k_f, v_f, k_bf, v_bf, qi_bf, ki_f, ki2_bf, wi_f = _in_proj(
                xs[si], g_norm_mix[1], m[0], m[1], w_in1, cols, outs, tm)
        outs_kv.append((k_f, v_f, ki_f))
        q3, qi3, wi3 = q2.reshape(nb, ns, nq_w), qi_bf.reshape(nb, ns, qi_w), wi_f.reshape(nb, ns, LANES)
        k3, v3, ki3 = k_bf.reshape(nb, ns, kv_w), v_bf.reshape(nb, ns, kv_w), ki2_bf.reshape(nb, ns, 2 * LANES)
        if si == 0:
            o = _dsa_prompt(q3, qi3, wi3, ki3, k3, v3, rel_bias)
        else:
            cki = cache_kidx_dsa[0].astype(BF16)
            zki = jnp.zeros_like(cki)
            ka = pad_keys(jnp.concatenate([cache_k_dsa[0].reshape(nb, past, kv_w).astype(BF16), k3], axis=1))
            va = pad_keys(jnp.concatenate([cache_v_dsa[0].reshape(nb, past, kv_w).astype(BF16), v3], axis=1))
            kia = pad_keys(jnp.concatenate([jnp.concatenate([cki, zki, zki, cki], axis=2), ki3], axis=1))
            pad_q = lambda a: jnp.pad(a, ((0, 0), (0, LANES - ns), (0, 0)))
            o = _dsa_attn(q3, pad_q(qi3), pad_q(wi3), kia, ka, va, bias_s, [nkb_s], var_s, past, l_s,
                          min(IDX_TOPK, l_s // 4), LANES, LANES)
        x = _out_proj(o.reshape(nb * ns, nq_w), xs[si], m[2], w_out1, tm)
        xs[si] = _moe(x, g_norm_ffn[1], m[3], m[4], m[5], w_router, router_bias, *moe_w1, g_final, True, tm_moe)

    h2 = (DIFF_HEADS, 2, DIFF_HEAD_DIM)
    hv = (DIFF_HEADS, 2 * DIFF_HEAD_DIM)
    hk = (DSA_KV_HEADS, DSA_HEAD_DIM)
    (kd_p, vd_p), (kd_s, vd_s), (ks_p, vs_p, ki_p), (ks_s, vs_s, ki_s) = outs_kv
    return (xs[0].reshape(b, s, d), xs[1].reshape(bs, ss, d),
            kd_p.reshape(1, b, s, *h2), vd_p.reshape(1, b, s, *hv),
            ks_p.reshape(1, b, s, *hk), vs_p.reshape(1, b, s, *hk), ki_p.reshape(1, b, s, IDX_DIM),
            kd_s.reshape(1, bs, ss, *h2), vd_s.reshape(1, bs, ss, *hv),
            ks_s.reshape(1, bs, ss, *hk), vs_s.reshape(1, bs, ss, *hk), ki_s.reshape(1, bs, ss, IDX_DIM))
```

```python
import functools

import jax
import jax.numpy as jnp
import numpy as np
from jax import lax
from jax.experimental import pallas as pl
from jax.experimental.pallas import tpu as pltpu

F32 = jnp.float32
BF16 = jnp.bfloat16
I32 = jnp.int32

D_MODEL = 1024
CHUNK = 64
DIFF_HEADS = 8
DIFF_HEAD_DIM = 64
DSA_HEADS = 16
DSA_KV_HEADS = 4
DSA_GROUP = DSA_HEADS // DSA_KV_HEADS
DSA_HEAD_DIM = 64
IDX_HEADS = 8
IDX_DIM = 64
IDX_TOPK = 256
REL_BUCKETS = 32
BIAS_HEADS = 16
N_EXPERTS = 16
N_GROUPS = 4
EXPERTS_PER_GROUP = N_EXPERTS // N_GROUPS
EXPERT_FF = 512
RMS_EPS = 1e-6

LANES = 128
NEG = -1e30
LOG2E = 1.4426950408889634
INT_MIN = -(2 ** 31)
NEG_INF_KEY = -0x7F800000
VMEM_LIMIT = 56 * 1024 * 1024

_BUCKET_STEPS = (12, 16, 23, 32, 46, 64, 91)
FAR_BLOCKS = 2

_NT = (((1,), (1,)), ((), ()))


def _pick_tile(n, prefs):
    for t in prefs:
        if n % t == 0:
            return t
    raise ValueError(f"no tile in {prefs} divides {n}")


def _rms(x, g):
    ms = jnp.mean(x * x, axis=-1, keepdims=True)
    return x * lax.rsqrt(ms + RMS_EPS) * g


def _per_segment(x, seg_vals, op):
    tm, d = x.shape
    nseg = tm // CHUNK
    return op(x.reshape(nseg, CHUNK, d), seg_vals[:, None, :]).reshape(tm, d)


def _norm_mod(x, g, shift, scale):
    y = _rms(x, g)
    y = _per_segment(y, scale, lambda a, s: a * (1.0 + s))
    return _per_segment(y, shift, lambda a, s: a + s)


def _ada_kernel(c_ref, w_ref, b_ref, o_ref):
    c = c_ref[...]
    a = (c * jax.nn.sigmoid(c)).astype(BF16)
    o_ref[0] = jnp.dot(a, w_ref[0].astype(BF16), preferred_element_type=F32) + b_ref[0]


def _ada_mod(c_all, w_ada, b_ada):
    depth, d, n6 = w_ada.shape
    bc = c_all.shape[0]
    tn = _pick_tile(n6, (1536, 1024, 512, 128))
    return pl.pallas_call(
        _ada_kernel,
        grid=(depth, n6 // tn),
        in_specs=[
            pl.BlockSpec((bc, d), lambda l, j: (0, 0)),
            pl.BlockSpec((1, d, tn), lambda l, j: (l, 0, j)),
            pl.BlockSpec((1, 1, tn), lambda l, j: (l, 0, j)),
        ],
        out_specs=pl.BlockSpec((1, bc, tn), lambda l, j: (l, 0, j)),
        out_shape=jax.ShapeDtypeStruct((depth, bc, n6), F32),
        compiler_params=pltpu.CompilerParams(vmem_limit_bytes=VMEM_LIMIT),
        name="ada_mod",
    )(c_all, w_ada, b_ada.reshape(depth, 1, n6))


def _bias_kernel(off_ref, q0_ref, len_ref, rb_ref, o_ref, *, tq, tk):
    v = pl.program_id(0)
    h = pl.program_id(1)
    shape = (tq, tk)
    r = lax.broadcasted_iota(I32, shape, 0)
    c = lax.broadcasted_iota(I32, shape, 1)
    rel = off_ref[v] + c - r
    qpos = q0_ref[v] + r
    kpos = qpos + rel
    n = jnp.abs(rel)
    big = jnp.full(shape, 8, I32)
    for t in _BUCKET_STEPS:
        big = big + jnp.where(n >= t, 1, 0)
    bucket = jnp.where(n < 8, n, big) + jnp.where(rel > 0, REL_BUCKETS // 2, 0)
    val = jnp.zeros(shape, F32)
    for b in range(REL_BUCKETS):
        val = jnp.where(bucket == b, rb_ref[b, h], val)
    vis = jnp.where((kpos >> 6) <= (qpos >> 6), 1, 0) * jnp.where(kpos < len_ref[0], 1, 0)
    o_ref[0, 0] = jnp.where(vis > 0, val * LOG2E, NEG)


def _bias_tiles(rel_bias, offs, q0s, length, tq, tk):
    nvar = len(offs)
    shape = (tq, tk)
    return pl.pallas_call(
        functools.partial(_bias_kernel, tq=tq, tk=tk),
        grid_spec=pltpu.PrefetchScalarGridSpec(
            num_scalar_prefetch=3,
            grid=(nvar, BIAS_HEADS),
            in_specs=[pl.BlockSpec(memory_space=pltpu.SMEM)],
            out_specs=pl.BlockSpec((1, 1) + shape, lambda v, h, *_: (v, h, 0, 0)),
        ),
        out_shape=jax.ShapeDtypeStruct((nvar, BIAS_HEADS) + shape, F32),
        name="bias_tiles",
    )(jnp.asarray(offs, I32), jnp.asarray(q0s, I32), jnp.asarray([length], I32), rel_bias)


def _inproj_kernel(x_ref, g_ref, shift_ref, scale_ref, w_ref, wt_ref, *out_refs, cols, t_cols):
    h = _norm_mod(x_ref[...], g_ref[...], shift_ref[0], scale_ref[0]).astype(BF16)
    for start, width, dests in cols:
        p = jnp.dot(h, w_ref[:, start:start + width], preferred_element_type=F32)
        for out_idx, out_off, mult in dests:
            o_ref = out_refs[out_idx]
            val = p if mult == 1.0 else p * mult
            o_ref[:, out_off:out_off + width] = val.astype(o_ref.dtype)
    t_refs = out_refs[len(out_refs) - len(t_cols):]
    for (start, width, _), t_ref in zip(t_cols, t_refs):
        t_ref[0] = lax.dot_general(wt_ref[start:start + width, :], h, _NT,
                                   preferred_element_type=F32).astype(t_ref.dtype)


def _in_proj(x, g, shift, scale, w_bf, cols, out_defs, tm, wt_bf=None, t_cols=(), seq=None):
    n, d = x.shape
    nt = n // tm
    nseg = tm // CHUNK
    nout = w_bf.shape[1]
    if wt_bf is None:
        wt_bf = jnp.zeros((16, d), BF16)
    seg_spec = pl.BlockSpec((1, nseg, d), lambda t: (t, 0, 0))
    tpb = (seq or n) // tm
    return pl.pallas_call(
        functools.partial(_inproj_kernel, cols=cols, t_cols=t_cols),
        grid=(nt,),
        in_specs=[
            pl.BlockSpec((tm, d), lambda t: (t, 0)),
            pl.BlockSpec((1, d), lambda t: (0, 0)),
            seg_spec, seg_spec,
            pl.BlockSpec((d, nout), lambda t: (0, 0)),
            pl.BlockSpec(wt_bf.shape, lambda t: (0, 0)),
        ],
        out_specs=([pl.BlockSpec((tm, w), lambda t: (t, 0)) for w, _ in out_defs]
                   + [pl.BlockSpec((1, w, tm), lambda t: (t // tpb, 0, t % tpb)) for _, w, _ in t_cols]),
        out_shape=([jax.ShapeDtypeStruct((n, w), dt) for w, dt in out_defs]
                   + [jax.ShapeDtypeStruct((n // (seq or n), w, seq or n), dt) for _, w, dt in t_cols]),
        compiler_params=pltpu.CompilerParams(vmem_limit_bytes=VMEM_LIMIT),
        name="in_proj",
    )(x, g.reshape(1, d), shift.reshape(nt, nseg, d), scale.reshape(nt, nseg, d), w_bf, wt_bf)


def _outproj_kernel(o_ref, x_ref, gate_ref, w_ref, out_ref):
    r = jnp.dot(o_ref[...], w_ref[...], preferred_element_type=F32)
    out_ref[...] = x_ref[...] + _per_segment(r, gate_ref[0], lambda a, s: a * s)


def _out_proj(o, x, gate, w_bf, tm):
    n, d = x.shape
    kdim = o.shape[1]
    nt = n // tm
    nseg = tm // CHUNK
    return pl.pallas_call(
        _outproj_kernel,
        grid=(nt,),
        in_specs=[
            pl.BlockSpec((tm, kdim), lambda t: (t, 0)),
            pl.BlockSpec((tm, d), lambda t: (t, 0)),
            pl.BlockSpec((1, nseg, d), lambda t: (t, 0, 0)),
            pl.BlockSpec((kdim, d), lambda t: (0, 0)),
        ],
        out_specs=pl.BlockSpec((tm, d), lambda t: (t, 0)),
        out_shape=jax.ShapeDtypeStruct((n, d), F32),
        compiler_params=pltpu.CompilerParams(vmem_limit_bytes=VMEM_LIMIT),
        name="out_proj",
    )(o, x, gate.reshape(nt, nseg, d), w_bf)


def _diff_attn_kernel(it_ref, jt_ref, var_ref, fin_ref, lam_ref, q_ref, k_ref, v_ref, bias_ref, g_ref,
                      o_ref, qb_s, p_s, al_s, m_s, acc_s, *, tq, tk, tb, rc, nkb, lam_init):
    t = pl.program_id(1)
    nchunk = 2 * tq // rc
    ncb = tk // LANES
    nrb = tq // tb
    bvar = [[var_ref[(it_ref[t] * nrb + rb) * nkb + jt_ref[t] * ncb + cb] for cb in range(ncb)]
            for rb in range(nrb)]

    @pl.when(jt_ref[t] == 0)
    def _():
        m_s[...] = jnp.full(m_s.shape, NEG, F32)
        acc_s[...] = jnp.zeros(acc_s.shape, F32)
        lane = lax.broadcasted_iota(I32, (tq, LANES), 1)
        for h in range(DIFF_HEADS):
            qh = q_ref[0, :, h * LANES:(h + 1) * LANES].astype(F32)
            qb_s[h, 0:tq, :] = jnp.where(lane < DIFF_HEAD_DIM, qh, 0.0).astype(BF16)
            qb_s[h, tq:2 * tq, :] = jnp.where(lane >= DIFF_HEAD_DIM, qh, 0.0).astype(BF16)

    def logits(h):
        return lax.dot_general(qb_s[h], k_ref[0, :, h * LANES:(h + 1) * LANES], _NT, preferred_element_type=F32)

    s_next = logits(0)
    for h in range(DIFF_HEADS):
        s_all = s_next
        if h + 1 < DIFF_HEADS:
            s_next = logits(h + 1)
        for c in range(nchunk):
            rows = slice(c * rc, (c + 1) * rc)
            qrow = (c * rc) % tq
            rb, brow = qrow // tb, qrow % tb
            hm = 2 * h + (c * rc) // tq
            s = jnp.concatenate(
                [s_all[rows, cb * LANES:(cb + 1) * LANES] + bias_ref[bvar[rb][cb], hm, brow:brow + rc, :]
                 for cb in range(ncb)], axis=1)
            m_prev = m_s[h, rows]
            m_new = jnp.maximum(m_prev, jnp.max(s, axis=1, keepdims=True))
            p = jnp.exp2(s - jnp.concatenate([m_new] * ncb, axis=1))
            p_s[h, rows] = p.astype(BF16)
            al_s[h, rows] = jnp.exp2(m_prev - m_new)
            m_s[h, rows] = m_new
        vh = v_ref[0, :, h * LANES:(h + 1) * LANES]
        v1 = jnp.concatenate([vh, jnp.ones_like(vh)], axis=1)
        alpha = al_s[h]
        acc_s[h] = (jnp.concatenate([alpha, alpha], axis=1) * acc_s[h]
                    + jnp.dot(p_s[h], v1, preferred_element_type=F32))

    @pl.when(fin_ref[t] == 1)
    def _():
        lp = lam_ref[...]
        lam = (jnp.exp(jnp.sum(lp[0:1] * lp[1:2], axis=1, keepdims=True))
               - jnp.exp(jnp.sum(lp[2:3] * lp[3:4], axis=1, keepdims=True)) + lam_init)
        g = g_ref[...]
        for h in range(DIFF_HEADS):
            o0 = acc_s[h, 0:tq, 0:LANES] / acc_s[h, 0:tq, LANES:2 * LANES]
            o1 = acc_s[h, tq:2 * tq, 0:LANES] / acc_s[h, tq:2 * tq, LANES:2 * LANES]
            o = _rms(o0 - lam * o1, g) * (1.0 - lam_init)
            o_ref[0, :, h * LANES:(h + 1) * LANES] = o.astype(o_ref.dtype)


def _diff_attn(q, k, v, bias, pairs, var, lam_p, subln_g, lam_init, tq, tk):
    b, sq, dq = q.shape
    nvar, tb = bias.shape[0], bias.shape[2]
    nkb = k.shape[1] // LANES
    it = np.asarray([p[0] for p in pairs], np.int32)
    jt = np.asarray([p[1] for p in pairs], np.int32)
    vt = np.asarray(var, np.int32).reshape(-1)
    fin = np.asarray([1 if (n + 1 == len(pairs) or pairs[n + 1][0] != p[0]) else 0
                      for n, p in enumerate(pairs)], np.int32)
    rc = min(tq, 64)
    return pl.pallas_call(
        functools.partial(_diff_attn_kernel, tq=tq, tk=tk, tb=tb, rc=rc, nkb=nkb, lam_init=lam_init),
        grid_spec=pltpu.PrefetchScalarGridSpec(
            num_scalar_prefetch=4,
            grid=(b, len(pairs)),
            in_specs=[
                pl.BlockSpec((4, DIFF_HEAD_DIM), lambda bb, t, *_: (0, 0)),
                pl.BlockSpec((1, tq, dq), lambda bb, t, it_, jt_, vt_, fin_: (bb, it_[t], 0)),
                pl.BlockSpec((1, tk, dq), lambda bb, t, it_, jt_, vt_, fin_: (bb, jt_[t], 0)),
                pl.BlockSpec((1, tk, dq), lambda bb, t, it_, jt_, vt_, fin_: (bb, jt_[t], 0)),
                pl.BlockSpec((nvar, BIAS_HEADS, tb, LANES), lambda bb, t, *_: (0, 0, 0, 0),
                             pipeline_mode=pl.Buffered(1)),
                pl.BlockSpec((1, 2 * DIFF_HEAD_DIM), lambda bb, t, *_: (0, 0)),
            ],
            out_specs=pl.BlockSpec((1, tq, dq), lambda bb, t, it_, jt_, vt_, fin_: (bb, it_[t], 0)),
            scratch_shapes=[
                pltpu.VMEM((DIFF_HEADS, 2 * tq, LANES), BF16),
                pltpu.VMEM((DIFF_HEADS, 2 * tq, tk), BF16),
                pltpu.VMEM((DIFF_HEADS, 2 * tq, LANES), F32),
                pltpu.VMEM((DIFF_HEADS, 2 * tq, LANES), F32),
                pltpu.VMEM((DIFF_HEADS, 2 * tq, 2 * LANES), F32),
            ],
        ),
        out_shape=jax.ShapeDtypeStruct((b, sq, dq), BF16),
        compiler_params=pltpu.CompilerParams(vmem_limit_bytes=VMEM_LIMIT),
        name="diff_attn",
    )(jnp.asarray(it), jnp.asarray(jt), jnp.asarray(vt), jnp.asarray(fin), lam_p, q, k, v, bias,
      subln_g.reshape(1, -1))


def _block_bias(rel_bias, nq, nkb, tq):
    bpq = tq // LANES
    lo, hi = -bpq, FAR_BLOCKS
    bias = _bias_tiles(rel_bias, [-LANES * d for d in range(lo, hi + 1)], [4 * tq] * (hi - lo + 1), 1 << 30, tq, LANES)
    var = [[min(max(bpq * i - jb, lo), hi) - lo for jb in range(nkb)] for i in range(nq)]
    return bias, var


def _diff_prompt(q, k, v, rel_bias, lam_p, subln_g, lam_init):
    s = q.shape[1]
    tq = tk = 512
    nq = s // tq
    pairs = [(i, j) for i in range(nq) for j in range((i * tq) // tk + 1)]
    bias, var = _block_bias(rel_bias, s // LANES, s // LANES, LANES)
    return _diff_attn(q, k, v, bias, pairs, var, lam_p, subln_g, lam_init, tq, tk)


def _sort_key(x):
    bits = pltpu.bitcast(x, I32)
    return jnp.where(bits < 0, INT_MIN - bits, bits)


def _dsa_kernel(nvis_ref, var_ref, q_ref, qi_ref, wi_ref, ki_ref, k_ref, v_ref, bias_ref, o_ref,
                keys_s, qs_s, p_s, al_s, m_s, acc_s, *, tq, tqa, tk, nk, rc, q0, length, topk, nbits):
    i = pl.program_id(1)
    nvis = nvis_ref[i]
    npair = DSA_KV_HEADS // 2
    hpp = 2 * DSA_GROUP
    nchunk = hpp * tqa // rc

    for hd in range(DSA_HEADS):
        qs_s[hd // hpp, (hd % hpp) * tqa:(hd % hpp + 1) * tqa, :] = q_ref[0, :, hd * LANES:(hd + 1) * LANES]
    qi_stack = jnp.concatenate([qi_ref[0, :, u * LANES:(u + 1) * LANES] for u in range(IDX_HEADS // 2)], axis=0)
    wi_t = wi_ref[0].T
    wrows = [wi_t[h:h + 1, :] for h in range(IDX_HEADS)]

    k_io = lax.broadcasted_iota(I32, (tk, tq), 0)
    q_chunk = (q0 + i * tq + lax.broadcasted_iota(I32, (tk, tq), 1)) >> 6
    score_scale = IDX_DIM ** -0.5 * IDX_HEADS ** -0.5

    def score_tile(j, last):
        off = pl.multiple_of(j * tk, tk)
        ki_lo = ki_ref[0, pl.ds(off, tk), 0:LANES]
        ki_hi = ki_ref[0, pl.ds(off, tk), LANES:2 * LANES]
        s_even = lax.dot_general(ki_lo, qi_stack, _NT, preferred_element_type=F32)
        s_odd = lax.dot_general(ki_hi, qi_stack, _NT, preferred_element_type=F32)
        score = jnp.zeros((tk, tq), F32)
        for u in range(IDX_HEADS // 2):
            score = score + jnp.maximum(s_even[:, u * tq:(u + 1) * tq], 0.0) * wrows[2 * u]
            score = score + jnp.maximum(s_odd[:, u * tq:(u + 1) * tq], 0.0) * wrows[2 * u + 1]
        key = _sort_key(score * score_scale)
        if last:
            kpos = j * tk + k_io
            vis = jnp.where((kpos >> 6) <= q_chunk, 1, 0) * jnp.where(kpos < length, 1, 0)
            key = jnp.where(vis > 0, key, INT_MIN)
        keys_s[j] = key

    def score_body(jj, carry):
        score_tile(2 * jj, False)
        score_tile(2 * jj + 1, False)
        return carry

    lax.fori_loop(0, (nvis - 1) // 2, score_body, 0)

    @pl.when(nvis % 2 == 0)
    def _():
        score_tile(nvis - 2, False)

    score_tile(nvis - 1, True)

    nacc = 4

    def column_count(hit_fn):
        def body(j, accs):
            kk = keys_s[j]
            accs = list(accs)
            for r in range(tk // 8):
                accs[r % nacc] = accs[r % nacc] + jnp.where(hit_fn(j, r, kk[8 * r:8 * r + 8]), 1.0, 0.0)
            return tuple(accs)
        accs = lax.fori_loop(0, nvis, body, (jnp.zeros((8, tq), F32),) * nacc)
        return jnp.sum(functools.reduce(lambda a, b: a + b, accs), axis=0, keepdims=True)

    def count_ge(cand):
        below = jnp.broadcast_to(cand - 1, (8, tq))
        return column_count(lambda j, r, kk: kk > below)

    def bit_body(it, ans):
        cand = ans | jnp.left_shift(jnp.int32(1), 31 - it)
        return jnp.where(count_ge(cand ^ INT_MIN) >= topk, cand, ans)

    thr = lax.fori_loop(0, 32, bit_body, jnp.zeros((1, tq), I32)) ^ INT_MIN
    thr_sel = jnp.maximum(thr, NEG_INF_KEY + 1)
    row_full = thr >= thr_sel

    cnt_ge = count_ge(thr_sel)
    cnt_gt = count_ge(thr_sel + 1)
    excess = jnp.max(jnp.where(row_full, cnt_ge - topk, 0.0))

    @pl.when(excess > 0.0)
    def _():
        quota = topk - cnt_gt
        thr8 = jnp.broadcast_to(thr, (8, tq))
        k_io8 = k_io[0:8]

        def count_tied_before(pos):
            pos8 = jnp.broadcast_to(pos, (8, tq))
            return column_count(lambda j, r, kk: jnp.where(kk == thr8, 1, 0)
                                * jnp.where(j * tk + 8 * r + k_io8 < pos8, 1, 0) > 0)

        def pos_body(it, p):
            cand = p | jnp.left_shift(jnp.int32(1), nbits - 1 - it)
            return jnp.where(count_tied_before(cand) < quota, cand, p)

        last = lax.fori_loop(0, nbits, pos_body, jnp.zeros((1, tq), I32))
        last = jnp.where(row_full, last, nk * tk)

        def demote(j, carry):
            kpos = j * tk + k_io
            kk = keys_s[j]
            drop = jnp.where(kk == thr, 1, 0) * jnp.where(kpos > last, 1, 0)
            keys_s[j] = jnp.where(drop > 0, thr - 1, kk)
            return carry

        lax.fori_loop(0, nvis, demote, 0)

    m_s[...] = jnp.full(m_s.shape, NEG, F32)
    acc_s[...] = jnp.zeros(acc_s.shape, F32)

    ncb = tk // LANES

    def attn_body(j, carry):
        off = pl.multiple_of(j * tk, tk)
        bvar = [var_ref[(i * nk + j) * ncb + cb] for cb in range(ncb)]
        maskb = jnp.where(keys_s[j] >= thr_sel, 0.0, NEG).T[0:tqa]
        s_pair = [lax.dot_general(qs_s[pr], k_ref[0, pl.ds(off, tk), pr * LANES:(pr + 1) * LANES], _NT,
                                  preferred_element_type=F32) for pr in range(npair)]
        for pr in range(npair):
            for c in range(nchunk):
                rows = slice(c * rc, (c + 1) * rc)
                hd = pr * hpp + (c * rc) // tqa
                brow = (c * rc) % tqa
                s = jnp.concatenate(
                    [s_pair[pr][rows, cb * LANES:(cb + 1) * LANES] + bias_ref[bvar[cb], hd, brow:brow + rc, :]
                     for cb in range(ncb)], axis=1) + maskb[brow:brow + rc]
                m_prev = m_s[pr, rows]
                m_new = jnp.maximum(m_prev, jnp.max(s, axis=1, keepdims=True))
                p = jnp.exp2(s - jnp.concatenate([m_new] * ncb, axis=1))
                p_s[pr, rows] = p.astype(BF16)
                al_s[pr, rows] = jnp.exp2(m_prev - m_new)
                m_s[pr, rows] = m_new
            vp = v_ref[0, pl.ds(off, tk), pr * LANES:(pr + 1) * LANES]
            v1 = jnp.concatenate([vp, jnp.ones_like(vp)], axis=1)
            alpha = al_s[pr]
            acc_s[pr] = (jnp.concatenate([alpha, alpha], axis=1) * acc_s[pr]
                         + jnp.dot(p_s[pr], v1, preferred_element_type=F32))
        return carry

    lax.fori_loop(0, nvis, attn_body, 0)

    for hd in range(DSA_HEADS):
        rows = slice((hd % hpp) * tqa, (hd % hpp + 1) * tqa)
        o = acc_s[hd // hpp, rows, 0:LANES] / acc_s[hd // hpp, rows, LANES:2 * LANES]
        o_ref[0, :, hd * LANES:(hd + 1) * LANES] = o.astype(o_ref.dtype)


def _dsa_attn(q, qi, wi, ki2, k, v, bias, nvis, var, q0, length, topk, tq, tk):
    b, sq, dq = q.shape
    lp = k.shape[1]
    nq = qi.shape[1] // tq
    tqa = sq // nq
    nk = lp // tk
    nvar = bias.shape[0]
    npair = DSA_KV_HEADS // 2
    prow = 2 * DSA_GROUP * tqa
    kernel = functools.partial(_dsa_kernel, tq=tq, tqa=tqa, tk=tk, nk=nk, rc=min(tqa, 64), q0=q0, length=length,
                               topk=topk, nbits=max(1, (lp - 1).bit_length()))
    once = dict(pipeline_mode=pl.Buffered(1))
    full = lambda bb, ii, *_: (bb, 0, 0)
    tile = lambda bb, ii, *_: (bb, ii, 0)
    return pl.pallas_call(
        kernel,
        grid_spec=pltpu.PrefetchScalarGridSpec(
            num_scalar_prefetch=2,
            grid=(b, nq),
            in_specs=[
                pl.BlockSpec((1, tqa, dq), tile),
                pl.BlockSpec((1, tq, qi.shape[2]), tile),
                pl.BlockSpec((1, tq, LANES), tile),
                pl.BlockSpec((1, lp, 2 * LANES), full, **once),
                pl.BlockSpec((1, lp, k.shape[2]), full, **once),
                pl.BlockSpec((1, lp, v.shape[2]), full, **once),
                pl.BlockSpec((nvar, BIAS_HEADS, tqa, LANES), lambda bb, ii, *_: (0, 0, 0, 0), **once),
            ],
            out_specs=pl.BlockSpec((1, tqa, dq), tile),
            scratch_shapes=[
                pltpu.VMEM((nk, tk, tq), I32),
                pltpu.VMEM((npair, prow, LANES), BF16),
                pltpu.VMEM((npair, prow, tk), BF16),
                pltpu.VMEM((npair, prow, LANES), F32),
                pltpu.VMEM((npair, prow, LANES), F32),
                pltpu.VMEM((npair, prow, 2 * LANES), F32),
            ],
        ),
        out_shape=jax.ShapeDtypeStruct((b, sq, dq), BF16),
        compiler_params=pltpu.CompilerParams(vmem_limit_bytes=VMEM_LIMIT),
        name="dsa_attn",
    )(jnp.asarray(nvis, I32), jnp.asarray(var, I32).reshape(-1), q, qi, wi, ki2, k, v, bias)


def _dsa_prompt(q, qi, wi, ki2, k, v, rel_bias):
    s = q.shape[1]
    tq, tk = 128, 512
    nq = s // tq
    nvis = [(i * tq) // tk + 1 for i in range(nq)]
    bias, var = _block_bias(rel_bias, nq, s // LANES, tq)
    return _dsa_attn(q, qi, wi, ki2, k, v, bias, nvis, var, 0, s, min(IDX_TOPK, s // 4), tq, tk)


def _route(lt_s, rb_ref, gt_s):
    rows = [lt_s[e:e + 1, :] for e in range(N_EXPERTS)]
    mx = functools.reduce(jnp.maximum, rows)
    ex = [jnp.exp(r - mx) for r in rows]
    z = functools.reduce(lambda a, b: a + b, ex)
    probs = [x / z for x in ex]
    sel = [probs[e] + rb_ref[e] for e in range(N_EXPERTS)]
    best_val, best_grp = None, None
    for gidx in range(N_GROUPS):
        a, b, c, d = sel[EXPERTS_PER_GROUP * gidx:EXPERTS_PER_GROUP * (gidx + 1)]
        hi1, lo1, hi2, lo2 = jnp.maximum(a, b), jnp.minimum(a, b), jnp.maximum(c, d), jnp.minimum(c, d)
        top2 = jnp.maximum(hi1, hi2) + jnp.maximum(jnp.minimum(hi1, hi2), jnp.maximum(lo1, lo2))
        if gidx == 0:
            best_val, best_grp = top2, jnp.zeros_like(top2, dtype=I32)
        else:
            better = top2 > best_val
            best_grp = jnp.where(better, gidx, best_grp)
            best_val = jnp.where(better, top2, best_val)
    masked = [jnp.where(best_grp == e // EXPERTS_PER_GROUP, sel[e], -jnp.inf) for e in range(N_EXPERTS)]

    def first_argmax(vals):
        bv, bi = vals[0], jnp.zeros_like(best_grp)
        for e in range(1, N_EXPERTS):
            better = vals[e] > bv
            bi = jnp.where(better, e, bi)
            bv = jnp.where(better, vals[e], bv)
        return bi

    i1 = first_argmax(masked)
    i2 = first_argmax([jnp.where(i1 == e, -jnp.inf, masked[e]) for e in range(N_EXPERTS)])
    w1 = functools.reduce(lambda a, b: a + b, [jnp.where(i1 == e, probs[e], 0.0) for e in range(N_EXPERTS)])
    w2 = functools.reduce(lambda a, b: a + b, [jnp.where(i2 == e, probs[e], 0.0) for e in range(N_EXPERTS)])
    wsum = w1 + w2
    w1, w2 = w1 / wsum, w2 / wsum
    for e in range(N_EXPERTS):
        gt_s[e:e + 1, :] = jnp.where(i1 == e, w1, 0.0) + jnp.where(i2 == e, w2, 0.0)


def _moe_kernel(rb_ref, x_ref, g_ref, shift_ref, scale_ref, gate_ref, wrh_ref, wrl_ref, wg_ref, wu_ref, wd_ref,
                gfin_ref, out_ref, hb_s, lt_s, gt_s, gates_s, acc_s, *, final_norm):
    e = pl.program_id(1)

    @pl.when(e == 0)
    def _():
        h = _norm_mod(x_ref[...], g_ref[...], shift_ref[0], scale_ref[0])
        hb = h.astype(BF16)
        hb_s[...] = hb
        hl = (h - hb.astype(F32)).astype(BF16)
        lt_s[...] = (lax.dot_general(wrh_ref[...], hb, _NT, preferred_element_type=F32)
                     + lax.dot_general(wrl_ref[...], hb, _NT, preferred_element_type=F32)
                     + lax.dot_general(wrh_ref[...], hl, _NT, preferred_element_type=F32))
        gt_s[...] = jnp.zeros(gt_s.shape, F32)
        _route(lt_s, rb_ref, gt_s)
        gates_s[...] = gt_s[...].T
        acc_s[...] = jnp.zeros(acc_s.shape, F32)

    hb = hb_s[...]
    up = jnp.dot(hb, wu_ref[0], preferred_element_type=F32)
    gt = jnp.dot(hb, wg_ref[0], preferred_element_type=F32)
    lane = lax.broadcasted_iota(I32, gates_s.shape, 1)
    gcol = jnp.sum(jnp.where(lane == e, gates_s[...], 0.0), axis=1, keepdims=True)
    a = (gt * jax.nn.sigmoid(gt) * up * gcol).astype(BF16)
    acc_s[...] += jnp.dot(a, wd_ref[0], preferred_element_type=F32)

    @pl.when(e == N_EXPERTS - 1)
    def _():
        y = x_ref[...] + _per_segment(acc_s[...], gate_ref[0], lambda v, s: v * s)
        if final_norm:
            y = _rms(y, gfin_ref[...])
        out_ref[...] = y


def _moe(x, g, shift, scale, gate, w_router, router_bias, wg_bf, wu_bf, wd_bf, g_final, final_norm, tm):
    n, d = x.shape
    nt = n // tm
    nseg = tm // CHUNK
    ff = wg_bf.shape[2]
    wrt = w_router.T
    wrh = wrt.astype(BF16)
    wrl = (wrt - wrh.astype(F32)).astype(BF16)
    seg_spec = pl.BlockSpec((1, nseg, d), lambda t, e: (t, 0, 0))
    row_spec = pl.BlockSpec((1, d), lambda t, e: (0, 0))
    return pl.pallas_call(
        functools.partial(_moe_kernel, final_norm=final_norm),
        grid=(nt, N_EXPERTS),
        in_specs=[
            pl.BlockSpec(memory_space=pltpu.SMEM),
            pl.BlockSpec((tm, d), lambda t, e: (t, 0)),
            row_spec, seg_spec, seg_spec, seg_spec,
            pl.BlockSpec((N_EXPERTS, d), lambda t, e: (0, 0)),
            pl.BlockSpec((N_EXPERTS, d), lambda t, e: (0, 0)),
            pl.BlockSpec((1, d, ff), lambda t, e: (e, 0, 0)),
            pl.BlockSpec((1, d, ff), lambda t, e: (e, 0, 0)),
            pl.BlockSpec((1, ff, d), lambda t, e: (e, 0, 0)),
            row_spec,
        ],
        out_specs=pl.BlockSpec((tm, d), lambda t, e: (t, 0)),
        out_shape=jax.ShapeDtypeStruct((n, d), F32),
        scratch_shapes=[
            pltpu.VMEM((tm, d), BF16),
            pltpu.VMEM((N_EXPERTS, tm), F32),
            pltpu.VMEM((LANES, tm), F32),
            pltpu.VMEM((tm, LANES), F32),
            pltpu.VMEM((tm, d), F32),
        ],
        compiler_params=pltpu.CompilerParams(vmem_limit_bytes=VMEM_LIMIT),
        name="moe",
    )(router_bias, x, g.reshape(1, d), shift.reshape(nt, nseg, d), scale.reshape(nt, nseg, d),
      gate.reshape(nt, nseg, d), wrh, wrl, wg_bf, wu_bf, wd_bf, g_final.reshape(1, d))


def _dsa_in_weights(w):
    d = w.shape[0]
    o1 = DSA_HEADS * DSA_HEAD_DIM
    o2 = o1 + DSA_KV_HEADS * DSA_HEAD_DIM
    o3 = o2 + DSA_KV_HEADS * DSA_HEAD_DIM
    o4 = o3 + IDX_HEADS * IDX_DIM
    o5 = o4 + IDX_DIM
    wq = w[:, :o1].reshape(d, DSA_HEADS, 1, DSA_HEAD_DIM)
    half = ((np.arange(DSA_HEADS) // DSA_GROUP) % 2).reshape(1, DSA_HEADS, 1, 1)
    wq = jnp.concatenate([jnp.where(half == 0, wq, 0.0), jnp.where(half == 1, wq, 0.0)], axis=2)
    wki = w[:, o4:o5]
    zk = jnp.zeros_like(wki)
    wwi = jnp.pad(w[:, o5:], ((0, 0), (0, LANES - IDX_HEADS)))
    return jnp.concatenate([wq.reshape(d, 2 * o1), w[:, o1:o4], wki, zk, zk, wki, wwi], axis=1).astype(BF16)


def _dsa_out_weights(w):
    d = w.shape[1]
    w4 = w.reshape(DSA_HEADS, 1, DSA_HEAD_DIM, d)
    half = ((np.arange(DSA_HEADS) // DSA_GROUP) % 2).reshape(DSA_HEADS, 1, 1, 1)
    w4 = jnp.concatenate([jnp.where(half == 0, w4, 0.0), jnp.where(half == 1, w4, 0.0)], axis=1)
    return w4.reshape(2 * DSA_HEADS * DSA_HEAD_DIM, d).astype(BF16)


def kernel(x_prompt, x_sample, cache_k_diff, cache_v_diff, cache_k_dsa, cache_v_dsa, cache_kidx_dsa,
           c_prompt, c_sample, rel_bias, w_in_diff, w_out_diff, lam_diff, subln_g_diff, w_in_dsa, w_out_dsa,
           w_ada, b_ada, g_norm_mix, g_norm_ffn, w_router, router_bias, w_gate, w_up, w_down, g_final):
    b, s, d = x_prompt.shape
    bs, ss, _ = x_sample.shape
    past = cache_k_diff.shape[2]
    depth = w_ada.shape[0]
    assert d == D_MODEL and ss == CHUNK and s % 512 == 0 and past % CHUNK == 0 and depth == 2
    l_s = past + ss
    lp_s = -(-l_s // LANES) * LANES
    nkb_s = lp_s // LANES

    streams = []
    for x3, rows in ((x_prompt, np.repeat(np.arange(b), s // CHUNK)), (x_sample, b + np.arange(bs))):
        n = x3.shape[0] * x3.shape[1]
        streams.append((x3.reshape(n, d), x3.shape[:2], rows,
                        _pick_tile(n, (512, 256, 128, 64)), _pick_tile(n, (1024, 512, 256, 128, 64))))
    mod = _ada_mod(jnp.concatenate([c_prompt, c_sample], axis=0), w_ada, b_ada)

    def mods(layer, rows):
        m = mod[layer][rows]
        return [m[:, i * d:(i + 1) * d] for i in range(6)]

    def pad_keys(x):
        return jnp.pad(x, ((0, 0), (0, lp_s - x.shape[1]), (0, 0)))

    bias_s = _bias_tiles(rel_bias, [jb * LANES - past for jb in range(nkb_s)], [past] * nkb_s, l_s, ss, LANES)
    var_s = [list(range(nkb_s))]
    xs = [st[0] for st in streams]
    outs_kv = []

    qk = DIFF_HEADS * 2 * DIFF_HEAD_DIM
    lam_init = 0.8 - 0.6 * float(np.exp(-0.3 * 0))
    q_scale = DIFF_HEAD_DIM ** -0.5 * LOG2E
    w_in0, w_out0 = w_in_diff[0].astype(BF16), w_out_diff[0].astype(BF16)
    wt_k0 = w_in_diff[0][:, qk:2 * qk].T.astype(BF16)
    moe_w0 = (w_gate[0].astype(BF16), w_up[0].astype(BF16), w_down[0].astype(BF16))
    for si, (_, (nb, ns), rows, tm, tm_moe) in enumerate(streams):
        m = mods(0, rows)
        if si == 0:
            cols = ((0, qk, ((0, 0, q_scale),)), (qk, qk, ((2, 0, 1.0),)), (2 * qk, qk, ((1, 0, 1.0), (3, 0, 1.0))))
            outs = ((qk, BF16), (qk, F32), (qk, BF16), (qk, BF16))
            q_bf, v_f, k_bf, v_bf, k_t = _in_proj(xs[si], g_norm_mix[0], m[0], m[1], w_in0, cols, outs, tm,
                                                  wt_k0, ((0, qk, F32),), ns)
            k_f = jnp.transpose(k_t.reshape(nb, DIFF_HEADS, 2, DIFF_HEAD_DIM, ns), (0, 4, 1, 2, 3))
        else:
            cols = ((0, qk, ((0, 0, q_scale),)), (qk, qk, ((1, 0, 1.0), (3, 0, 1.0))),
                    (2 * qk, qk, ((2, 0, 1.0), (4, 0, 1.0))))
            outs = ((qk, BF16), (qk, F32), (qk, F32), (qk, BF16), (qk, BF16))
            q_bf, k_f, v_f, k_bf, v_bf = _in_proj(xs[si], g_norm_mix[0], m[0], m[1], w_in0, cols, outs, tm)
        outs_kv.append((k_f, v_f))
        q3, k3, v3 = (a.reshape(nb, ns, qk) for a in (q_bf, k_bf, v_bf))
        if si == 0:
            o = _diff_prompt(q3, k3, v3, rel_bias, lam_diff[0], subln_g_diff[0], lam_init)
        else:
            ka = pad_keys(jnp.concatenate([cache_k_diff[0].reshape(nb, past, qk).astype(BF16), k3], axis=1))
            va = pad_keys(jnp.concatenate([cache_v_diff[0].reshape(nb, past, qk).astype(BF16), v3], axis=1))
            o = _diff_attn(q3, ka, va, bias_s, [(0, 0)], var_s, lam_diff[0], subln_g_diff[0], lam_init, ns, lp_s)
        x = _out_proj(o.reshape(nb * ns, qk), xs[si], m[2], w_out0, tm)
        xs[si] = _moe(x, g_norm_ffn[0], m[3], m[4], m[5], w_router, router_bias, *moe_w0, g_final, False, tm_moe)

    nq_w = 2 * DSA_HEADS * DSA_HEAD_DIM
    kv_w = DSA_KV_HEADS * DSA_HEAD_DIM
    qi_w = IDX_HEADS * IDX_DIM
    c_k, c_v, c_qi = nq_w, nq_w + kv_w, nq_w + 2 * kv_w
    c_ki, c_wi = c_qi + qi_w, c_qi + qi_w + 2 * LANES
    half_q = nq_w // 2
    q_scale = DSA_HEAD_DIM ** -0.5 * LOG2E
    cols_q = ((0, half_q, ((0, 0, q_scale),)), (half_q, half_q, ((0, half_q, q_scale),)))
    w_in1, w_out1 = _dsa_in_weights(w_in_dsa[0]), _dsa_out_weights(w_out_dsa[0])
    o_k = DSA_HEADS * DSA_HEAD_DIM
    o_v, o_ki = o_k + kv_w, o_k + 2 * kv_w + qi_w
    wt_1 = jnp.concatenate([w_in_dsa[0][:, o_k:o_v], w_in_dsa[0][:, o_v:o_v + kv_w],
                            w_in_dsa[0][:, o_ki:o_ki + IDX_DIM]], axis=1).T.astype(BF16)
    moe_w1 = (w_gate[1].astype(BF16), w_up[1].astype(BF16), w_down[1].astype(BF16))
    for si, (_, (nb, ns), rows, tm, tm_moe) in enumerate(streams):
        m = mods(1, rows)
        if si == 0:
            cols = cols_q + ((c_k, kv_w, ((1, 0, 1.0),)), (c_v, kv_w, ((2, 0, 1.0),)), (c_qi, qi_w, ((3, 0, 1.0),)),
                             (c_ki, 2 * LANES, ((4, 0, 1.0),)), (c_wi, LANES, ((5, 0, 1.0),)))
            outs = ((nq_w, BF16), (kv_w, BF16), (kv_w, BF16), (qi_w, BF16), (2 * LANES, BF16), (LANES, F32))
            q2, k_bf, v_bf, qi_bf, ki2_bf, wi_f, k_t, v_t, ki_t = _in_proj(
                xs[si], g_norm_mix[1], m[0], m[1], w_in1, cols, outs, tm, wt_1,
                ((0, kv_w, F32), (kv_w, kv_w, F32), (2 * kv_w, IDX_DIM, F32)), ns)
            k_f = jnp.transpose(k_t.reshape(nb, DSA_KV_HEADS, DSA_HEAD_DIM, ns), (0, 3, 1, 2))
            v_f = jnp.transpose(v_t.reshape(nb, DSA_KV_HEADS, DSA_HEAD_DIM, ns), (0, 3, 1, 2))
            ki_f = jnp.transpose(ki_t, (0, 2, 1))
        else:
            cols = cols_q + ((c_k, kv_w, ((1, 0, 1.0), (3, 0, 1.0))), (c_v, kv_w, ((2, 0, 1.0), (4, 0, 1.0))),
                             (c_qi, qi_w, ((5, 0, 1.0),)), (c_ki, 2 * LANES, ((7, 0, 1.0),)),
                             (c_ki, IDX_DIM, ((6, 0, 1.0),)), (c_wi, LANES, ((8, 0, 1.0),)))
            outs = ((nq_w, BF16), (kv_w, F32), (kv_w, F32), (kv_w, BF16), (kv_w, BF16), (qi_w, BF16),
                    (IDX_DIM, F32), (2 * LANES, BF16), (LANES, F32))
            q2, k_f, v_f, k_bf, v_bf, qi_bf, ki_f, ki2_bf, wi_f = _in_proj(
                xs[si], g_norm_mix[1], m[0], m[1], w_in1, cols, outs, tm)
        outs_kv.append((k_f, v_f, ki_f))
        q3, qi3, wi3 = q2.reshape(nb, ns, nq_w), qi_bf.reshape(nb, ns, qi_w), wi_f.reshape(nb, ns, LANES)
        k3, v3, ki3 = k_bf.reshape(nb, ns, kv_w), v_bf.reshape(nb, ns, kv_w), ki2_bf.reshape(nb, ns, 2 * LANES)
        if si == 0:
            o = _dsa_prompt(q3, qi3, wi3, ki3, k3, v3, rel_bias)
        else:
            cki = cache_kidx_dsa[0].astype(BF16)
            zki = jnp.zeros_like(cki)
            ka = pad_keys(jnp.concatenate([cache_k_dsa[0].reshape(nb, past, kv_w).astype(BF16), k3], axis=1))
            va = pad_keys(jnp.concatenate([cache_v_dsa[0].reshape(nb, past, kv_w).astype(BF16), v3], axis=1))
            kia = pad_keys(jnp.concatenate([jnp.concatenate([cki, zki, zki, cki], axis=2), ki3], axis=1))
            pad_q = lambda a: jnp.pad(a, ((0, 0), (0, LANES - ns), (0, 0)))
            o = _dsa_attn(q3, pad_q(qi3), pad_q(wi3), kia, ka, va, bias_s, [nkb_s], var_s, past, l_s,
                          min(IDX_TOPK, l_s // 4), LANES, LANES)
        x = _out_proj(o.reshape(nb * ns, nq_w), xs[si], m[2], w_out1, tm)
        xs[si] = _moe(x, g_norm_ffn[1], m[3], m[4], m[5], w_router, router_bias, *moe_w1, g_final, True, tm_moe)

    h2 = (DIFF_HEADS, 2, DIFF_HEAD_DIM)
    hv = (DIFF_HEADS, 2 * DIFF_HEAD_DIM)
    hk = (DSA_KV_HEADS, DSA_HEAD_DIM)
    (kd_p, vd_p), (kd_s, vd_s), (ks_p, vs_p, ki_p), (ks_s, vs_s, ki_s) = outs_kv
    return (xs[0].reshape(b, s, d), xs[1].reshape(bs, ss, d),
            kd_p.reshape(1, b, s, *h2), vd_p.reshape(1, b, s, *hv),
            ks_p.reshape(1, b, s, *hk), vs_p.reshape(1, b, s, *hk), ki_p.reshape(1, b, s, IDX_DIM),
            kd_s.reshape(1, bs, ss, *h2), vd_s.reshape(1, bs, ss, *hv),
            ks_s.reshape(1, bs, ss, *hk), vs_s.reshape(1, bs, ss, *hk), ki_s.reshape(1, bs, ss, IDX_DIM))
```

```python
import functools

import jax
import jax.numpy as jnp
import numpy as np
from jax import lax
from jax.experimental import pallas as pl
from jax.experimental.pallas import tpu as pltpu

F32 = jnp.float32
BF16 = jnp.bfloat16
I32 = jnp.int32

D_MODEL = 1024
CHUNK = 64
DIFF_HEADS = 8
DIFF_HEAD_DIM = 64
DSA_HEADS = 16
DSA_KV_HEADS = 4
DSA_GROUP = DSA_HEADS // DSA_KV_HEADS
DSA_HEAD_DIM = 64
IDX_HEADS = 8
IDX_DIM = 64
IDX_TOPK = 256
REL_BUCKETS = 32
BIAS_HEADS = 16
N_EXPERTS = 16
N_GROUPS = 4
EXPERTS_PER_GROUP = N_EXPERTS // N_GROUPS
RMS_EPS = 1e-6

LANES = 128
NEG = -1e30
LOG2E = 1.4426950408889634
INT_MIN = -(2 ** 31)
NEG_INF_KEY = -0x7F800000
VMEM_LIMIT = 56 * 1024 * 1024

_BUCKET_STEPS = (12, 16, 23, 32, 46, 64, 91)
FAR_BLOCKS = 2

_NT = (((1,), (1,)), ((), ()))


def _pick_tile(n, prefs):
    for t in prefs:
        if n % t == 0:
            return t
    raise ValueError(f"no tile in {prefs} divides {n}")


def _rms(x, g):
    ms = jnp.mean(x * x, axis=-1, keepdims=True)
    return x * lax.rsqrt(ms + RMS_EPS) * g


def _per_segment(x, seg_vals, op):
    tm, d = x.shape
    nseg = tm // CHUNK
    return op(x.reshape(nseg, CHUNK, d), seg_vals[:, None, :]).reshape(tm, d)


def _norm_mod(x, g, shift, scale):
    y = _rms(x, g)
    y = _per_segment(y, scale, lambda a, s: a * (1.0 + s))
    return _per_segment(y, shift, lambda a, s: a + s)


def _ada_kernel(c_ref, w_ref, b_ref, o_ref):
    c = c_ref[...]
    a = (c * jax.nn.sigmoid(c)).astype(BF16)
    o_ref[0] = jnp.dot(a, w_ref[0].astype(BF16), preferred_element_type=F32) + b_ref[0]


def _ada_mod(c_all, w_ada, b_ada):
    depth, d, n6 = w_ada.shape
    bc = c_all.shape[0]
    tn = _pick_tile(n6, (1536, 1024, 512, 128))
    return pl.pallas_call(
        _ada_kernel,
        grid=(depth, n6 // tn),
        in_specs=[
            pl.BlockSpec((bc, d), lambda l, j: (0, 0)),
            pl.BlockSpec((1, d, tn), lambda l, j: (l, 0, j)),
            pl.BlockSpec((1, 1, tn), lambda l, j: (l, 0, j)),
        ],
        out_specs=pl.BlockSpec((1, bc, tn), lambda l, j: (l, 0, j)),
        out_shape=jax.ShapeDtypeStruct((depth, bc, n6), F32),
        compiler_params=pltpu.CompilerParams(vmem_limit_bytes=VMEM_LIMIT),
        name="ada_mod",
    )(c_all, w_ada, b_ada.reshape(depth, 1, n6))


def _bias_kernel(off_ref, q0_ref, len_ref, rb_ref, o_ref, *, tq, tk):
    v = pl.program_id(0)
    shape = (tq, tk)
    r = lax.broadcasted_iota(I32, shape, 0)
    c = lax.broadcasted_iota(I32, shape, 1)
    rel = off_ref[v] + c - r
    qpos = q0_ref[v] + r
    kpos = qpos + rel
    n = jnp.abs(rel)
    big = jnp.full(shape, 8, I32)
    for t in _BUCKET_STEPS:
        big = big + jnp.where(n >= t, 1, 0)
    bucket = jnp.where(n < 8, n, big) + jnp.where(rel > 0, REL_BUCKETS // 2, 0)
    vis = jnp.where((kpos >> 6) <= (qpos >> 6), 1, 0) * jnp.where(kpos < len_ref[0], 1, 0)
    for h in range(BIAS_HEADS):
        val = jnp.zeros(shape, F32)
        for b in range(REL_BUCKETS):
            val = jnp.where(bucket == b, rb_ref[b, h], val)
        o_ref[0, h] = jnp.where(vis > 0, val * LOG2E, NEG)


def _bias_tiles(rel_bias, offs, q0s, length, tq, tk):
    nvar = len(offs)
    return pl.pallas_call(
        functools.partial(_bias_kernel, tq=tq, tk=tk),
        grid_spec=pltpu.PrefetchScalarGridSpec(
            num_scalar_prefetch=3,
            grid=(nvar,),
            in_specs=[pl.BlockSpec(memory_space=pltpu.SMEM)],
            out_specs=pl.BlockSpec((1, BIAS_HEADS, tq, tk), lambda v, *_: (v, 0, 0, 0)),
        ),
        out_shape=jax.ShapeDtypeStruct((nvar, BIAS_HEADS, tq, tk), F32),
        name="bias_tiles",
    )(jnp.asarray(offs, I32), jnp.asarray(q0s, I32), jnp.asarray([length], I32), rel_bias)


def _inproj_kernel(x_ref, g_ref, shift_ref, scale_ref, w_ref, wt_ref, *out_refs, cols, t_cols):
    h = _norm_mod(x_ref[...], g_ref[...], shift_ref[0], scale_ref[0]).astype(BF16)
    for start, width, dests in cols:
        p = jnp.dot(h, w_ref[:, start:start + width], preferred_element_type=F32)
        for out_idx, out_off, mult in dests:
            o_ref = out_refs[out_idx]
            val = p if mult == 1.0 else p * mult
            o_ref[:, out_off:out_off + width] = val.astype(o_ref.dtype)
    t_refs = out_refs[len(out_refs) - len(t_cols):]
    for (start, width, _), t_ref in zip(t_cols, t_refs):
        t_ref[0] = lax.dot_general(wt_ref[start:start + width, :], h, _NT,
                                   preferred_element_type=F32).astype(t_ref.dtype)


def _in_proj(x, g, shift, scale, w_bf, cols, out_defs, tm, wt_bf=None, t_cols=(), seq=None):
    n, d = x.shape
    nt = n // tm
    nseg = tm // CHUNK
    nout = w_bf.shape[1]
    if wt_bf is None:
        wt_bf = jnp.zeros((16, d), BF16)
    seg_spec = pl.BlockSpec((1, nseg, d), lambda t: (t, 0, 0))
    tpb = (seq or n) // tm
    return pl.pallas_call(
        functools.partial(_inproj_kernel, cols=cols, t_cols=t_cols),
        grid=(nt,),
        in_specs=[
            pl.BlockSpec((tm, d), lambda t: (t, 0)),
            pl.BlockSpec((1, d), lambda t: (0, 0)),
            seg_spec, seg_spec,
            pl.BlockSpec((d, nout), lambda t: (0, 0)),
            pl.BlockSpec(wt_bf.shape, lambda t: (0, 0)),
        ],
        out_specs=([pl.BlockSpec((tm, w), lambda t: (t, 0)) for w, _ in out_defs]
                   + [pl.BlockSpec((1, w, tm), lambda t: (t // tpb, 0, t % tpb)) for _, w, _ in t_cols]),
        out_shape=([jax.ShapeDtypeStruct((n, w), dt) for w, dt in out_defs]
                   + [jax.ShapeDtypeStruct((n // (seq or n), w, seq or n), dt) for _, w, dt in t_cols]),
        compiler_params=pltpu.CompilerParams(vmem_limit_bytes=VMEM_LIMIT),
        name="in_proj",
    )(x, g.reshape(1, d), shift.reshape(nt, nseg, d), scale.reshape(nt, nseg, d), w_bf, wt_bf)


def _outproj_kernel(o_ref, x_ref, gate_ref, w_ref, out_ref):
    r = jnp.dot(o_ref[...], w_ref[...], preferred_element_type=F32)
    out_ref[...] = x_ref[...] + _per_segment(r, gate_ref[0], lambda a, s: a * s)


def _out_proj(o, x, gate, w_bf, tm):
    n, d = x.shape
    kdim = o.shape[1]
    nt = n // tm
    nseg = tm // CHUNK
    return pl.pallas_call(
        _outproj_kernel,
        grid=(nt,),
        in_specs=[
            pl.BlockSpec((tm, kdim), lambda t: (t, 0)),
            pl.BlockSpec((tm, d), lambda t: (t, 0)),
            pl.BlockSpec((1, nseg, d), lambda t: (t, 0, 0)),
            pl.BlockSpec((kdim, d), lambda t: (0, 0)),
        ],
        out_specs=pl.BlockSpec((tm, d), lambda t: (t, 0)),
        out_shape=jax.ShapeDtypeStruct((n, d), F32),
        compiler_params=pltpu.CompilerParams(vmem_limit_bytes=VMEM_LIMIT),
        name="out_proj",
    )(o, x, gate.reshape(nt, nseg, d), w_bf)


def _diff_attn_kernel(it_ref, jt_ref, var_ref, fin_ref, lam_ref, q_ref, k_ref, v_ref, bias_ref, g_ref,
                      o_ref, qb_s, p_s, al_s, m_s, acc_s, *, tq, tk, tb, rc, nkb, lam_init):
    t = pl.program_id(1)
    nchunk = 2 * tq // rc
    ncb = tk // LANES
    nrb = tq // tb
    bvar = [[var_ref[(it_ref[t] * nrb + rb) * nkb + jt_ref[t] * ncb + cb] for cb in range(ncb)]
            for rb in range(nrb)]

    @pl.when(jt_ref[t] == 0)
    def _():
        m_s[...] = jnp.full(m_s.shape, NEG, F32)
        acc_s[...] = jnp.zeros(acc_s.shape, F32)
        lane = lax.broadcasted_iota(I32, (tq, LANES), 1)
        for h in range(DIFF_HEADS):
            qh = q_ref[0, :, h * LANES:(h + 1) * LANES].astype(F32)
            qb_s[h, 0:tq, :] = jnp.where(lane < DIFF_HEAD_DIM, qh, 0.0).astype(BF16)
            qb_s[h, tq:2 * tq, :] = jnp.where(lane >= DIFF_HEAD_DIM, qh, 0.0).astype(BF16)

    def logits(h):
        return lax.dot_general(qb_s[h], k_ref[0, :, h * LANES:(h + 1) * LANES], _NT, preferred_element_type=F32)

    s_next = logits(0)
    for h in range(DIFF_HEADS):
        s_all = s_next
        if h + 1 < DIFF_HEADS:
            s_next = logits(h + 1)
        for c in range(nchunk):
            rows = slice(c * rc, (c + 1) * rc)
            qrow = (c * rc) % tq
            rb, brow = qrow // tb, qrow % tb
            hm = 2 * h + (c * rc) // tq
            s = jnp.concatenate(
                [s_all[rows, cb * LANES:(cb + 1) * LANES] + bias_ref[bvar[rb][cb], hm, brow:brow + rc, :]
                 for cb in range(ncb)], axis=1)
            m_prev = m_s[h, rows]
            m_new = jnp.maximum(m_prev, jnp.max(s, axis=1, keepdims=True))
            p = jnp.exp2(s - jnp.concatenate([m_new] * ncb, axis=1))
            p_s[h, rows] = p.astype(BF16)
            al_s[h, rows] = jnp.exp2(m_prev - m_new)
            m_s[h, rows] = m_new
        vh = v_ref[0, :, h * LANES:(h + 1) * LANES]
        v1 = jnp.concatenate([vh, jnp.ones_like(vh)], axis=1)
        alpha = al_s[h]
        acc_s[h] = (jnp.concatenate([alpha, alpha], axis=1) * acc_s[h]
                    + jnp.dot(p_s[h], v1, preferred_element_type=F32))

    @pl.when(fin_ref[t] == 1)
    def _():
        lp = lam_ref[...]
        lam = (jnp.exp(jnp.sum(lp[0:1] * lp[1:2], axis=1, keepdims=True))
               - jnp.exp(jnp.sum(lp[2:3] * lp[3:4], axis=1, keepdims=True)) + lam_init)
        g = g_ref[...]
        for h in range(DIFF_HEADS):
            o0 = acc_s[h, 0:tq, 0:LANES] / acc_s[h, 0:tq, LANES:2 * LANES]
            o1 = acc_s[h, tq:2 * tq, 0:LANES] / acc_s[h, tq:2 * tq, LANES:2 * LANES]
            o = _rms(o0 - lam * o1, g) * (1.0 - lam_init)
            o_ref[0, :, h * LANES:(h + 1) * LANES] = o.astype(o_ref.dtype)


def _diff_attn(q, k, v, bias, pairs, var, lam_p, subln_g, lam_init, tq, tk):
    b, sq, dq = q.shape
    nvar, tb = bias.shape[0], bias.shape[2]
    nkb = k.shape[1] // LANES
    it = np.asarray([p[0] for p in pairs], np.int32)
    jt = np.asarray([p[1] for p in pairs], np.int32)
    vt = np.asarray(var, np.int32).reshape(-1)
    fin = np.asarray([1 if (n + 1 == len(pairs) or pairs[n + 1][0] != p[0]) else 0
                      for n, p in enumerate(pairs)], np.int32)
    rc = min(tq, 64)
    return pl.pallas_call(
        functools.partial(_diff_attn_kernel, tq=tq, tk=tk, tb=tb, rc=rc, nkb=nkb, lam_init=lam_init),
        grid_spec=pltpu.PrefetchScalarGridSpec(
            num_scalar_prefetch=4,
            grid=(b, len(pairs)),
            in_specs=[
                pl.BlockSpec((4, DIFF_HEAD_DIM), lambda bb, t, *_: (0, 0)),
                pl.BlockSpec((1, tq, dq), lambda bb, t, it_, jt_, vt_, fin_: (bb, it_[t], 0)),
                pl.BlockSpec((1, tk, dq), lambda bb, t, it_, jt_, vt_, fin_: (bb, jt_[t], 0)),
                pl.BlockSpec((1, tk, dq), lambda bb, t, it_, jt_, vt_, fin_: (bb, jt_[t], 0)),
                pl.BlockSpec((nvar, BIAS_HEADS, tb, LANES), lambda bb, t, *_: (0, 0, 0, 0),
                             pipeline_mode=pl.Buffered(1)),
                pl.BlockSpec((1, 2 * DIFF_HEAD_DIM), lambda bb, t, *_: (0, 0)),
            ],
            out_specs=pl.BlockSpec((1, tq, dq), lambda bb, t, it_, jt_, vt_, fin_: (bb, it_[t], 0)),
            scratch_shapes=[
                pltpu.VMEM((DIFF_HEADS, 2 * tq, LANES), BF16),
                pltpu.VMEM((DIFF_HEADS, 2 * tq, tk), BF16),
                pltpu.VMEM((DIFF_HEADS, 2 * tq, LANES), F32),
                pltpu.VMEM((DIFF_HEADS, 2 * tq, LANES), F32),
                pltpu.VMEM((DIFF_HEADS, 2 * tq, 2 * LANES), F32),
            ],
        ),
        out_shape=jax.ShapeDtypeStruct((b, sq, dq), BF16),
        compiler_params=pltpu.CompilerParams(vmem_limit_bytes=VMEM_LIMIT),
        name="diff_attn",
    )(jnp.asarray(it), jnp.asarray(jt), jnp.asarray(vt), jnp.asarray(fin), lam_p, q, k, v, bias,
      subln_g.reshape(1, -1))


def _block_bias(rel_bias, nq, nkb, tq):
    bpq = tq // LANES
    lo, hi = -bpq, FAR_BLOCKS
    bias = _bias_tiles(rel_bias, [-LANES * d for d in range(lo, hi + 1)], [4 * tq] * (hi - lo + 1), 1 << 30, tq, LANES)
    var = [[min(max(bpq * i - jb, lo), hi) - lo for jb in range(nkb)] for i in range(nq)]
    return bias, var


def _diff_prompt(q, k, v, rel_bias, lam_p, subln_g, lam_init):
    s = q.shape[1]
    tq = tk = 512
    nq = s // tq
    pairs = [(i, j) for i in range(nq) for j in range((i * tq) // tk + 1)]
    bias, var = _block_bias(rel_bias, s // LANES, s // LANES, LANES)
    return _diff_attn(q, k, v, bias, pairs, var, lam_p, subln_g, lam_init, tq, tk)


def _sort_key(x):
    bits = pltpu.bitcast(x, I32)
    return jnp.where(bits < 0, INT_MIN - bits, bits)


def _dsa_kernel(nvis_ref, var_ref, q_ref, qi_ref, wi_ref, ki_ref, k_ref, v_ref, bias_ref, o_ref,
                keys_s, qs_s, p_s, al_s, m_s, acc_s, *, tq, tqa, tk, nk, rc, q0, length, topk, nbits):
    i = pl.program_id(1)
    nvis = nvis_ref[i]
    npair = DSA_KV_HEADS // 2
    hpp = 2 * DSA_GROUP
    nchunk = hpp * tqa // rc

    for hd in range(DSA_HEADS):
        qs_s[hd // hpp, (hd % hpp) * tqa:(hd % hpp + 1) * tqa, :] = q_ref[0, :, hd * LANES:(hd + 1) * LANES]
    qi_stack = jnp.concatenate([qi_ref[0, :, u * LANES:(u + 1) * LANES] for u in range(IDX_HEADS // 2)], axis=0)
    wi_t = wi_ref[0].T
    wrows = [wi_t[h:h + 1, :] for h in range(IDX_HEADS)]

    k_io = lax.broadcasted_iota(I32, (tk, tq), 0)
    q_chunk = (q0 + i * tq + lax.broadcasted_iota(I32, (tk, tq), 1)) >> 6
    score_scale = IDX_DIM ** -0.5 * IDX_HEADS ** -0.5

    def score_tile(j, last):
        off = pl.multiple_of(j * tk, tk)
        ki_lo = ki_ref[0, pl.ds(off, tk), 0:LANES]
        ki_hi = ki_ref[0, pl.ds(off, tk), LANES:2 * LANES]
        s_even = lax.dot_general(ki_lo, qi_stack, _NT, preferred_element_type=F32)
        s_odd = lax.dot_general(ki_hi, qi_stack, _NT, preferred_element_type=F32)
        score = jnp.zeros((tk, tq), F32)
        for u in range(IDX_HEADS // 2):
            score = score + jnp.maximum(s_even[:, u * tq:(u + 1) * tq], 0.0) * wrows[2 * u]
            score = score + jnp.maximum(s_odd[:, u * tq:(u + 1) * tq], 0.0) * wrows[2 * u + 1]
        key = _sort_key(score * score_scale)
        if last:
            kpos = j * tk + k_io
            vis = jnp.where((kpos >> 6) <= q_chunk, 1, 0) * jnp.where(kpos < length, 1, 0)
            key = jnp.where(vis > 0, key, INT_MIN)
        keys_s[j] = key

    def score_body(jj, carry):
        score_tile(2 * jj, False)
        score_tile(2 * jj + 1, False)
        return carry

    lax.fori_loop(0, (nvis - 1) // 2, score_body, 0)

    @pl.when(nvis % 2 == 0)
    def _():
        score_tile(nvis - 2, False)

    score_tile(nvis - 1, True)

    nacc = 4

    def column_count(hit_fn):
        def body(j, accs):
            kk = keys_s[j]
            accs = list(accs)
            for r in range(tk // 8):
                accs[r % nacc] = accs[r % nacc] + jnp.where(hit_fn(j, r, kk[8 * r:8 * r + 8]), 1.0, 0.0)
            return tuple(accs)
        accs = lax.fori_loop(0, nvis, body, (jnp.zeros((8, tq), F32),) * nacc)
        return jnp.sum(functools.reduce(lambda a, b: a + b, accs), axis=0, keepdims=True)

    def count_ge(cand):
        below = jnp.broadcast_to(cand - 1, (8, tq))
        return column_count(lambda j, r, kk: kk > below)

    def bit_body(it, ans):
        cand = ans | jnp.left_shift(jnp.int32(1), 31 - it)
        return jnp.where(count_ge(cand ^ INT_MIN) >= topk, cand, ans)

    thr = lax.fori_loop(0, 32, bit_body, jnp.zeros((1, tq), I32)) ^ INT_MIN
    thr_sel = jnp.maximum(thr, NEG_INF_KEY + 1)
    row_full = thr >= thr_sel

    cnt_ge = count_ge(thr_sel)
    cnt_gt = count_ge(thr_sel + 1)
    excess = jnp.max(jnp.where(row_full, cnt_ge - topk, 0.0))

    @pl.when(excess > 0.0)
    def _():
        quota = topk - cnt_gt
        thr8 = jnp.broadcast_to(thr, (8, tq))
        k_io8 = k_io[0:8]

        def count_tied_before(pos):
            pos8 = jnp.broadcast_to(pos, (8, tq))
            return column_count(lambda j, r, kk: jnp.where(kk == thr8, 1, 0)
                                * jnp.where(j * tk + 8 * r + k_io8 < pos8, 1, 0) > 0)

        def pos_body(it, p):
            cand = p | jnp.left_shift(jnp.int32(1), nbits - 1 - it)
            return jnp.where(count_tied_before(cand) < quota, cand, p)

        last = lax.fori_loop(0, nbits, pos_body, jnp.zeros((1, tq), I32))
        last = jnp.where(row_full, last, nk * tk)

        def demote(j, carry):
            kpos = j * tk + k_io
            kk = keys_s[j]
            drop = jnp.where(kk == thr, 1, 0) * jnp.where(kpos > last, 1, 0)
            keys_s[j] = jnp.where(drop > 0, thr - 1, kk)
            return carry

        lax.fori_loop(0, nvis, demote, 0)

    m_s[...] = jnp.full(m_s.shape, NEG, F32)
    acc_s[...] = jnp.zeros(acc_s.shape, F32)

    ncb = tk // LANES

    def attn_body(j, carry):
        off = pl.multiple_of(j * tk, tk)
        bvar = [var_ref[(i * nk + j) * ncb + cb] for cb in range(ncb)]
        maskb = jnp.where(keys_s[j] >= thr_sel, 0.0, NEG).T[0:tqa]
        s_pair = [lax.dot_general(qs_s[pr], k_ref[0, pl.ds(off, tk), pr * LANES:(pr + 1) * LANES], _NT,
                                  preferred_element_type=F32) for pr in range(npair)]
        for pr in range(npair):
            for c in range(nchunk):
                rows = slice(c * rc, (c + 1) * rc)
                hd = pr * hpp + (c * rc) // tqa
                brow = (c * rc) % tqa
                s = jnp.concatenate(
                    [s_pair[pr][rows, cb * LANES:(cb + 1) * LANES] + bias_ref[bvar[cb], hd, brow:brow + rc, :]
                     for cb in range(ncb)], axis=1) + maskb[brow:brow + rc]
                m_prev = m_s[pr, rows]
                m_new = jnp.maximum(m_prev, jnp.max(s, axis=1, keepdims=True))
                p = jnp.exp2(s - jnp.concatenate([m_new] * ncb, axis=1))
                p_s[pr, rows] = p.astype(BF16)
                al_s[pr, rows] = jnp.exp2(m_prev - m_new)
                m_s[pr, rows] = m_new
            vp = v_ref[0, pl.ds(off, tk), pr * LANES:(pr + 1) * LANES]
            v1 = jnp.concatenate([vp, jnp.ones_like(vp)], axis=1)
            alpha = al_s[pr]
            acc_s[pr] = (jnp.concatenate([alpha, alpha], axis=1) * acc_s[pr]
                         + jnp.dot(p_s[pr], v1, preferred_element_type=F32))
        return carry

    lax.fori_loop(0, nvis, attn_body, 0)

    for hd in range(DSA_HEADS):
        rows = slice((hd % hpp) * tqa, (hd % hpp + 1) * tqa)
        o = acc_s[hd // hpp, rows, 0:LANES] / acc_s[hd // hpp, rows, LANES:2 * LANES]
        o_ref[0, :, hd * LANES:(hd + 1) * LANES] = o.astype(o_ref.dtype)


def _dsa_attn(q, qi, wi, ki2, k, v, bias, nvis, var, q0, length, topk, tq, tk):
    b, sq, dq = q.shape
    lp = k.shape[1]
    nq = qi.shape[1] // tq
    tqa = sq // nq
    nk = lp // tk
    nvar = bias.shape[0]
    npair = DSA_KV_HEADS // 2
    prow = 2 * DSA_GROUP * tqa
    kernel = functools.partial(_dsa_kernel, tq=tq, tqa=tqa, tk=tk, nk=nk, rc=min(tqa, 64), q0=q0, length=length,
                               topk=topk, nbits=max(1, (lp - 1).bit_length()))
    once = dict(pipeline_mode=pl.Buffered(1))
    full = lambda bb, ii, *_: (bb, 0, 0)
    tile = lambda bb, ii, *_: (bb, ii, 0)
    return pl.pallas_call(
        kernel,
        grid_spec=pltpu.PrefetchScalarGridSpec(
            num_scalar_prefetch=2,
            grid=(b, nq),
            in_specs=[
                pl.BlockSpec((1, tqa, dq), tile),
                pl.BlockSpec((1, tq, qi.shape[2]), tile),
                pl.BlockSpec((1, tq, LANES), tile),
                pl.BlockSpec((1, lp, 2 * LANES), full, **once),
                pl.BlockSpec((1, lp, k.shape[2]), full, **once),
                pl.BlockSpec((1, lp, v.shape[2]), full, **once),
                pl.BlockSpec((nvar, BIAS_HEADS, tqa, LANES), lambda bb, ii, *_: (0, 0, 0, 0), **once),
            ],
            out_specs=pl.BlockSpec((1, tqa, dq), tile),
            scratch_shapes=[
                pltpu.VMEM((nk, tk, tq), I32),
                pltpu.VMEM((npair, prow, LANES), BF16),
                pltpu.VMEM((npair, prow, tk), BF16),
                pltpu.VMEM((npair, prow, LANES), F32),
                pltpu.VMEM((npair, prow, LANES), F32),
                pltpu.VMEM((npair, prow, 2 * LANES), F32),
            ],
        ),
        out_shape=jax.ShapeDtypeStruct((b, sq, dq), BF16),
        compiler_params=pltpu.CompilerParams(vmem_limit_bytes=VMEM_LIMIT),
        name="dsa_attn",
    )(jnp.asarray(nvis, I32), jnp.asarray(var, I32).reshape(-1), q, qi, wi, ki2, k, v, bias)


def _dsa_prompt(q, qi, wi, ki2, k, v, rel_bias):
    s = q.shape[1]
    tq, tk = 128, 512
    nq = s // tq
    nvis = [(i * tq) // tk + 1 for i in range(nq)]
    bias, var = _block_bias(rel_bias, nq, s // LANES, tq)
    return _dsa_attn(q, qi, wi, ki2, k, v, bias, nvis, var, 0, s, min(IDX_TOPK, s // 4), tq, tk)


def _route(lt_s, rb_ref, gt_s):
    rows = [lt_s[e:e + 1, :] for e in range(N_EXPERTS)]
    mx = functools.reduce(jnp.maximum, rows)
    ex = [jnp.exp(r - mx) for r in rows]
    z = functools.reduce(lambda a, b: a + b, ex)
    probs = [x / z for x in ex]
    sel = [probs[e] + rb_ref[e] for e in range(N_EXPERTS)]
    best_val, best_grp = None, None
    for gidx in range(N_GROUPS):
        a, b, c, d = sel[EXPERTS_PER_GROUP * gidx:EXPERTS_PER_GROUP * (gidx + 1)]
        hi1, lo1, hi2, lo2 = jnp.maximum(a, b), jnp.minimum(a, b), jnp.maximum(c, d), jnp.minimum(c, d)
        top2 = jnp.maximum(hi1, hi2) + jnp.maximum(jnp.minimum(hi1, hi2), jnp.maximum(lo1, lo2))
        if gidx == 0:
            best_val, best_grp = top2, jnp.zeros_like(top2, dtype=I32)
        else:
            better = top2 > best_val
            best_grp = jnp.where(better, gidx, best_grp)
            best_val = jnp.where(better, top2, best_val)
    masked = [jnp.where(best_grp == e // EXPERTS_PER_GROUP, sel[e], -jnp.inf) for e in range(N_EXPERTS)]

    def first_argmax(vals):
        bv, bi = vals[0], jnp.zeros_like(best_grp)
        for e in range(1, N_EXPERTS):
            better = vals[e] > bv
            bi = jnp.where(better, e, bi)
            bv = jnp.where(better, vals[e], bv)
        return bi

    i1 = first_argmax(masked)
    i2 = first_argmax([jnp.where(i1 == e, -jnp.inf, masked[e]) for e in range(N_EXPERTS)])
    w1 = functools.reduce(lambda a, b: a + b, [jnp.where(i1 == e, probs[e], 0.0) for e in range(N_EXPERTS)])
    w2 = functools.reduce(lambda a, b: a + b, [jnp.where(i2 == e, probs[e], 0.0) for e in range(N_EXPERTS)])
    wsum = w1 + w2
    w1, w2 = w1 / wsum, w2 / wsum
    for e in range(N_EXPERTS):
        gt_s[e:e + 1, :] = jnp.where(i1 == e, w1, 0.0) + jnp.where(i2 == e, w2, 0.0)


def _moe_kernel(rb_ref, x_ref, g_ref, shift_ref, scale_ref, gate_ref, wrh_ref, wrl_ref, wg_ref, wu_ref, wd_ref,
                gfin_ref, out_ref, hb_s, lt_s, gt_s, gates_s, acc_s, *, final_norm):
    e = pl.program_id(1)

    @pl.when(e == 0)
    def _():
        h = _norm_mod(x_ref[...], g_ref[...], shift_ref[0], scale_ref[0])
        hb = h.astype(BF16)
        hb_s[...] = hb
        hl = (h - hb.astype(F32)).astype(BF16)
        lt_s[...] = (lax.dot_general(wrh_ref[...], hb, _NT, preferred_element_type=F32)
                     + lax.dot_general(wrl_ref[...], hb, _NT, preferred_element_type=F32)
                     + lax.dot_general(wrh_ref[...], hl, _NT, preferred_element_type=F32))
        gt_s[...] = jnp.zeros(gt_s.shape, F32)
        _route(lt_s, rb_ref, gt_s)
        gates_s[...] = gt_s[...].T
        acc_s[...] = jnp.zeros(acc_s.shape, F32)

    hb = hb_s[...]
    up = jnp.dot(hb, wu_ref[0], preferred_element_type=F32)
    gt = jnp.dot(hb, wg_ref[0], preferred_element_type=F32)
    lane = lax.broadcasted_iota(I32, gates_s.shape, 1)
    gcol = jnp.sum(jnp.where(lane == e, gates_s[...], 0.0), axis=1, keepdims=True)
    a = (gt * jax.nn.sigmoid(gt) * up * gcol).astype(BF16)
    acc_s[...] += jnp.dot(a, wd_ref[0], preferred_element_type=F32)

    @pl.when(e == N_EXPERTS - 1)
    def _():
        y = x_ref[...] + _per_segment(acc_s[...], gate_ref[0], lambda v, s: v * s)
        if final_norm:
            y = _rms(y, gfin_ref[...])
        out_ref[...] = y


def _moe(x, g, shift, scale, gate, w_router, router_bias, wg_bf, wu_bf, wd_bf, g_final, final_norm, tm):
    n, d = x.shape
    nt = n // tm
    nseg = tm // CHUNK
    ff = wg_bf.shape[2]
    wrt = w_router.T
    wrh = wrt.astype(BF16)
    wrl = (wrt - wrh.astype(F32)).astype(BF16)
    seg_spec = pl.BlockSpec((1, nseg, d), lambda t, e: (t, 0, 0))
    row_spec = pl.BlockSpec((1, d), lambda t, e: (0, 0))
    return pl.pallas_call(
        functools.partial(_moe_kernel, final_norm=final_norm),
        grid=(nt, N_EXPERTS),
        in_specs=[
            pl.BlockSpec(memory_space=pltpu.SMEM),
            pl.BlockSpec((tm, d), lambda t, e: (t, 0)),
            row_spec, seg_spec, seg_spec, seg_spec,
            pl.BlockSpec((N_EXPERTS, d), lambda t, e: (0, 0)),
            pl.BlockSpec((N_EXPERTS, d), lambda t, e: (0, 0)),
            pl.BlockSpec((1, d, ff), lambda t, e: (e, 0, 0)),
            pl.BlockSpec((1, d, ff), lambda t, e: (e, 0, 0)),
            pl.BlockSpec((1, ff, d), lambda t, e: (e, 0, 0)),
            row_spec,
        ],
        out_specs=pl.BlockSpec((tm, d), lambda t, e: (t, 0)),
        out_shape=jax.ShapeDtypeStruct((n, d), F32),
        scratch_shapes=[
            pltpu.VMEM((tm, d), BF16),
            pltpu.VMEM((N_EXPERTS, tm), F32),
            pltpu.VMEM((LANES, tm), F32),
            pltpu.VMEM((tm, LANES), F32),
            pltpu.VMEM((tm, d), F32),
        ],
        compiler_params=pltpu.CompilerParams(vmem_limit_bytes=VMEM_LIMIT),
        name="moe",
    )(router_bias, x, g.reshape(1, d), shift.reshape(nt, nseg, d), scale.reshape(nt, nseg, d),
      gate.reshape(nt, nseg, d), wrh, wrl, wg_bf, wu_bf, wd_bf, g_final.reshape(1, d))


def _dsa_in_weights(w):
    d = w.shape[0]
    o1 = DSA_HEADS * DSA_HEAD_DIM
    o2 = o1 + DSA_KV_HEADS * DSA_HEAD_DIM
    o3 = o2 + DSA_KV_HEADS * DSA_HEAD_DIM
    o4 = o3 + IDX_HEADS * IDX_DIM
    o5 = o4 + IDX_DIM
    wq = w[:, :o1].reshape(d, DSA_HEADS, 1, DSA_HEAD_DIM)
    half = ((np.arange(DSA_HEADS) // DSA_GROUP) % 2).reshape(1, DSA_HEADS, 1, 1)
    wq = jnp.concatenate([jnp.where(half == 0, wq, 0.0), jnp.where(half == 1, wq, 0.0)], axis=2)
    wki = w[:, o4:o5]
    zk = jnp.zeros_like(wki)
    wwi = jnp.pad(w[:, o5:], ((0, 0), (0, LANES - IDX_HEADS)))
    return jnp.concatenate([wq.reshape(d, 2 * o1), w[:, o1:o4], wki, zk, zk, wki, wwi], axis=1).astype(BF16)


def _dsa_out_weights(w):
    d = w.shape[1]
    w4 = w.reshape(DSA_HEADS, 1, DSA_HEAD_DIM, d)
    half = ((np.arange(DSA_HEADS) // DSA_GROUP) % 2).reshape(DSA_HEADS, 1, 1, 1)
    w4 = jnp.concatenate([jnp.where(half == 0, w4, 0.0), jnp.where(half == 1, w4, 0.0)], axis=1)
    return w4.reshape(2 * DSA_HEADS * DSA_HEAD_DIM, d).astype(BF16)


def kernel(x_prompt, x_sample, cache_k_diff, cache_v_diff, cache_k_dsa, cache_v_dsa, cache_kidx_dsa,
           c_prompt, c_sample, rel_bias, w_in_diff, w_out_diff, lam_diff, subln_g_diff, w_in_dsa, w_out_dsa,
           w_ada, b_ada, g_norm_mix, g_norm_ffn, w_router, router_bias, w_gate, w_up, w_down, g_final):
    b, s, d = x_prompt.shape
    bs, ss, _ = x_sample.shape
    past = cache_k_diff.shape[2]
    depth = w_ada.shape[0]
    assert d == D_MODEL and ss == CHUNK and s % 512 == 0 and past % CHUNK == 0 and depth == 2
    l_s = past + ss
    lp_s = -(-l_s // LANES) * LANES
    nkb_s = lp_s // LANES

    streams = []
    for x3, rows in ((x_prompt, np.repeat(np.arange(b), s // CHUNK)), (x_sample, b + np.arange(bs))):
        n = x3.shape[0] * x3.shape[1]
        streams.append((x3.reshape(n, d), x3.shape[:2], rows,
                        _pick_tile(n, (512, 256, 128, 64)), _pick_tile(n, (1024, 512, 256, 128, 64))))
    mod = _ada_mod(jnp.concatenate([c_prompt, c_sample], axis=0), w_ada, b_ada)

    def mods(layer, rows):
        m = mod[layer][rows]
        return [m[:, i * d:(i + 1) * d] for i in range(6)]

    def pad_keys(x):
        return jnp.pad(x, ((0, 0), (0, lp_s - x.shape[1]), (0, 0)))

    bias_s = _bias_tiles(rel_bias, [jb * LANES - past for jb in range(nkb_s)], [past] * nkb_s, l_s, ss, LANES)
    var_s = [list(range(nkb_s))]
    xs = [st[0] for st in streams]
    outs_kv = []

    qk = DIFF_HEADS * 2 * DIFF_HEAD_DIM
    lam_init = 0.8 - 0.6 * float(np.exp(-0.3 * 0))
    q_scale = DIFF_HEAD_DIM ** -0.5 * LOG2E
    w_in0, w_out0 = w_in_diff[0].astype(BF16), w_out_diff[0].astype(BF16)
    wt_k0 = w_in_diff[0][:, qk:2 * qk].T.astype(BF16)
    moe_w0 = (w_gate[0].astype(BF16), w_up[0].astype(BF16), w_down[0].astype(BF16))
    for si, (_, (nb, ns), rows, tm, tm_moe) in enumerate(streams):
        m = mods(0, rows)
        if si == 0:
            cols = ((0, qk, ((0, 0, q_scale),)), (qk, qk, ((2, 0, 1.0),)), (2 * qk, qk, ((1, 0, 1.0), (3, 0, 1.0))))
            outs = ((qk, BF16), (qk, F32), (qk, BF16), (qk, BF16))
            q_bf, v_f, k_bf, v_bf, k_t = _in_proj(xs[si], g_norm_mix[0], m[0], m[1], w_in0, cols, outs, tm,
                                                  wt_k0, ((0, qk, F32),), ns)
            k_f = jnp.transpose(k_t.reshape(nb, DIFF_HEADS, 2, DIFF_HEAD_DIM, ns), (0, 4, 1, 2, 3))
        else:
            cols = ((0, qk, ((0, 0, q_scale),)), (qk, qk, ((1, 0, 1.0), (3, 0, 1.0))),
                    (2 * qk, qk, ((2, 0, 1.0), (4, 0, 1.0))))
            outs = ((qk, BF16), (qk, F32), (qk, F32), (qk, BF16), (qk, BF16))
            q_bf, k_f, v_f, k_bf, v_bf = _in_proj(xs[si], g_norm_mix[0], m[0], m[1], w_in0, cols, outs, tm)
        outs_kv.append((k_f, v_f))
        q3, k3, v3 = (a.reshape(nb, ns, qk) for a in (q_bf, k_bf, v_bf))
        if si == 0:
            o = _diff_prompt(q3, k3, v3, rel_bias, lam_diff[0], subln_g_diff[0], lam_init)
        else:
            ka = pad_keys(jnp.concatenate([cache_k_diff[0].reshape(nb, past, qk).astype(BF16), k3], axis=1))
            va = pad_keys(jnp.concatenate([cache_v_diff[0].reshape(nb, past, qk).astype(BF16), v3], axis=1))
            o = _diff_attn(q3, ka, va, bias_s, [(0, 0)], var_s, lam_diff[0], subln_g_diff[0], lam_init, ns, lp_s)
        x = _out_proj(o.reshape(nb * ns, qk), xs[si], m[2], w_out0, tm)
        xs[si] = _moe(x, g_norm_ffn[0], m[3], m[4], m[5], w_router, router_bias, *moe_w0, g_final, False, tm_moe)

    nq_w = 2 * DSA_HEADS * DSA_HEAD_DIM
    kv_w = DSA_KV_HEADS * DSA_HEAD_DIM
    qi_w = IDX_HEADS * IDX_DIM
    c_k, c_v, c_qi = nq_w, nq_w + kv_w, nq_w + 2 * kv_w
    c_ki, c_wi = c_qi + qi_w, c_qi + qi_w + 2 * LANES
    half_q = nq_w // 2
    q_scale = DSA_HEAD_DIM ** -0.5 * LOG2E
    cols_q = ((0, half_q, ((0, 0, q_scale),)), (half_q, half_q, ((0, half_q, q_scale),)))
    w_in1, w_out1 = _dsa_in_weights(w_in_dsa[0]), _dsa_out_weights(w_out_dsa[0])
    o_k = DSA_HEADS * DSA_HEAD_DIM
    o_v, o_ki = o_k + kv_w, o_k + 2 * kv_w + qi_w
    wt_1 = jnp.concatenate([w_in_dsa[0][:, o_k:o_v], w_in_dsa[0][:, o_v:o_v + kv_w],
                            w_in_dsa[0][:, o_ki:o_ki + IDX_DIM]], axis=1).T.astype(BF16)
    moe_w1 = (w_gate[1].astype(BF16), w_up[1].astype(BF16), w_down[1].astype(BF16))
    for si, (_, (nb, ns), rows, tm, tm_moe) in enumerate(streams):
        m = mods(1, rows)
        if si == 0:
            cols = cols_q + ((c_k, kv_w, ((1, 0, 1.0),)), (c_v, kv_w, ((2, 0, 1.0),)), (c_qi, qi_w, ((3, 0, 1.0),)),
                             (c_ki, 2 * LANES, ((4, 0, 1.0),)), (c_wi, LANES, ((5, 0, 1.0),)))
            outs = ((nq_w, BF16), (kv_w, BF16), (kv_w, BF16), (qi_w, BF16), (2 * LANES, BF16), (LANES, F32))
            q2, k_bf, v_bf, qi_bf, ki2_bf, wi_f, k_t, v_t, ki_t = _in_proj(
                xs[si], g_norm_mix[1], m[0], m[1], w_in1, cols, outs, tm, wt_1,
                ((0, kv_w, F32), (kv_w, kv_w, F32), (2 * kv_w, IDX_DIM, F32)), ns)
            k_f = jnp.transpose(k_t.reshape(nb, DSA_KV_HEADS, DSA_HEAD_DIM, ns), (0, 3, 1, 2))
            v_f = jnp.transpose(v_t.reshape(nb, DSA_KV_HEADS, DSA_HEAD_DIM, ns), (0, 3, 1, 2))
            ki_f = jnp.transpose(ki_t, (0, 2, 1))
        else:
            cols = cols_q + ((c_k, kv_w, ((1, 0, 1.0), (3, 0, 1.0))), (c_v, kv_w, ((2, 0, 1.0), (4, 0, 1.0))),
                             (c_qi, qi_w, ((5, 0, 1.0),)), (c_ki, 2 * LANES, ((7, 0, 1.0),)),
                             (c_ki, IDX_DIM, ((6, 0, 1.0),)), (c_wi, LANES, ((8, 0, 1.0),)))
            outs = ((nq_w, BF16), (kv_w, F32), (kv_w, F32), (kv_w, BF16), (kv_w, BF16), (qi_w, BF16),
                    (IDX_DIM, F32), (2 * LANES, BF16), (LANES, F32))
            q2, k_f, v_f, k_bf, v_bf, qi_bf, ki_f, ki2_bf, wi_f = _in_proj(
                xs[si], g_norm_mix[1], m[0], m[1], w_in1, cols, outs, tm)
        outs_kv.append((k_f, v_f, ki_f))
        q3, qi3, wi3 = q2.reshape(nb, ns, nq_w), qi_bf.reshape(nb, ns, qi_w), wi_f.reshape(nb, ns, LANES)
        k3, v3, ki3 = k_bf.reshape(nb, ns, kv_w), v_bf.reshape(nb, ns, kv_w), ki2_bf.reshape(nb, ns, 2 * LANES)
        if si == 0:
            o = _dsa_prompt(q3, qi3, wi3, ki3, k3, v3, rel_bias)
        else:
            cki = cache_kidx_dsa[0].astype(BF16)
            zki = jnp.zeros_like(cki)
            ka = pad_keys(jnp.concatenate([cache_k_dsa[0].reshape(nb, past, kv_w).astype(BF16), k3], axis=1))
            va = pad_keys(jnp.concatenate([cache_v_dsa[0].reshape(nb, past, kv_w).astype(BF16), v3], axis=1))
            kia = pad_keys(jnp.concatenate([jnp.concatenate([cki, zki, zki, cki], axis=2), ki3], axis=1))
            pad_q = lambda a: jnp.pad(a, ((0, 0), (0, LANES - ns), (0, 0)))
            o = _dsa_attn(q3, pad_q(qi3), pad_q(wi3), kia, ka, va, bias_s, [nkb_s], var_s, past, l_s,
                          min(IDX_TOPK, l_s // 4), LANES, LANES)
        x = _out_proj(o.reshape(nb * ns, nq_w), xs[si], m[2], w_out1, tm)
        xs[si] = _moe(x, g_norm_ffn[1], m[3], m[4], m[5], w_router, router_bias, *moe_w1, g_final, True, tm_moe)

    h2 = (DIFF_HEADS, 2, DIFF_HEAD_DIM)
    hv = (DIFF_HEADS, 2 * DIFF_HEAD_DIM)
    hk = (DSA_KV_HEADS, DSA_HEAD_DIM)
    (kd_p, vd_p), (kd_s, vd_s), (ks_p, vs_p, ki_p), (ks_s, vs_s, ki_s) = outs_kv
    return (xs[0].reshape(b, s, d), xs[1].reshape(bs, ss, d),
            kd_p.reshape(1, b, s, *h2), vd_p.reshape(1, b, s, *hv),
            ks_p.reshape(1, b, s, *hk), vs_p.reshape(1, b, s, *hk), ki_p.reshape(1, b, s, IDX_DIM),
            kd_s.reshape(1, bs, ss, *h2), vd_s.reshape(1, bs, ss, *hv),
            ks_s.reshape(1, bs, ss, *hk), vs_s.reshape(1, bs, ss, *hk), ki_s.reshape(1, bs, ss, IDX_DIM))
```

```python
import functools

import jax
import jax.numpy as jnp
import numpy as np
from jax import lax
from jax.experimental import pallas as pl
from jax.experimental.pallas import tpu as pltpu

F32 = jnp.float32
BF16 = jnp.bfloat16
I32 = jnp.int32

D_MODEL = 1024
CHUNK = 64
DIFF_HEADS = 8
DIFF_HEAD_DIM = 64
DSA_HEADS = 16
DSA_KV_HEADS = 4
DSA_GROUP = DSA_HEADS // DSA_KV_HEADS
DSA_HEAD_DIM = 64
IDX_HEADS = 8
IDX_DIM = 64
IDX_TOPK = 256
REL_BUCKETS = 32
BIAS_HEADS = 16
N_EXPERTS = 16
N_GROUPS = 4
EXPERTS_PER_GROUP = N_EXPERTS // N_GROUPS
RMS_EPS = 1e-6

LANES = 128
NEG = -1e30
LOG2E = 1.4426950408889634
INT_MIN = -(2 ** 31)
NEG_INF_KEY = -0x7F800000
VMEM_LIMIT = 56 * 1024 * 1024

_BUCKET_STEPS = (12, 16, 23, 32, 46, 64, 91)
FAR_BLOCKS = 2

_NT = (((1,), (1,)), ((), ()))


def _pick_tile(n, prefs):
    for t in prefs:
        if n % t == 0:
            return t
    raise ValueError(f"no tile in {prefs} divides {n}")


def _rms(x, g):
    ms = jnp.mean(x * x, axis=-1, keepdims=True)
    return x * lax.rsqrt(ms + RMS_EPS) * g


def _per_segment(x, seg_vals, op):
    tm, d = x.shape
    nseg = tm // CHUNK
    return op(x.reshape(nseg, CHUNK, d), seg_vals[:, None, :]).reshape(tm, d)


def _norm_mod(x, g, shift, scale):
    y = _rms(x, g)
    y = _per_segment(y, scale, lambda a, s: a * (1.0 + s))
    return _per_segment(y, shift, lambda a, s: a + s)


def _ada_kernel(c_ref, w_ref, b_ref, o_ref):
    c = c_ref[...]
    a = (c * jax.nn.sigmoid(c)).astype(BF16)
    o_ref[0] = jnp.dot(a, w_ref[0].astype(BF16), preferred_element_type=F32) + b_ref[0]


def _ada_mod(c_all, w_ada, b_ada):
    depth, d, n6 = w_ada.shape
    bc = c_all.shape[0]
    tn = _pick_tile(n6, (1536, 1024, 512, 128))
    return pl.pallas_call(
        _ada_kernel,
        grid=(depth, n6 // tn),
        in_specs=[
            pl.BlockSpec((bc, d), lambda l, j: (0, 0)),
            pl.BlockSpec((1, d, tn), lambda l, j: (l, 0, j)),
            pl.BlockSpec((1, 1, tn), lambda l, j: (l, 0, j)),
        ],
        out_specs=pl.BlockSpec((1, bc, tn), lambda l, j: (l, 0, j)),
        out_shape=jax.ShapeDtypeStruct((depth, bc, n6), F32),
        compiler_params=pltpu.CompilerParams(vmem_limit_bytes=VMEM_LIMIT),
        name="ada_mod",
    )(c_all, w_ada, b_ada.reshape(depth, 1, n6))


def _bias_kernel(off_ref, q0_ref, len_ref, rb_ref, o_ref, *, tq, tk):
    v = pl.program_id(0)
    shape = (tq, tk)
    r = lax.broadcasted_iota(I32, shape, 0)
    c = lax.broadcasted_iota(I32, shape, 1)
    rel = off_ref[v] + c - r
    qpos = q0_ref[v] + r
    kpos = qpos + rel
    n = jnp.abs(rel)
    big = jnp.full(shape, 8, I32)
    for t in _BUCKET_STEPS:
        big = big + jnp.where(n >= t, 1, 0)
    bucket = jnp.where(n < 8, n, big) + jnp.where(rel > 0, REL_BUCKETS // 2, 0)
    vis = jnp.where((kpos >> 6) <= (qpos >> 6), 1, 0) * jnp.where(kpos < len_ref[0], 1, 0)
    for h in range(BIAS_HEADS):
        val = jnp.zeros(shape, F32)
        for b in range(REL_BUCKETS):
            val = jnp.where(bucket == b, rb_ref[b, h], val)
        o_ref[0, h] = jnp.where(vis > 0, val * LOG2E, NEG)


def _bias_tiles(rel_bias, offs, q0s, length, tq, tk):
    nvar = len(offs)
    return pl.pallas_call(
        functools.partial(_bias_kernel, tq=tq, tk=tk),
        grid_spec=pltpu.PrefetchScalarGridSpec(
            num_scalar_prefetch=3,
            grid=(nvar,),
            in_specs=[pl.BlockSpec(memory_space=pltpu.SMEM)],
            out_specs=pl.BlockSpec((1, BIAS_HEADS, tq, tk), lambda v, *_: (v, 0, 0, 0)),
        ),
        out_shape=jax.ShapeDtypeStruct((nvar, BIAS_HEADS, tq, tk), F32),
        name="bias_tiles",
    )(jnp.asarray(offs, I32), jnp.asarray(q0s, I32), jnp.asarray([length], I32), rel_bias)


def _inproj_kernel(x_ref, g_ref, shift_ref, scale_ref, w_ref, wt_ref, *out_refs, cols, t_cols):
    h = _norm_mod(x_ref[...], g_ref[...], shift_ref[0], scale_ref[0]).astype(BF16)
    for start, width, dests in cols:
        p = jnp.dot(h, w_ref[:, start:start + width], preferred_element_type=F32)
        for out_idx, out_off, mult in dests:
            o_ref = out_refs[out_idx]
            val = p if mult == 1.0 else p * mult
            o_ref[:, out_off:out_off + width] = val.astype(o_ref.dtype)
    t_refs = out_refs[len(out_refs) - len(t_cols):]
    for (start, width, _), t_ref in zip(t_cols, t_refs):
        t_ref[0] = lax.dot_general(wt_ref[start:start + width, :], h, _NT,
                                   preferred_element_type=F32).astype(t_ref.dtype)


def _in_proj(x, g, shift, scale, w_bf, cols, out_defs, tm, wt_bf=None, t_cols=(), seq=None):
    n, d = x.shape
    nt = n // tm
    nseg = tm // CHUNK
    nout = w_bf.shape[1]
    if wt_bf is None:
        wt_bf = jnp.zeros((16, d), BF16)
    seg_spec = pl.BlockSpec((1, nseg, d), lambda t: (t, 0, 0))
    tpb = (seq or n) // tm
    return pl.pallas_call(
        functools.partial(_inproj_kernel, cols=cols, t_cols=t_cols),
        grid=(nt,),
        in_specs=[
            pl.BlockSpec((tm, d), lambda t: (t, 0)),
            pl.BlockSpec((1, d), lambda t: (0, 0)),
            seg_spec, seg_spec,
            pl.BlockSpec((d, nout), lambda t: (0, 0)),
            pl.BlockSpec(wt_bf.shape, lambda t: (0, 0)),
        ],
        out_specs=([pl.BlockSpec((tm, w), lambda t: (t, 0)) for w, _ in out_defs]
                   + [pl.BlockSpec((1, w, tm), lambda t: (t // tpb, 0, t % tpb)) for _, w, _ in t_cols]),
        out_shape=([jax.ShapeDtypeStruct((n, w), dt) for w, dt in out_defs]
                   + [jax.ShapeDtypeStruct((n // (seq or n), w, seq or n), dt) for _, w, dt in t_cols]),
        compiler_params=pltpu.CompilerParams(vmem_limit_bytes=VMEM_LIMIT),
        name="in_proj",
    )(x, g.reshape(1, d), shift.reshape(nt, nseg, d), scale.reshape(nt, nseg, d), w_bf, wt_bf)


def _outproj_kernel(o_ref, x_ref, gate_ref, w_ref, out_ref):
    r = jnp.dot(o_ref[...], w_ref[...], preferred_element_type=F32)
    out_ref[...] = x_ref[...] + _per_segment(r, gate_ref[0], lambda a, s: a * s)


def _out_proj(o, x, gate, w_bf, tm):
    n, d = x.shape
    kdim = o.shape[1]
    nt = n // tm
    nseg = tm // CHUNK
    return pl.pallas_call(
        _outproj_kernel,
        grid=(nt,),
        in_specs=[
            pl.BlockSpec((tm, kdim), lambda t: (t, 0)),
            pl.BlockSpec((tm, d), lambda t: (t, 0)),
            pl.BlockSpec((1, nseg, d), lambda t: (t, 0, 0)),
            pl.BlockSpec((kdim, d), lambda t: (0, 0)),
        ],
        out_specs=pl.BlockSpec((tm, d), lambda t: (t, 0)),
        out_shape=jax.ShapeDtypeStruct((n, d), F32),
        compiler_params=pltpu.CompilerParams(vmem_limit_bytes=VMEM_LIMIT),
        name="out_proj",
    )(o, x, gate.reshape(nt, nseg, d), w_bf)


def _diff_attn_kernel(it_ref, jt_ref, var_ref, fin_ref, lam_ref, q_ref, k_ref, v_ref, bias_ref, g_ref,
                      o_ref, qb_s, p_s, al_s, m_s, acc_s, *, tq, tk, tb, rc, nkb, lam_init):
    t = pl.program_id(1)
    nchunk = 2 * tq // rc
    ncb = tk // LANES
    nrb = tq // tb
    bvar = [[var_ref[(it_ref[t] * nrb + rb) * nkb + jt_ref[t] * ncb + cb] for cb in range(ncb)]
            for rb in range(nrb)]

    @pl.when(jt_ref[t] == 0)
    def _():
        m_s[...] = jnp.full(m_s.shape, NEG, F32)
        acc_s[...] = jnp.zeros(acc_s.shape, F32)
        lane = lax.broadcasted_iota(I32, (tq, LANES), 1)
        for h in range(DIFF_HEADS):
            qh = q_ref[0, :, h * LANES:(h + 1) * LANES].astype(F32)
            qb_s[h, 0:tq, :] = jnp.where(lane < DIFF_HEAD_DIM, qh, 0.0).astype(BF16)
            qb_s[h, tq:2 * tq, :] = jnp.where(lane >= DIFF_HEAD_DIM, qh, 0.0).astype(BF16)

    def logits(h):
        return lax.dot_general(qb_s[h], k_ref[0, :, h * LANES:(h + 1) * LANES], _NT, preferred_element_type=F32)

    s_next = logits(0)
    for h in range(DIFF_HEADS):
        s_all = s_next
        if h + 1 < DIFF_HEADS:
            s_next = logits(h + 1)
        for c in range(nchunk):
            rows = slice(c * rc, (c + 1) * rc)
            qrow = (c * rc) % tq
            rb, brow = qrow // tb, qrow % tb
            hm = 2 * h + (c * rc) // tq
            s = jnp.concatenate(
                [s_all[rows, cb * LANES:(cb + 1) * LANES] + bias_ref[bvar[rb][cb], hm, brow:brow + rc, :]
                 for cb in range(ncb)], axis=1)
            m_prev = m_s[h, rows]
            m_new = jnp.maximum(m_prev, jnp.max(s, axis=1, keepdims=True))
            p = jnp.exp2(s - jnp.concatenate([m_new] * ncb, axis=1))
            p_s[h, rows] = p.astype(BF16)
            al_s[h, rows] = jnp.exp2(m_prev - m_new)
            m_s[h, rows] = m_new
        vh = v_ref[0, :, h * LANES:(h + 1) * LANES]
        v1 = jnp.concatenate([vh, jnp.ones_like(vh)], axis=1)
        alpha = al_s[h]
        acc_s[h] = (jnp.concatenate([alpha, alpha], axis=1) * acc_s[h]
                    + jnp.dot(p_s[h], v1, preferred_element_type=F32))

    @pl.when(fin_ref[t] == 1)
    def _():
        lp = lam_ref[...]
        lam = (jnp.exp(jnp.sum(lp[0:1] * lp[1:2], axis=1, keepdims=True))
               - jnp.exp(jnp.sum(lp[2:3] * lp[3:4], axis=1, keepdims=True)) + lam_init)
        g = g_ref[...]
        for h in range(DIFF_HEADS):
            o0 = acc_s[h, 0:tq, 0:LANES] / acc_s[h, 0:tq, LANES:2 * LANES]
            o1 = acc_s[h, tq:2 * tq, 0:LANES] / acc_s[h, tq:2 * tq, LANES:2 * LANES]
            o = _rms(o0 - lam * o1, g) * (1.0 - lam_init)
            o_ref[0, :, h * LANES:(h + 1) * LANES] = o.astype(o_ref.dtype)


def _diff_attn(q, k, v, bias, pairs, var, lam_p, subln_g, lam_init, tq, tk):
    b, sq, dq = q.shape
    nvar, tb = bias.shape[0], bias.shape[2]
    nkb = k.shape[1] // LANES
    it = np.asarray([p[0] for p in pairs], np.int32)
    jt = np.asarray([p[1] for p in pairs], np.int32)
    vt = np.asarray(var, np.int32).reshape(-1)
    fin = np.asarray([1 if (n + 1 == len(pairs) or pairs[n + 1][0] != p[0]) else 0
                      for n, p in enumerate(pairs)], np.int32)
    rc = min(tq, 64)
    return pl.pallas_call(
        functools.partial(_diff_attn_kernel, tq=tq, tk=tk, tb=tb, rc=rc, nkb=nkb, lam_init=lam_init),
        grid_spec=pltpu.PrefetchScalarGridSpec(
            num_scalar_prefetch=4,
            grid=(b, len(pairs)),
            in_specs=[
                pl.BlockSpec((4, DIFF_HEAD_DIM), lambda bb, t, *_: (0, 0)),
                pl.BlockSpec((1, tq, dq), lambda bb, t, it_, jt_, vt_, fin_: (bb, it_[t], 0)),
                pl.BlockSpec((1, tk, dq), lambda bb, t, it_, jt_, vt_, fin_: (bb, jt_[t], 0)),
                pl.BlockSpec((1, tk, dq), lambda bb, t, it_, jt_, vt_, fin_: (bb, jt_[t], 0)),
                pl.BlockSpec((nvar, BIAS_HEADS, tb, LANES), lambda bb, t, *_: (0, 0, 0, 0),
                             pipeline_mode=pl.Buffered(1)),
                pl.BlockSpec((1, 2 * DIFF_HEAD_DIM), lambda bb, t, *_: (0, 0)),
            ],
            out_specs=pl.BlockSpec((1, tq, dq), lambda bb, t, it_, jt_, vt_, fin_: (bb, it_[t], 0)),
            scratch_shapes=[
                pltpu.VMEM((DIFF_HEADS, 2 * tq, LANES), BF16),
                pltpu.VMEM((DIFF_HEADS, 2 * tq, tk), BF16),
                pltpu.VMEM((DIFF_HEADS, 2 * tq, LANES), F32),
                pltpu.VMEM((DIFF_HEADS, 2 * tq, LANES), F32),
                pltpu.VMEM((DIFF_HEADS, 2 * tq, 2 * LANES), F32),
            ],
        ),
        out_shape=jax.ShapeDtypeStruct((b, sq, dq), BF16),
        compiler_params=pltpu.CompilerParams(vmem_limit_bytes=VMEM_LIMIT),
        name="diff_attn",
    )(jnp.asarray(it), jnp.asarray(jt), jnp.asarray(vt), jnp.asarray(fin), lam_p, q, k, v, bias,
      subln_g.reshape(1, -1))


def _block_bias(rel_bias, nq, nkb, tq):
    bpq = tq // LANES
    lo, hi = -bpq, FAR_BLOCKS
    bias = _bias_tiles(rel_bias, [-LANES * d for d in range(lo, hi + 1)], [4 * tq] * (hi - lo + 1), 1 << 30, tq, LANES)
    var = [[min(max(bpq * i - jb, lo), hi) - lo for jb in range(nkb)] for i in range(nq)]
    return bias, var


def _diff_prompt(q, k, v, rel_bias, lam_p, subln_g, lam_init):
    s = q.shape[1]
    tq = tk = 512
    nq = s // tq
    pairs = [(i, j) for i in range(nq) for j in range((i * tq) // tk + 1)]
    bias, var = _block_bias(rel_bias, s // LANES, s // LANES, LANES)
    return _diff_attn(q, k, v, bias, pairs, var, lam_p, subln_g, lam_init, tq, tk)


def _sort_key(x):
    bits = pltpu.bitcast(x, I32)
    return jnp.where(bits < 0, INT_MIN - bits, bits)


def _dsa_kernel(nvis_ref, var_ref, q_ref, qi_ref, wi_ref, ki_ref, k_ref, v_ref, bias_ref, o_ref,
                keys_s, qs_s, p_s, al_s, m_s, acc_s, *, tq, tqa, tk, nk, rc, q0, length, topk, nbits):
    i = pl.program_id(1)
    nvis = nvis_ref[i]
    npair = DSA_KV_HEADS // 2
    hpp = 2 * DSA_GROUP
    nchunk = hpp * tqa // rc

    for hd in range(DSA_HEADS):
        qs_s[hd // hpp, (hd % hpp) * tqa:(hd % hpp + 1) * tqa, :] = q_ref[0, :, hd * LANES:(hd + 1) * LANES]
    qi_stack = jnp.concatenate([qi_ref[0, :, u * LANES:(u + 1) * LANES] for u in range(IDX_HEADS // 2)], axis=0)
    wi_t = wi_ref[0].T
    wrows = [wi_t[h:h + 1, :] for h in range(IDX_HEADS)]

    k_io = lax.broadcasted_iota(I32, (tk, tq), 0)
    q_chunk = (q0 + i * tq + lax.broadcasted_iota(I32, (tk, tq), 1)) >> 6
    score_scale = IDX_DIM ** -0.5 * IDX_HEADS ** -0.5

    def score_tile(j, last):
        off = pl.multiple_of(j * tk, tk)
        ki_lo = ki_ref[0, pl.ds(off, tk), 0:LANES]
        ki_hi = ki_ref[0, pl.ds(off, tk), LANES:2 * LANES]
        s_even = lax.dot_general(ki_lo, qi_stack, _NT, preferred_element_type=F32)
        s_odd = lax.dot_general(ki_hi, qi_stack, _NT, preferred_element_type=F32)
        score = jnp.zeros((tk, tq), F32)
        for u in range(IDX_HEADS // 2):
            score = score + jnp.maximum(s_even[:, u * tq:(u + 1) * tq], 0.0) * wrows[2 * u]
            score = score + jnp.maximum(s_odd[:, u * tq:(u + 1) * tq], 0.0) * wrows[2 * u + 1]
        key = _sort_key(score * score_scale)
        if last:
            kpos = j * tk + k_io
            vis = jnp.where((kpos >> 6) <= q_chunk, 1, 0) * jnp.where(kpos < length, 1, 0)
            key = jnp.where(vis > 0, key, INT_MIN)
        keys_s[j] = key

    def score_body(jj, carry):
        score_tile(2 * jj, False)
        score_tile(2 * jj + 1, False)
        return carry

    lax.fori_loop(0, (nvis - 1) // 2, score_body, 0)

    @pl.when(nvis % 2 == 0)
    def _():
        score_tile(nvis - 2, False)

    score_tile(nvis - 1, True)

    nacc = 4

    def column_counts(hit_fns):
        nf = len(hit_fns)

        def body(j, accs):
            kk = keys_s[j]
            accs = list(accs)
            for r in range(tk // 8):
                rows = kk[8 * r:8 * r + 8]
                for f, hit_fn in enumerate(hit_fns):
                    a = f * nacc + r % nacc
                    accs[a] = accs[a] + jnp.where(hit_fn(j, r, rows), 1.0, 0.0)
            return tuple(accs)

        accs = lax.fori_loop(0, nvis, body, (jnp.zeros((8, tq), F32),) * (nacc * nf))
        return [jnp.sum(functools.reduce(lambda a, b: a + b, accs[f * nacc:(f + 1) * nacc]), axis=0, keepdims=True)
                for f in range(nf)]

    def above(cand):
        below = jnp.broadcast_to(cand - 1, (8, tq))
        return lambda j, r, kk: kk > below

    def count_ge(cand):
        return column_counts([above(cand)])[0]

    def bit_body(it, ans):
        cand = ans | jnp.left_shift(jnp.int32(1), 31 - it)
        return jnp.where(count_ge(cand ^ INT_MIN) >= topk, cand, ans)

    thr = lax.fori_loop(0, 32, bit_body, jnp.zeros((1, tq), I32)) ^ INT_MIN
    thr_sel = jnp.maximum(thr, NEG_INF_KEY + 1)
    row_full = thr >= thr_sel

    cnt_ge, cnt_gt = column_counts([above(thr_sel), above(thr_sel + 1)])
    excess = jnp.max(jnp.where(row_full, cnt_ge - topk, 0.0))

    @pl.when(excess > 0.0)
    def _():
        quota = topk - cnt_gt
        thr8 = jnp.broadcast_to(thr, (8, tq))
        k_io8 = lax.broadcasted_iota(I32, (8, tq), 0)

        def count_tied_before(pos):
            pos8 = jnp.broadcast_to(pos, (8, tq))
            return column_counts([lambda j, r, kk: jnp.where(kk == thr8, 1, 0)
                                  * jnp.where(j * tk + 8 * r + k_io8 < pos8, 1, 0) > 0])[0]

        def pos_body(it, p):
            cand = p | jnp.left_shift(jnp.int32(1), nbits - 1 - it)
            return jnp.where(count_tied_before(cand) < quota, cand, p)

        last = lax.fori_loop(0, nbits, pos_body, jnp.zeros((1, tq), I32))
        last = jnp.where(row_full, last, nk * tk)

        def demote(j, carry):
            kpos = j * tk + k_io
            kk = keys_s[j]
            drop = jnp.where(kk == thr, 1, 0) * jnp.where(kpos > last, 1, 0)
            keys_s[j] = jnp.where(drop > 0, thr - 1, kk)
            return carry

        lax.fori_loop(0, nvis, demote, 0)

    m_s[...] = jnp.full(m_s.shape, NEG, F32)
    acc_s[...] = jnp.zeros(acc_s.shape, F32)

    ncb = tk // LANES

    def attn_body(j, carry):
        off = pl.multiple_of(j * tk, tk)
        bvar = [var_ref[(i * nk + j) * ncb + cb] for cb in range(ncb)]
        maskb = jnp.where(keys_s[j] >= thr_sel, 0.0, NEG).T[0:tqa]
        s_pair = [lax.dot_general(qs_s[pr], k_ref[0, pl.ds(off, tk), pr * LANES:(pr + 1) * LANES], _NT,
                                  preferred_element_type=F32) for pr in range(npair)]
        for pr in range(npair):
            for c in range(nchunk):
                rows = slice(c * rc, (c + 1) * rc)
                hd = pr * hpp + (c * rc) // tqa
                brow = (c * rc) % tqa
                s = jnp.concatenate(
                    [s_pair[pr][rows, cb * LANES:(cb + 1) * LANES] + bias_ref[bvar[cb], hd, brow:brow + rc, :]
                     for cb in range(ncb)], axis=1) + maskb[brow:brow + rc]
                m_prev = m_s[pr, rows]
                m_new = jnp.maximum(m_prev, jnp.max(s, axis=1, keepdims=True))
                p = jnp.exp2(s - jnp.concatenate([m_new] * ncb, axis=1))
                p_s[pr, rows] = p.astype(BF16)
                al_s[pr, rows] = jnp.exp2(m_prev - m_new)
                m_s[pr, rows] = m_new
            vp = v_ref[0, pl.ds(off, tk), pr * LANES:(pr + 1) * LANES]
            v1 = jnp.concatenate([vp, jnp.ones_like(vp)], axis=1)
            alpha = al_s[pr]
            acc_s[pr] = (jnp.concatenate([alpha, alpha], axis=1) * acc_s[pr]
                         + jnp.dot(p_s[pr], v1, preferred_element_type=F32))
        return carry

    lax.fori_loop(0, nvis, attn_body, 0)

    for hd in range(DSA_HEADS):
        rows = slice((hd % hpp) * tqa, (hd % hpp + 1) * tqa)
        o = acc_s[hd // hpp, rows, 0:LANES] / acc_s[hd // hpp, rows, LANES:2 * LANES]
        o_ref[0, :, hd * LANES:(hd + 1) * LANES] = o.astype(o_ref.dtype)


def _dsa_attn(q, qi, wi, ki2, k, v, bias, nvis, var, q0, length, topk, tq, tk):
    b, sq, dq = q.shape
    lp = k.shape[1]
    nq = qi.shape[1] // tq
    tqa = sq // nq
    nk = lp // tk
    nvar = bias.shape[0]
    npair = DSA_KV_HEADS // 2
    prow = 2 * DSA_GROUP * tqa
    kernel = functools.partial(_dsa_kernel, tq=tq, tqa=tqa, tk=tk, nk=nk, rc=min(tqa, 64), q0=q0, length=length,
                               topk=topk, nbits=max(1, (lp - 1).bit_length()))
    once = dict(pipeline_mode=pl.Buffered(1))
    full = lambda bb, ii, *_: (bb, 0, 0)
    tile = lambda bb, ii, *_: (bb, ii, 0)
    return pl.pallas_call(
        kernel,
        grid_spec=pltpu.PrefetchScalarGridSpec(
            num_scalar_prefetch=2,
            grid=(b, nq),
            in_specs=[
                pl.BlockSpec((1, tqa, dq), tile),
                pl.BlockSpec((1, tq, qi.shape[2]), tile),
                pl.BlockSpec((1, tq, LANES), tile),
                pl.BlockSpec((1, lp, 2 * LANES), full, **once),
                pl.BlockSpec((1, lp, k.shape[2]), full, **once),
                pl.BlockSpec((1, lp, v.shape[2]), full, **once),
                pl.BlockSpec((nvar, BIAS_HEADS, tqa, LANES), lambda bb, ii, *_: (0, 0, 0, 0), **once),
            ],
            out_specs=pl.BlockSpec((1, tqa, dq), tile),
            scratch_shapes=[
                pltpu.VMEM((nk, tk, tq), I32),
                pltpu.VMEM((npair, prow, LANES), BF16),
                pltpu.VMEM((npair, prow, tk), BF16),
                pltpu.VMEM((npair, prow, LANES), F32),
                pltpu.VMEM((npair, prow, LANES), F32),
                pltpu.VMEM((npair, prow, 2 * LANES), F32),
            ],
        ),
        out_shape=jax.ShapeDtypeStruct((b, sq, dq), BF16),
        compiler_params=pltpu.CompilerParams(vmem_limit_bytes=VMEM_LIMIT),
        name="dsa_attn",
    )(jnp.asarray(nvis, I32), jnp.asarray(var, I32).reshape(-1), q, qi, wi, ki2, k, v, bias)


def _dsa_prompt(q, qi, wi, ki2, k, v, rel_bias):
    s = q.shape[1]
    tq, tk = 128, 512
    nq = s // tq
    nvis = [(i * tq) // tk + 1 for i in range(nq)]
    bias, var = _block_bias(rel_bias, nq, s // LANES, tq)
    return _dsa_attn(q, qi, wi, ki2, k, v, bias, nvis, var, 0, s, min(IDX_TOPK, s // 4), tq, tk)


def _route(lt_s, rb_ref, gt_s):
    rows = [lt_s[e:e + 1, :] for e in range(N_EXPERTS)]
    mx = functools.reduce(jnp.maximum, rows)
    ex = [jnp.exp(r - mx) for r in rows]
    z = functools.reduce(lambda a, b: a + b, ex)
    probs = [x / z for x in ex]
    sel = [probs[e] + rb_ref[e] for e in range(N_EXPERTS)]
    best_val, best_grp = None, None
    for gidx in range(N_GROUPS):
        a, b, c, d = sel[EXPERTS_PER_GROUP * gidx:EXPERTS_PER_GROUP * (gidx + 1)]
        hi1, lo1, hi2, lo2 = jnp.maximum(a, b), jnp.minimum(a, b), jnp.maximum(c, d), jnp.minimum(c, d)
        top2 = jnp.maximum(hi1, hi2) + jnp.maximum(jnp.minimum(hi1, hi2), jnp.maximum(lo1, lo2))
        if gidx == 0:
            best_val, best_grp = top2, jnp.zeros_like(top2, dtype=I32)
        else:
            better = top2 > best_val
            best_grp = jnp.where(better, gidx, best_grp)
            best_val = jnp.where(better, top2, best_val)
    masked = [jnp.where(best_grp == e // EXPERTS_PER_GROUP, sel[e], -jnp.inf) for e in range(N_EXPERTS)]

    def first_argmax(vals):
        bv, bi = vals[0], jnp.zeros_like(best_grp)
        for e in range(1, N_EXPERTS):
            better = vals[e] > bv
            bi = jnp.where(better, e, bi)
            bv = jnp.where(better, vals[e], bv)
        return bi

    i1 = first_argmax(masked)
    i2 = first_argmax([jnp.where(i1 == e, -jnp.inf, masked[e]) for e in range(N_EXPERTS)])
    w1 = functools.reduce(lambda a, b: a + b, [jnp.where(i1 == e, probs[e], 0.0) for e in range(N_EXPERTS)])
    w2 = functools.reduce(lambda a, b: a + b, [jnp.where(i2 == e, probs[e], 0.0) for e in range(N_EXPERTS)])
    wsum = w1 + w2
    w1, w2 = w1 / wsum, w2 / wsum
    for e in range(N_EXPERTS):
        gt_s[e:e + 1, :] = jnp.where(i1 == e, w1, 0.0) + jnp.where(i2 == e, w2, 0.0)


def _moe_kernel(rb_ref, x_ref, g_ref, shift_ref, scale_ref, gate_ref, wrh_ref, wrl_ref, wg_ref, wu_ref, wd_ref,
                gfin_ref, out_ref, hb_s, lt_s, gt_s, gates_s, acc_s, *, final_norm):
    e = pl.program_id(1)

    @pl.when(e == 0)
    def _():
        h = _norm_mod(x_ref[...], g_ref[...], shift_ref[0], scale_ref[0])
        hb = h.astype(BF16)
        hb_s[...] = hb
        hl = (h - hb.astype(F32)).astype(BF16)
        lt_s[...] = (lax.dot_general(wrh_ref[...], hb, _NT, preferred_element_type=F32)
                     + lax.dot_general(wrl_ref[...], hb, _NT, preferred_element_type=F32)
                     + lax.dot_general(wrh_ref[...], hl, _NT, preferred_element_type=F32))
        gt_s[...] = jnp.zeros(gt_s.shape, F32)
        _route(lt_s, rb_ref, gt_s)
        gates_s[...] = gt_s[...].T
        acc_s[...] = jnp.zeros(acc_s.shape, F32)

    hb = hb_s[...]
    up = jnp.dot(hb, wu_ref[0], preferred_element_type=F32)
    gt = jnp.dot(hb, wg_ref[0], preferred_element_type=F32)
    lane = lax.broadcasted_iota(I32, gates_s.shape, 1)
    gcol = jnp.sum(jnp.where(lane == e, gates_s[...], 0.0), axis=1, keepdims=True)
    a = (gt * jax.nn.sigmoid(gt) * up * gcol).astype(BF16)
    acc_s[...] += jnp.dot(a, wd_ref[0], preferred_element_type=F32)

    @pl.when(e == N_EXPERTS - 1)
    def _():
        y = x_ref[...] + _per_segment(acc_s[...], gate_ref[0], lambda v, s: v * s)
        if final_norm:
            y = _rms(y, gfin_ref[...])
        out_ref[...] = y


def _moe(x, g, shift, scale, gate, w_router, router_bias, wg_bf, wu_bf, wd_bf, g_final, final_norm, tm):
    n, d = x.shape
    nt = n // tm
    nseg = tm // CHUNK
    ff = wg_bf.shape[2]
    wrt = w_router.T
    wrh = wrt.astype(BF16)
    wrl = (wrt - wrh.astype(F32)).astype(BF16)
    seg_spec = pl.BlockSpec((1, nseg, d), lambda t, e: (t, 0, 0))
    row_spec = pl.BlockSpec((1, d), lambda t, e: (0, 0))
    return pl.pallas_call(
        functools.partial(_moe_kernel, final_norm=final_norm),
        grid=(nt, N_EXPERTS),
        in_specs=[
            pl.BlockSpec(memory_space=pltpu.SMEM),
            pl.BlockSpec((tm, d), lambda t, e: (t, 0)),
            row_spec, seg_spec, seg_spec, seg_spec,
            pl.BlockSpec((N_EXPERTS, d), lambda t, e: (0, 0)),
            pl.BlockSpec((N_EXPERTS, d), lambda t, e: (0, 0)),
            pl.BlockSpec((1, d, ff), lambda t, e: (e, 0, 0)),
            pl.BlockSpec((1, d, ff), lambda t, e: (e, 0, 0)),
            pl.BlockSpec((1, ff, d), lambda t, e: (e, 0, 0)),
            row_spec,
        ],
        out_specs=pl.BlockSpec((tm, d), lambda t, e: (t, 0)),
        out_shape=jax.ShapeDtypeStruct((n, d), F32),
        scratch_shapes=[
            pltpu.VMEM((tm, d), BF16),
            pltpu.VMEM((N_EXPERTS, tm), F32),
            pltpu.VMEM((LANES, tm), F32),
            pltpu.VMEM((tm, LANES), F32),
            pltpu.VMEM((tm, d), F32),
        ],
        compiler_params=pltpu.CompilerParams(vmem_limit_bytes=VMEM_LIMIT),
        name="moe",
    )(router_bias, x, g.reshape(1, d), shift.reshape(nt, nseg, d), scale.reshape(nt, nseg, d),
      gate.reshape(nt, nseg, d), wrh, wrl, wg_bf, wu_bf, wd_bf, g_final.reshape(1, d))


def _dsa_in_weights(w):
    d = w.shape[0]
    o1 = DSA_HEADS * DSA_HEAD_DIM
    o2 = o1 + DSA_KV_HEADS * DSA_HEAD_DIM
    o3 = o2 + DSA_KV_HEADS * DSA_HEAD_DIM
    o4 = o3 + IDX_HEADS * IDX_DIM
    o5 = o4 + IDX_DIM
    wq = w[:, :o1].reshape(d, DSA_HEADS, 1, DSA_HEAD_DIM)
    half = ((np.arange(DSA_HEADS) // DSA_GROUP) % 2).reshape(1, DSA_HEADS, 1, 1)
    wq = jnp.concatenate([jnp.where(half == 0, wq, 0.0), jnp.where(half == 1, wq, 0.0)], axis=2)
    wki = w[:, o4:o5]
    zk = jnp.zeros_like(wki)
    wwi = jnp.pad(w[:, o5:], ((0, 0), (0, LANES - IDX_HEADS)))
    return jnp.concatenate([wq.reshape(d, 2 * o1), w[:, o1:o4], wki, zk, zk, wki, wwi], axis=1).astype(BF16)


def _dsa_out_weights(w):
    d = w.shape[1]
    w4 = w.reshape(DSA_HEADS, 1, DSA_HEAD_DIM, d)
    half = ((np.arange(DSA_HEADS) // DSA_GROUP) % 2).reshape(DSA_HEADS, 1, 1, 1)
    w4 = jnp.concatenate([jnp.where(half == 0, w4, 0.0), jnp.where(half == 1, w4, 0.0)], axis=1)
    return w4.reshape(2 * DSA_HEADS * DSA_HEAD_DIM, d).astype(BF16)


def kernel(x_prompt, x_sample, cache_k_diff, cache_v_diff, cache_k_dsa, cache_v_dsa, cache_kidx_dsa,
           c_prompt, c_sample, rel_bias, w_in_diff, w_out_diff, lam_diff, subln_g_diff, w_in_dsa, w_out_dsa,
           w_ada, b_ada, g_norm_mix, g_norm_ffn, w_router, router_bias, w_gate, w_up, w_down, g_final):
    b, s, d = x_prompt.shape
    bs, ss, _ = x_sample.shape
    past = cache_k_diff.shape[2]
    depth = w_ada.shape[0]
    assert d == D_MODEL and ss == CHUNK and s % 512 == 0 and past % CHUNK == 0 and depth == 2
    l_s = past + ss
    lp_s = -(-l_s // LANES) * LANES
    nkb_s = lp_s // LANES

    streams = []
    for x3, rows in ((x_prompt, np.repeat(np.arange(b), s // CHUNK)), (x_sample, b + np.arange(bs))):
        n = x3.shape[0] * x3.shape[1]
        streams.append((x3.reshape(n, d), x3.shape[:2], rows,
                        _pick_tile(n, (512, 256, 128, 64)), _pick_tile(n, (1024, 512, 256, 128, 64))))
    mod = _ada_mod(jnp.concatenate([c_prompt, c_sample], axis=0), w_ada, b_ada)

    def mods(layer, rows):
        m = mod[layer][rows]
        return [m[:, i * d:(i + 1) * d] for i in range(6)]

    def pad_keys(x):
        return jnp.pad(x, ((0, 0), (0, lp_s - x.shape[1]), (0, 0)))

    bias_s = _bias_tiles(rel_bias, [jb * LANES - past for jb in range(nkb_s)], [past] * nkb_s, l_s, ss, LANES)
    var_s = [list(range(nkb_s))]
    xs = [st[0] for st in streams]
    outs_kv = []

    qk = DIFF_HEADS * 2 * DIFF_HEAD_DIM
    lam_init = 0.8 - 0.6 * float(np.exp(-0.3 * 0))
    q_scale = DIFF_HEAD_DIM ** -0.5 * LOG2E
    w_in0, w_out0 = w_in_diff[0].astype(BF16), w_out_diff[0].astype(BF16)
    wt_k0 = w_in_diff[0][:, qk:2 * qk].T.astype(BF16)
    moe_w0 = (w_gate[0].astype(BF16), w_up[0].astype(BF16), w_down[0].astype(BF16))
    for si, (_, (nb, ns), rows, tm, tm_moe) in enumerate(streams):
        m = mods(0, rows)
        if si == 0:
            cols = ((0, qk, ((0, 0, q_scale),)), (qk, qk, ((2, 0, 1.0),)), (2 * qk, qk, ((1, 0, 1.0), (3, 0, 1.0))))
            outs = ((qk, BF16), (qk, F32), (qk, BF16), (qk, BF16))
            q_bf, v_f, k_bf, v_bf, k_t = _in_proj(xs[si], g_norm_mix[0], m[0], m[1], w_in0, cols, outs, tm,
                                                  wt_k0, ((0, qk, F32),), ns)
            k_f = jnp.transpose(k_t.reshape(nb, DIFF_HEADS, 2, DIFF_HEAD_DIM, ns), (0, 4, 1, 2, 3))
        else:
            cols = ((0, qk, ((0, 0, q_scale),)), (qk, qk, ((1, 0, 1.0), (3, 0, 1.0))),
                    (2 * qk, qk, ((2, 0, 1.0), (4, 0, 1.0))))
            outs = ((qk, BF16), (qk, F32), (qk, F32), (qk, BF16), (qk, BF16))
            q_bf, k_f, v_f, k_bf, v_bf = _in_proj(xs[si], g_norm_mix[0], m[0], m[1], w_in0, cols, outs, tm)
        outs_kv.append((k_f, v_f))
        q3, k3, v3 = (a.reshape(nb, ns, qk) for a in (q_bf, k_bf, v_bf))
        if si == 0:
            o = _diff_prompt(q3, k3, v3, rel_bias, lam_diff[0], subln_g_diff[0], lam_init)
        else:
            ka = pad_keys(jnp.concatenate([cache_k_diff[0].reshape(nb, past, qk).astype(BF16), k3], axis=1))
            va = pad_keys(jnp.concatenate([cache_v_diff[0].reshape(nb, past, qk).astype(BF16), v3], axis=1))
            o = _diff_attn(q3, ka, va, bias_s, [(0, 0)], var_s, lam_diff[0], subln_g_diff[0], lam_init, ns, lp_s)
        x = _out_proj(o.reshape(nb * ns, qk), xs[si], m[2], w_out0, tm)
        xs[si] = _moe(x, g_norm_ffn[0], m[3], m[4], m[5], w_router, router_bias, *moe_w0, g_final, False, tm_moe)

    nq_w = 2 * DSA_HEADS * DSA_HEAD_DIM
    kv_w = DSA_KV_HEADS * DSA_HEAD_DIM
    qi_w = IDX_HEADS * IDX_DIM
    c_k, c_v, c_qi = nq_w, nq_w + kv_w, nq_w + 2 * kv_w
    c_ki, c_wi = c_qi + qi_w, c_qi + qi_w + 2 * LANES
    half_q = nq_w // 2
    q_scale = DSA_HEAD_DIM ** -0.5 * LOG2E
    cols_q = ((0, half_q, ((0, 0, q_scale),)), (half_q, half_q, ((0, half_q, q_scale),)))
    w_in1, w_out1 = _dsa_in_weights(w_in_dsa[0]), _dsa_out_weights(w_out_dsa[0])
    o_k = DSA_HEADS * DSA_HEAD_DIM
    o_v, o_ki = o_k + kv_w, o_k + 2 * kv_w + qi_w
    wt_1 = jnp.concatenate([w_in_dsa[0][:, o_k:o_v], w_in_dsa[0][:, o_v:o_v + kv_w],
                            w_in_dsa[0][:, o_ki:o_ki + IDX_DIM]], axis=1).T.astype(BF16)
    moe_w1 = (w_gate[1].astype(BF16), w_up[1].astype(BF16), w_down[1].astype(BF16))
    for si, (_, (nb, ns), rows, tm, tm_moe) in enumerate(streams):
        m = mods(1, rows)
        if si == 0:
            cols = cols_q + ((c_k, kv_w, ((1, 0, 1.0),)), (c_v, kv_w, ((2, 0, 1.0),)), (c_qi, qi_w, ((3, 0, 1.0),)),
                             (c_ki, 2 * LANES, ((4, 0, 1.0),)), (c_wi, LANES, ((5, 0, 1.0),)))
            outs = ((nq_w, BF16), (kv_w, BF16), (kv_w, BF16), (qi_w, BF16), (2 * LANES, BF16), (LANES, F32))
            q2, k_bf, v_bf, qi_bf, ki2_bf, wi_f, k_t, v_t, ki_t = _in_proj(
                xs[si], g_norm_mix[1], m[0], m[1], w_in1, cols, outs, tm, wt_1,
                ((0, kv_w, F32), (kv_w, kv_w, F32), (2 * kv_w, IDX_DIM, F32)), ns)
            k_f = jnp.transpose(k_t.reshape(nb, DSA_KV_HEADS, DSA_HEAD_DIM, ns), (0, 3, 1, 2))
            v_f = jnp.transpose(v_t.reshape(nb, DSA_KV_HEADS, DSA_HEAD_DIM, ns), (0, 3, 1, 2))
            ki_f = jnp.transpose(ki_t, (0, 2, 1))
        else:
            cols = cols_q + ((c_k, kv_w, ((1, 0, 1.0), (3, 0, 1.0))), (c_v, kv_w, ((2, 0, 1.0), (4, 0, 1.0))),
                             (c_qi, qi_w, ((5, 0, 1.0),)), (c_ki, 2 * LANES, ((7, 0, 1.0),)),
                             (c_ki, IDX_DIM, ((6, 0, 1.0),)), (c_wi, LANES, ((8, 0, 1.0),)))
            outs = ((nq_w, BF16), (kv_w, F32), (kv_w, F32), (kv_w, BF16), (kv_w, BF16), (qi_w, BF16),
                    (IDX_DIM, F32), (2 * LANES, BF16), (LANES, F32))
            q2, k_f, v_f, k_bf, v_bf, qi_bf, ki_f, ki2_bf, wi_f = _in_proj(
                xs[si], g_norm_mix[1], m[0], m[1], w_in1, cols, outs, tm)
        outs_kv.append((k_f, v_f, ki_f))
        q3, qi3, wi3 = q2.reshape(nb, ns, nq_w), qi_bf.reshape(nb, ns, qi_w), wi_f.reshape(nb, ns, LANES)
        k3, v3, ki3 = k_bf.reshape(nb, ns, kv_w), v_bf.reshape(nb, ns, kv_w), ki2_bf.reshape(nb, ns, 2 * LANES)
        if si == 0:
            o = _dsa_prompt(q3, qi3, wi3, ki3, k3, v3, rel_bias)
        else:
            cki = cache_kidx_dsa[0].astype(BF16)
            zki = jnp.zeros_like(cki)
            ka = pad_keys(jnp.concatenate([cache_k_dsa[0].reshape(nb, past, kv_w).astype(BF16), k3], axis=1))
            va = pad_keys(jnp.concatenate([cache_v_dsa[0].reshape(nb, past, kv_w).astype(BF16), v3], axis=1))
            kia = pad_keys(jnp.concatenate([jnp.concatenate([cki, zki, zki, cki], axis=2), ki3], axis=1))
            pad_q = lambda a: jnp.pad(a, ((0, 0), (0, LANES - ns), (0, 0)))
            tk_s = max(t for t in (512, 384, 256, 128) if lp_s % t == 0)
            o = _dsa_attn(q3, pad_q(qi3), pad_q(wi3), kia, ka, va, bias_s, [lp_s // tk_s], var_s, past, l_s,
                          min(IDX_TOPK, l_s // 4), LANES, tk_s)
        x = _out_proj(o.reshape(nb * ns, nq_w), xs[si], m[2], w_out1, tm)
        xs[si] = _moe(x, g_norm_ffn[1], m[3], m[4], m[5], w_router, router_bias, *moe_w1, g_final, True, tm_moe)

    h2 = (DIFF_HEADS, 2, DIFF_HEAD_DIM)
    hv = (DIFF_HEADS, 2 * DIFF_HEAD_DIM)
    hk = (DSA_KV_HEADS, DSA_HEAD_DIM)
    (kd_p, vd_p), (kd_s, vd_s), (ks_p, vs_p, ki_p), (ks_s, vs_s, ki_s) = outs_kv
    return (xs[0].reshape(b, s, d), xs[1].reshape(bs, ss, d),
            kd_p.reshape(1, b, s, *h2), vd_p.reshape(1, b, s, *hv),
            ks_p.reshape(1, b, s, *hk), vs_p.reshape(1, b, s, *hk), ki_p.reshape(1, b, s, IDX_DIM),
            kd_s.reshape(1, bs, ss, *h2), vd_s.reshape(1, bs, ss, *hv),
            ks_s.reshape(1, bs, ss, *hk), vs_s.reshape(1, bs, ss, *hk), ki_s.reshape(1, bs, ss, IDX_DIM))
```

```python
import functools

import jax
import jax.numpy as jnp
import numpy as np
from jax import lax
from jax.experimental import pallas as pl
from jax.experimental.pallas import tpu as pltpu

F32 = jnp.float32
BF16 = jnp.bfloat16
I32 = jnp.int32

D_MODEL = 1024
CHUNK = 64
DIFF_HEADS = 8
DIFF_HEAD_DIM = 64
DSA_HEADS = 16
DSA_KV_HEADS = 4
DSA_GROUP = DSA_HEADS // DSA_KV_HEADS
DSA_HEAD_DIM = 64
IDX_HEADS = 8
IDX_DIM = 64
IDX_TOPK = 256
REL_BUCKETS = 32
BIAS_HEADS = 16
N_EXPERTS = 16
N_GROUPS = 4
EXPERTS_PER_GROUP = N_EXPERTS // N_GROUPS
RMS_EPS = 1e-6

LANES = 128
NEG = -1e30
LOG2E = 1.4426950408889634
INT_MIN = -(2 ** 31)
NEG_INF_KEY = -0x7F800000
VMEM_LIMIT = 56 * 1024 * 1024

_BUCKET_STEPS = (12, 16, 23, 32, 46, 64, 91)
FAR_BLOCKS = 2

_NT = (((1,), (1,)), ((), ()))


def _pick_tile(n, prefs):
    for t in prefs:
        if n % t == 0:
            return t
    raise ValueError(f"no tile in {prefs} divides {n}")


def _rms(x, g):
    ms = jnp.mean(x * x, axis=-1, keepdims=True)
    return x * lax.rsqrt(ms + RMS_EPS) * g


def _per_segment(x, seg_vals, op):
    tm, d = x.shape
    nseg = tm // CHUNK
    return op(x.reshape(nseg, CHUNK, d), seg_vals[:, None, :]).reshape(tm, d)


def _norm_mod(x, g, shift, scale):
    y = _rms(x, g)
    y = _per_segment(y, scale, lambda a, s: a * (1.0 + s))
    return _per_segment(y, shift, lambda a, s: a + s)


def _ada_kernel(c_ref, w_ref, b_ref, o_ref):
    c = c_ref[...]
    a = (c * jax.nn.sigmoid(c)).astype(BF16)
    o_ref[0] = jnp.dot(a, w_ref[0].astype(BF16), preferred_element_type=F32) + b_ref[0]


def _ada_mod(c_all, w_ada, b_ada):
    depth, d, n6 = w_ada.shape
    bc = c_all.shape[0]
    tn = _pick_tile(n6, (1536, 1024, 512, 128))
    return pl.pallas_call(
        _ada_kernel,
        grid=(depth, n6 // tn),
        in_specs=[
            pl.BlockSpec((bc, d), lambda l, j: (0, 0)),
            pl.BlockSpec((1, d, tn), lambda l, j: (l, 0, j)),
            pl.BlockSpec((1, 1, tn), lambda l, j: (l, 0, j)),
        ],
        out_specs=pl.BlockSpec((1, bc, tn), lambda l, j: (l, 0, j)),
        out_shape=jax.ShapeDtypeStruct((depth, bc, n6), F32),
        compiler_params=pltpu.CompilerParams(vmem_limit_bytes=VMEM_LIMIT),
        name="ada_mod",
    )(c_all, w_ada, b_ada.reshape(depth, 1, n6))


def _bias_kernel(off_ref, q0_ref, len_ref, rb_ref, o_ref, *, tq, tk):
    v = pl.program_id(0)
    shape = (tq, tk)
    r = lax.broadcasted_iota(I32, shape, 0)
    c = lax.broadcasted_iota(I32, shape, 1)
    rel = off_ref[v] + c - r
    qpos = q0_ref[v] + r
    kpos = qpos + rel
    n = jnp.abs(rel)
    big = jnp.full(shape, 8, I32)
    for t in _BUCKET_STEPS:
        big = big + jnp.where(n >= t, 1, 0)
    bucket = jnp.where(n < 8, n, big) + jnp.where(rel > 0, REL_BUCKETS // 2, 0)
    vis = jnp.where((kpos >> 6) <= (qpos >> 6), 1, 0) * jnp.where(kpos < len_ref[0], 1, 0)
    for h in range(BIAS_HEADS):
        val = jnp.zeros(shape, F32)
        for b in range(REL_BUCKETS):
            val = jnp.where(bucket == b, rb_ref[b, h], val)
        o_ref[0, h] = jnp.where(vis > 0, val * LOG2E, NEG)


def _bias_tiles(rel_bias, offs, q0s, length, tq, tk):
    nvar = len(offs)
    return pl.pallas_call(
        functools.partial(_bias_kernel, tq=tq, tk=tk),
        grid_spec=pltpu.PrefetchScalarGridSpec(
            num_scalar_prefetch=3,
            grid=(nvar,),
            in_specs=[pl.BlockSpec(memory_space=pltpu.SMEM)],
            out_specs=pl.BlockSpec((1, BIAS_HEADS, tq, tk), lambda v, *_: (v, 0, 0, 0)),
        ),
        out_shape=jax.ShapeDtypeStruct((nvar, BIAS_HEADS, tq, tk), F32),
        name="bias_tiles",
    )(jnp.asarray(offs, I32), jnp.asarray(q0s, I32), jnp.asarray([length], I32), rel_bias)


def _inproj_kernel(x_ref, g_ref, shift_ref, scale_ref, w_ref, wt_ref, *out_refs, cols, t_cols):
    h = _norm_mod(x_ref[...], g_ref[...], shift_ref[0], scale_ref[0]).astype(BF16)
    for start, width, dests in cols:
        p = jnp.dot(h, w_ref[:, start:start + width], preferred_element_type=F32)
        for out_idx, out_off, mult in dests:
            o_ref = out_refs[out_idx]
            val = p if mult == 1.0 else p * mult
            o_ref[:, out_off:out_off + width] = val.astype(o_ref.dtype)
    t_refs = out_refs[len(out_refs) - len(t_cols):]
    for (start, width, _), t_ref in zip(t_cols, t_refs):
        t_ref[0] = lax.dot_general(wt_ref[start:start + width, :], h, _NT,
                                   preferred_element_type=F32).astype(t_ref.dtype)


def _in_proj(x, g, shift, scale, w_bf, cols, out_defs, tm, wt_bf=None, t_cols=(), seq=None):
    n, d = x.shape
    nt = n // tm
    nseg = tm // CHUNK
    nout = w_bf.shape[1]
    if wt_bf is None:
        wt_bf = jnp.zeros((16, d), BF16)
    seg_spec = pl.BlockSpec((1, nseg, d), lambda t: (t, 0, 0))
    tpb = (seq or n) // tm
    return pl.pallas_call(
        functools.partial(_inproj_kernel, cols=cols, t_cols=t_cols),
        grid=(nt,),
        in_specs=[
            pl.BlockSpec((tm, d), lambda t: (t, 0)),
            pl.BlockSpec((1, d), lambda t: (0, 0)),
            seg_spec, seg_spec,
            pl.BlockSpec((d, nout), lambda t: (0, 0)),
            pl.BlockSpec(wt_bf.shape, lambda t: (0, 0)),
        ],
        out_specs=([pl.BlockSpec((tm, w), lambda t: (t, 0)) for w, _ in out_defs]
                   + [pl.BlockSpec((1, w, tm), lambda t: (t // tpb, 0, t % tpb)) for _, w, _ in t_cols]),
        out_shape=([jax.ShapeDtypeStruct((n, w), dt) for w, dt in out_defs]
                   + [jax.ShapeDtypeStruct((n // (seq or n), w, seq or n), dt) for _, w, dt in t_cols]),
        compiler_params=pltpu.CompilerParams(vmem_limit_bytes=VMEM_LIMIT),
        name="in_proj",
    )(x, g.reshape(1, d), shift.reshape(nt, nseg, d), scale.reshape(nt, nseg, d), w_bf, wt_bf)


def _outproj_kernel(o_ref, x_ref, gate_ref, w_ref, out_ref):
    r = jnp.dot(o_ref[...], w_ref[...], preferred_element_type=F32)
    out_ref[...] = x_ref[...] + _per_segment(r, gate_ref[0], lambda a, s: a * s)


def _out_proj(o, x, gate, w_bf, tm):
    n, d = x.shape
    kdim = o.shape[1]
    nt = n // tm
    nseg = tm // CHUNK
    return pl.pallas_call(
        _outproj_kernel,
        grid=(nt,),
        in_specs=[
            pl.BlockSpec((tm, kdim), lambda t: (t, 0)),
            pl.BlockSpec((tm, d), lambda t: (t, 0)),
            pl.BlockSpec((1, nseg, d), lambda t: (t, 0, 0)),
            pl.BlockSpec((kdim, d), lambda t: (0, 0)),
        ],
        out_specs=pl.BlockSpec((tm, d), lambda t: (t, 0)),
        out_shape=jax.ShapeDtypeStruct((n, d), F32),
        compiler_params=pltpu.CompilerParams(vmem_limit_bytes=VMEM_LIMIT),
        name="out_proj",
    )(o, x, gate.reshape(nt, nseg, d), w_bf)


def _diff_attn_kernel(it_ref, jt_ref, var_ref, fin_ref, lam_ref, q_ref, k_ref, v_ref, bias_ref, g_ref,
                      o_ref, qb_s, p_s, al_s, m_s, acc_s, *, tq, tk, tb, rc, nkb, lam_init):
    t = pl.program_id(1)
    nchunk = 2 * tq // rc
    ncb = tk // LANES
    nrb = tq // tb
    bvar = [[var_ref[(it_ref[t] * nrb + rb) * nkb + jt_ref[t] * ncb + cb] for cb in range(ncb)]
            for rb in range(nrb)]

    @pl.when(jt_ref[t] == 0)
    def _():
        m_s[...] = jnp.full(m_s.shape, NEG, F32)
        acc_s[...] = jnp.zeros(acc_s.shape, F32)
        lane = lax.broadcasted_iota(I32, (tq, LANES), 1)
        for h in range(DIFF_HEADS):
            qh = q_ref[0, :, h * LANES:(h + 1) * LANES].astype(F32)
            qb_s[h, 0:tq, :] = jnp.where(lane < DIFF_HEAD_DIM, qh, 0.0).astype(BF16)
            qb_s[h, tq:2 * tq, :] = jnp.where(lane >= DIFF_HEAD_DIM, qh, 0.0).astype(BF16)

    def logits(u):
        h, mm = divmod(u, 2)
        return lax.dot_general(qb_s[h, mm * tq:(mm + 1) * tq], k_ref[0, :, h * LANES:(h + 1) * LANES], _NT,
                               preferred_element_type=F32)

    s_next = logits(0)
    for h in range(DIFF_HEADS):
        for mm in range(2):
            u = 2 * h + mm
            s_half = s_next
            if u + 1 < 2 * DIFF_HEADS:
                s_next = logits(u + 1)
            for c in range(tq // rc):
                rows = slice(mm * tq + c * rc, mm * tq + (c + 1) * rc)
                qrow = c * rc
                rb, brow = qrow // tb, qrow % tb
                s = jnp.concatenate(
                    [s_half[c * rc:(c + 1) * rc, cb * LANES:(cb + 1) * LANES]
                     + bias_ref[bvar[rb][cb], u, brow:brow + rc, :] for cb in range(ncb)], axis=1)
                m_prev = m_s[h, rows]
                m_new = jnp.maximum(m_prev, jnp.max(s, axis=1, keepdims=True))
                p = jnp.exp2(s - jnp.concatenate([m_new] * ncb, axis=1))
                p_s[h, rows] = p.astype(BF16)
                al_s[h, rows] = jnp.exp2(m_prev - m_new)
                m_s[h, rows] = m_new
        vh = v_ref[0, :, h * LANES:(h + 1) * LANES]
        v1 = jnp.concatenate([vh, jnp.ones_like(vh)], axis=1)
        alpha = al_s[h]
        acc_s[h] = (jnp.concatenate([alpha, alpha], axis=1) * acc_s[h]
                    + jnp.dot(p_s[h], v1, preferred_element_type=F32))

    @pl.when(fin_ref[t] == 1)
    def _():
        lp = lam_ref[...]
        lam = (jnp.exp(jnp.sum(lp[0:1] * lp[1:2], axis=1, keepdims=True))
               - jnp.exp(jnp.sum(lp[2:3] * lp[3:4], axis=1, keepdims=True)) + lam_init)
        g = g_ref[...]
        for h in range(DIFF_HEADS):
            o0 = acc_s[h, 0:tq, 0:LANES] / acc_s[h, 0:tq, LANES:2 * LANES]
            o1 = acc_s[h, tq:2 * tq, 0:LANES] / acc_s[h, tq:2 * tq, LANES:2 * LANES]
            o = _rms(o0 - lam * o1, g) * (1.0 - lam_init)
            o_ref[0, :, h * LANES:(h + 1) * LANES] = o.astype(o_ref.dtype)


def _diff_attn(q, k, v, bias, pairs, var, lam_p, subln_g, lam_init, tq, tk):
    b, sq, dq = q.shape
    nvar, tb = bias.shape[0], bias.shape[2]
    nkb = k.shape[1] // LANES
    it = np.asarray([p[0] for p in pairs], np.int32)
    jt = np.asarray([p[1] for p in pairs], np.int32)
    vt = np.asarray(var, np.int32).reshape(-1)
    fin = np.asarray([1 if (n + 1 == len(pairs) or pairs[n + 1][0] != p[0]) else 0
                      for n, p in enumerate(pairs)], np.int32)
    rc = min(tq, 64)
    return pl.pallas_call(
        functools.partial(_diff_attn_kernel, tq=tq, tk=tk, tb=tb, rc=rc, nkb=nkb, lam_init=lam_init),
        grid_spec=pltpu.PrefetchScalarGridSpec(
            num_scalar_prefetch=4,
            grid=(b, len(pairs)),
            in_specs=[
                pl.BlockSpec((4, DIFF_HEAD_DIM), lambda bb, t, *_: (0, 0)),
                pl.BlockSpec((1, tq, dq), lambda bb, t, it_, jt_, vt_, fin_: (bb, it_[t], 0)),
                pl.BlockSpec((1, tk, dq), lambda bb, t, it_, jt_, vt_, fin_: (bb, jt_[t], 0)),
                pl.BlockSpec((1, tk, dq), lambda bb, t, it_, jt_, vt_, fin_: (bb, jt_[t], 0)),
                pl.BlockSpec((nvar, BIAS_HEADS, tb, LANES), lambda bb, t, *_: (0, 0, 0, 0),
                             pipeline_mode=pl.Buffered(1)),
                pl.BlockSpec((1, 2 * DIFF_HEAD_DIM), lambda bb, t, *_: (0, 0)),
            ],
            out_specs=pl.BlockSpec((1, tq, dq), lambda bb, t, it_, jt_, vt_, fin_: (bb, it_[t], 0)),
            scratch_shapes=[
                pltpu.VMEM((DIFF_HEADS, 2 * tq, LANES), BF16),
                pltpu.VMEM((DIFF_HEADS, 2 * tq, tk), BF16),
                pltpu.VMEM((DIFF_HEADS, 2 * tq, LANES), F32),
                pltpu.VMEM((DIFF_HEADS, 2 * tq, LANES), F32),
                pltpu.VMEM((DIFF_HEADS, 2 * tq, 2 * LANES), F32),
            ],
        ),
        out_shape=jax.ShapeDtypeStruct((b, sq, dq), BF16),
        compiler_params=pltpu.CompilerParams(vmem_limit_bytes=VMEM_LIMIT),
        name="diff_attn",
    )(jnp.asarray(it), jnp.asarray(jt), jnp.asarray(vt), jnp.asarray(fin), lam_p, q, k, v, bias,
      subln_g.reshape(1, -1))


def _block_bias(rel_bias, nq, nkb, tq):
    bpq = tq // LANES
    lo, hi = -bpq, FAR_BLOCKS
    bias = _bias_tiles(rel_bias, [-LANES * d for d in range(lo, hi + 1)], [4 * tq] * (hi - lo + 1), 1 << 30, tq, LANES)
    var = [[min(max(bpq * i - jb, lo), hi) - lo for jb in range(nkb)] for i in range(nq)]
    return bias, var


def _diff_prompt(q, k, v, rel_bias, lam_p, subln_g, lam_init):
    s = q.shape[1]
    tq = tk = 512
    nq = s // tq
    pairs = [(i, j) for i in range(nq) for j in range((i * tq) // tk + 1)]
    bias, var = _block_bias(rel_bias, s // LANES, s // LANES, LANES)
    return _diff_attn(q, k, v, bias, pairs, var, lam_p, subln_g, lam_init, tq, tk)


def _sort_key(x):
    bits = pltpu.bitcast(x, I32)
    return jnp.where(bits < 0, INT_MIN - bits, bits)


def _dsa_kernel(nvis_ref, var_ref, q_ref, qi_ref, wi_ref, ki_ref, k_ref, v_ref, bias_ref, o_ref,
                keys_s, qs_s, p_s, al_s, m_s, acc_s, *, tq, tqa, tk, nk, rc, q0, length, topk, nbits):
    i = pl.program_id(1)
    nvis = nvis_ref[i]
    npair = DSA_KV_HEADS // 2
    hpp = 2 * DSA_GROUP
    nchunk = hpp * tqa // rc

    for hd in range(DSA_HEADS):
        qs_s[hd // hpp, (hd % hpp) * tqa:(hd % hpp + 1) * tqa, :] = q_ref[0, :, hd * LANES:(hd + 1) * LANES]
    qi_stack = jnp.concatenate([qi_ref[0, :, u * LANES:(u + 1) * LANES] for u in range(IDX_HEADS // 2)], axis=0)
    wi_t = wi_ref[0].T
    wrows = [wi_t[h:h + 1, :] for h in range(IDX_HEADS)]

    k_io = lax.broadcasted_iota(I32, (tk, tq), 0)
    q_chunk = (q0 + i * tq + lax.broadcasted_iota(I32, (tk, tq), 1)) >> 6
    score_scale = IDX_DIM ** -0.5 * IDX_HEADS ** -0.5

    def score_tile(j, last):
        off = pl.multiple_of(j * tk, tk)
        ki_lo = ki_ref[0, pl.ds(off, tk), 0:LANES]
        ki_hi = ki_ref[0, pl.ds(off, tk), LANES:2 * LANES]
        s_even = lax.dot_general(ki_lo, qi_stack, _NT, preferred_element_type=F32)
        s_odd = lax.dot_general(ki_hi, qi_stack, _NT, preferred_element_type=F32)
        score = jnp.zeros((tk, tq), F32)
        for u in range(IDX_HEADS // 2):
            score = score + jnp.maximum(s_even[:, u * tq:(u + 1) * tq], 0.0) * wrows[2 * u]
            score = score + jnp.maximum(s_odd[:, u * tq:(u + 1) * tq], 0.0) * wrows[2 * u + 1]
        key = _sort_key(score * score_scale)
        if last:
            kpos = j * tk + k_io
            vis = jnp.where((kpos >> 6) <= q_chunk, 1, 0) * jnp.where(kpos < length, 1, 0)
            key = jnp.where(vis > 0, key, INT_MIN)
        keys_s[j] = key

    def score_body(jj, carry):
        score_tile(2 * jj, False)
        score_tile(2 * jj + 1, False)
        return carry

    lax.fori_loop(0, (nvis - 1) // 2, score_body, 0)

    @pl.when(nvis % 2 == 0)
    def _():
        score_tile(nvis - 2, False)

    score_tile(nvis - 1, True)

    nacc = 4

    def column_counts(hit_fns):
        nf = len(hit_fns)

        def body(j, accs):
            kk = keys_s[j]
            accs = list(accs)
            for r in range(tk // 8):
                rows = kk[8 * r:8 * r + 8]
                for f, hit_fn in enumerate(hit_fns):
                    a = f * nacc + r % nacc
                    accs[a] = accs[a] + jnp.where(hit_fn(j, r, rows), 1.0, 0.0)
            return tuple(accs)

        accs = lax.fori_loop(0, nvis, body, (jnp.zeros((8, tq), F32),) * (nacc * nf))
        return [jnp.sum(functools.reduce(lambda a, b: a + b, accs[f * nacc:(f + 1) * nacc]), axis=0, keepdims=True)
                for f in range(nf)]

    def above(cand):
        below = jnp.broadcast_to(cand - 1, (8, tq))
        return lambda j, r, kk: kk > below

    def count_ge(cand):
        return column_counts([above(cand)])[0]

    def bit_body(it, ans):
        cand = ans | jnp.left_shift(jnp.int32(1), 31 - it)
        return jnp.where(count_ge(cand ^ INT_MIN) >= topk, cand, ans)

    thr = lax.fori_loop(0, 32, bit_body, jnp.zeros((1, tq), I32)) ^ INT_MIN
    thr_sel = jnp.maximum(thr, NEG_INF_KEY + 1)
    row_full = thr >= thr_sel

    cnt_ge, cnt_gt = column_counts([above(thr_sel), above(thr_sel + 1)])
    excess = jnp.max(jnp.where(row_full, cnt_ge - topk, 0.0))

    @pl.when(excess > 0.0)
    def _():
        quota = topk - cnt_gt
        thr8 = jnp.broadcast_to(thr, (8, tq))
        k_io8 = lax.broadcasted_iota(I32, (8, tq), 0)

        def count_tied_before(pos):
            pos8 = jnp.broadcast_to(pos, (8, tq))
            return column_counts([lambda j, r, kk: jnp.where(kk == thr8, 1, 0)
                                  * jnp.where(j * tk + 8 * r + k_io8 < pos8, 1, 0) > 0])[0]

        def pos_body(it, p):
            cand = p | jnp.left_shift(jnp.int32(1), nbits - 1 - it)
            return jnp.where(count_tied_before(cand) < quota, cand, p)

        last = lax.fori_loop(0, nbits, pos_body, jnp.zeros((1, tq), I32))
        last = jnp.where(row_full, last, nk * tk)

        def demote(j, carry):
            kpos = j * tk + k_io
            kk = keys_s[j]
            drop = jnp.where(kk == thr, 1, 0) * jnp.where(kpos > last, 1, 0)
            keys_s[j] = jnp.where(drop > 0, thr - 1, kk)
            return carry

        lax.fori_loop(0, nvis, demote, 0)

    m_s[...] = jnp.full(m_s.shape, NEG, F32)
    acc_s[...] = jnp.zeros(acc_s.shape, F32)

    ncb = tk // LANES

    def attn_body(j, carry):
        off = pl.multiple_of(j * tk, tk)
        bvar = [var_ref[(i * nk + j) * ncb + cb] for cb in range(ncb)]
        maskb = jnp.where(keys_s[j] >= thr_sel, 0.0, NEG).T[0:tqa]
        s_pair = [lax.dot_general(qs_s[pr], k_ref[0, pl.ds(off, tk), pr * LANES:(pr + 1) * LANES], _NT,
                                  preferred_element_type=F32) for pr in range(npair)]
        for pr in range(npair):
            for c in range(nchunk):
                rows = slice(c * rc, (c + 1) * rc)
                hd = pr * hpp + (c * rc) // tqa
                brow = (c * rc) % tqa
                s = jnp.concatenate(
                    [s_pair[pr][rows, cb * LANES:(cb + 1) * LANES] + bias_ref[bvar[cb], hd, brow:brow + rc, :]
                     for cb in range(ncb)], axis=1) + maskb[brow:brow + rc]
                m_prev = m_s[pr, rows]
                m_new = jnp.maximum(m_prev, jnp.max(s, axis=1, keepdims=True))
                p = jnp.exp2(s - jnp.concatenate([m_new] * ncb, axis=1))
                p_s[pr, rows] = p.astype(BF16)
                al_s[pr, rows] = jnp.exp2(m_prev - m_new)
                m_s[pr, rows] = m_new
            vp = v_ref[0, pl.ds(off, tk), pr * LANES:(pr + 1) * LANES]
            v1 = jnp.concatenate([vp, jnp.ones_like(vp)], axis=1)
            alpha = al_s[pr]
            acc_s[pr] = (jnp.concatenate([alpha, alpha], axis=1) * acc_s[pr]
                         + jnp.dot(p_s[pr], v1, preferred_element_type=F32))
        return carry

    lax.fori_loop(0, nvis, attn_body, 0)

    for hd in range(DSA_HEADS):
        rows = slice((hd % hpp) * tqa, (hd % hpp + 1) * tqa)
        o = acc_s[hd // hpp, rows, 0:LANES] / acc_s[hd // hpp, rows, LANES:2 * LANES]
        o_ref[0, :, hd * LANES:(hd + 1) * LANES] = o.astype(o_ref.dtype)


def _dsa_attn(q, qi, wi, ki2, k, v, bias, nvis, var, q0, length, topk, tq, tk):
    b, sq, dq = q.shape
    lp = k.shape[1]
    nq = qi.shape[1] // tq
    tqa = sq // nq
    nk = lp // tk
    nvar = bias.shape[0]
    npair = DSA_KV_HEADS // 2
    prow = 2 * DSA_GROUP * tqa
    kernel = functools.partial(_dsa_kernel, tq=tq, tqa=tqa, tk=tk, nk=nk, rc=min(tqa, 64), q0=q0, length=length,
                               topk=topk, nbits=max(1, (lp - 1).bit_length()))
    once = dict(pipeline_mode=pl.Buffered(1))
    full = lambda bb, ii, *_: (bb, 0, 0)
    tile = lambda bb, ii, *_: (bb, ii, 0)
    return pl.pallas_call(
        kernel,
        grid_spec=pltpu.PrefetchScalarGridSpec(
            num_scalar_prefetch=2,
            grid=(b, nq),
            in_specs=[
                pl.BlockSpec((1, tqa, dq), tile),
                pl.BlockSpec((1, tq, qi.shape[2]), tile),
                pl.BlockSpec((1, tq, LANES), tile),
                pl.BlockSpec((1, lp, 2 * LANES), full, **once),
                pl.BlockSpec((1, lp, k.shape[2]), full, **once),
                pl.BlockSpec((1, lp, v.shape[2]), full, **once),
                pl.BlockSpec((nvar, BIAS_HEADS, tqa, LANES), lambda bb, ii, *_: (0, 0, 0, 0), **once),
            ],
            out_specs=pl.BlockSpec((1, tqa, dq), tile),
            scratch_shapes=[
                pltpu.VMEM((nk, tk, tq), I32),
                pltpu.VMEM((npair, prow, LANES), BF16),
                pltpu.VMEM((npair, prow, tk), BF16),
                pltpu.VMEM((npair, prow, LANES), F32),
                pltpu.VMEM((npair, prow, LANES), F32),
                pltpu.VMEM((npair, prow, 2 * LANES), F32),
            ],
        ),
        out_shape=jax.ShapeDtypeStruct((b, sq, dq), BF16),
        compiler_params=pltpu.CompilerParams(vmem_limit_bytes=VMEM_LIMIT),
        name="dsa_attn",
    )(jnp.asarray(nvis, I32), jnp.asarray(var, I32).reshape(-1), q, qi, wi, ki2, k, v, bias)


def _dsa_prompt(q, qi, wi, ki2, k, v, rel_bias):
    s = q.shape[1]
    tq, tk = 128, 512
    nq = s // tq
    nvis = [(i * tq) // tk + 1 for i in range(nq)]
    bias, var = _block_bias(rel_bias, nq, s // LANES, tq)
    return _dsa_attn(q, qi, wi, ki2, k, v, bias, nvis, var, 0, s, min(IDX_TOPK, s // 4), tq, tk)


def _route(lt_s, rb_ref, gt_s):
    rows = [lt_s[e:e + 1, :] for e in range(N_EXPERTS)]
    mx = functools.reduce(jnp.maximum, rows)
    ex = [jnp.exp(r - mx) for r in rows]
    z = functools.reduce(lambda a, b: a + b, ex)
    probs = [x / z for x in ex]
    sel = [probs[e] + rb_ref[e] for e in range(N_EXPERTS)]
    best_val, best_grp = None, None
    for gidx in range(N_GROUPS):
        a, b, c, d = sel[EXPERTS_PER_GROUP * gidx:EXPERTS_PER_GROUP * (gidx + 1)]
        hi1, lo1, hi2, lo2 = jnp.maximum(a, b), jnp.minimum(a, b), jnp.maximum(c, d), jnp.minimum(c, d)
        top2 = jnp.maximum(hi1, hi2) + jnp.maximum(jnp.minimum(hi1, hi2), jnp.maximum(lo1, lo2))
        if gidx == 0:
            best_val, best_grp = top2, jnp.zeros_like(top2, dtype=I32)
        else:
            better = top2 > best_val
            best_grp = jnp.where(better, gidx, best_grp)
            best_val = jnp.where(better, top2, best_val)
    masked = [jnp.where(best_grp == e // EXPERTS_PER_GROUP, sel[e], -jnp.inf) for e in range(N_EXPERTS)]

    def first_argmax(vals):
        bv, bi = vals[0], jnp.zeros_like(best_grp)
        for e in range(1, N_EXPERTS):
            better = vals[e] > bv
            bi = jnp.where(better, e, bi)
            bv = jnp.where(better, vals[e], bv)
        return bi

    i1 = first_argmax(masked)
    i2 = first_argmax([jnp.where(i1 == e, -jnp.inf, masked[e]) for e in range(N_EXPERTS)])
    w1 = functools.reduce(lambda a, b: a + b, [jnp.where(i1 == e, probs[e], 0.0) for e in range(N_EXPERTS)])
    w2 = functools.reduce(lambda a, b: a + b, [jnp.where(i2 == e, probs[e], 0.0) for e in range(N_EXPERTS)])
    wsum = w1 + w2
    w1, w2 = w1 / wsum, w2 / wsum
    for e in range(N_EXPERTS):
        gt_s[e:e + 1, :] = jnp.where(i1 == e, w1, 0.0) + jnp.where(i2 == e, w2, 0.0)


def _moe_kernel(rb_ref, x_ref, g_ref, shift_ref, scale_ref, gate_ref, wrh_ref, wrl_ref, wg_ref, wu_ref, wd_ref,
                gfin_ref, out_ref, hb_s, lt_s, gt_s, gates_s, acc_s, *, final_norm):
    e = pl.program_id(1)

    @pl.when(e == 0)
    def _():
        h = _norm_mod(x_ref[...], g_ref[...], shift_ref[0], scale_ref[0])
        hb = h.astype(BF16)
        hb_s[...] = hb
        hl = (h - hb.astype(F32)).astype(BF16)
        lt_s[...] = (lax.dot_general(wrh_ref[...], hb, _NT, preferred_element_type=F32)
                     + lax.dot_general(wrl_ref[...], hb, _NT, preferred_element_type=F32)
                     + lax.dot_general(wrh_ref[...], hl, _NT, preferred_element_type=F32))
        gt_s[...] = jnp.zeros(gt_s.shape, F32)
        _route(lt_s, rb_ref, gt_s)
        gates_s[...] = gt_s[...].T
        acc_s[...] = jnp.zeros(acc_s.shape, F32)

    hb = hb_s[...]
    up = jnp.dot(hb, wu_ref[0], preferred_element_type=F32)
    gt = jnp.dot(hb, wg_ref[0], preferred_element_type=F32)
    lane = lax.broadcasted_iota(I32, gates_s.shape, 1)
    gcol = jnp.sum(jnp.where(lane == e, gates_s[...], 0.0), axis=1, keepdims=True)
    a = (gt * jax.nn.sigmoid(gt) * up * gcol).astype(BF16)
    acc_s[...] += jnp.dot(a, wd_ref[0], preferred_element_type=F32)

    @pl.when(e == N_EXPERTS - 1)
    def _():
        y = x_ref[...] + _per_segment(acc_s[...], gate_ref[0], lambda v, s: v * s)
        if final_norm:
            y = _rms(y, gfin_ref[...])
        out_ref[...] = y


def _moe(x, g, shift, scale, gate, w_router, router_bias, wg_bf, wu_bf, wd_bf, g_final, final_norm, tm):
    n, d = x.shape
    nt = n // tm
    nseg = tm // CHUNK
    ff = wg_bf.shape[2]
    wrt = w_router.T
    wrh = wrt.astype(BF16)
    wrl = (wrt - wrh.astype(F32)).astype(BF16)
    seg_spec = pl.BlockSpec((1, nseg, d), lambda t, e: (t, 0, 0))
    row_spec = pl.BlockSpec((1, d), lambda t, e: (0, 0))
    return pl.pallas_call(
        functools.partial(_moe_kernel, final_norm=final_norm),
        grid=(nt, N_EXPERTS),
        in_specs=[
            pl.BlockSpec(memory_space=pltpu.SMEM),
            pl.BlockSpec((tm, d), lambda t, e: (t, 0)),
            row_spec, seg_spec, seg_spec, seg_spec,
            pl.BlockSpec((N_EXPERTS, d), lambda t, e: (0, 0)),
            pl.BlockSpec((N_EXPERTS, d), lambda t, e: (0, 0)),
            pl.BlockSpec((1, d, ff), lambda t, e: (e, 0, 0)),
            pl.BlockSpec((1, d, ff), lambda t, e: (e, 0, 0)),
            pl.BlockSpec((1, ff, d), lambda t, e: (e, 0, 0)),
            row_spec,
        ],
        out_specs=pl.BlockSpec((tm, d), lambda t, e: (t, 0)),
        out_shape=jax.ShapeDtypeStruct((n, d), F32),
        scratch_shapes=[
            pltpu.VMEM((tm, d), BF16),
            pltpu.VMEM((N_EXPERTS, tm), F32),
            pltpu.VMEM((LANES, tm), F32),
            pltpu.VMEM((tm, LANES), F32),
            pltpu.VMEM((tm, d), F32),
        ],
        compiler_params=pltpu.CompilerParams(vmem_limit_bytes=VMEM_LIMIT),
        name="moe",
    )(router_bias, x, g.reshape(1, d), shift.reshape(nt, nseg, d), scale.reshape(nt, nseg, d),
      gate.reshape(nt, nseg, d), wrh, wrl, wg_bf, wu_bf, wd_bf, g_final.reshape(1, d))


def _dsa_in_weights(w):
    d = w.shape[0]
    o1 = DSA_HEADS * DSA_HEAD_DIM
    o2 = o1 + DSA_KV_HEADS * DSA_HEAD_DIM
    o3 = o2 + DSA_KV_HEADS * DSA_HEAD_DIM
    o4 = o3 + IDX_HEADS * IDX_DIM
    o5 = o4 + IDX_DIM
    wq = w[:, :o1].reshape(d, DSA_HEADS, 1, DSA_HEAD_DIM)
    half = ((np.arange(DSA_HEADS) // DSA_GROUP) % 2).reshape(1, DSA_HEADS, 1, 1)
    wq = jnp.concatenate([jnp.where(half == 0, wq, 0.0), jnp.where(half == 1, wq, 0.0)], axis=2)
    wki = w[:, o4:o5]
    zk = jnp.zeros_like(wki)
    wwi = jnp.pad(w[:, o5:], ((0, 0), (0, LANES - IDX_HEADS)))
    return jnp.concatenate([wq.reshape(d, 2 * o1), w[:, o1:o4], wki, zk, zk, wki, wwi], axis=1).astype(BF16)


def _dsa_out_weights(w):
    d = w.shape[1]
    w4 = w.reshape(DSA_HEADS, 1, DSA_HEAD_DIM, d)
    half = ((np.arange(DSA_HEADS) // DSA_GROUP) % 2).reshape(DSA_HEADS, 1, 1, 1)
    w4 = jnp.concatenate([jnp.where(half == 0, w4, 0.0), jnp.where(half == 1, w4, 0.0)], axis=1)
    return w4.reshape(2 * DSA_HEADS * DSA_HEAD_DIM, d).astype(BF16)


def kernel(x_prompt, x_sample, cache_k_diff, cache_v_diff, cache_k_dsa, cache_v_dsa, cache_kidx_dsa,
           c_prompt, c_sample, rel_bias, w_in_diff, w_out_diff, lam_diff, subln_g_diff, w_in_dsa, w_out_dsa,
           w_ada, b_ada, g_norm_mix, g_norm_ffn, w_router, router_bias, w_gate, w_up, w_down, g_final):
    b, s, d = x_prompt.shape
    bs, ss, _ = x_sample.shape
    past = cache_k_diff.shape[2]
    depth = w_ada.shape[0]
    assert d == D_MODEL and ss == CHUNK and s % 512 == 0 and past % CHUNK == 0 and depth == 2
    l_s = past + ss
    lp_s = -(-l_s // LANES) * LANES
    nkb_s = lp_s // LANES

    streams = []
    for x3, rows in ((x_prompt, np.repeat(np.arange(b), s // CHUNK)), (x_sample, b + np.arange(bs))):
        n = x3.shape[0] * x3.shape[1]
        streams.append((x3.reshape(n, d), x3.shape[:2], rows,
                        _pick_tile(n, (512, 256, 128, 64)), _pick_tile(n, (1024, 512, 256, 128, 64))))
    mod = _ada_mod(jnp.concatenate([c_prompt, c_sample], axis=0), w_ada, b_ada)

    def mods(layer, rows):
        m = mod[layer][rows]
        return [m[:, i * d:(i + 1) * d] for i in range(6)]

    def pad_keys(x):
        return jnp.pad(x, ((0, 0), (0, lp_s - x.shape[1]), (0, 0)))

    bias_s = _bias_tiles(rel_bias, [jb * LANES - past for jb in range(nkb_s)], [past] * nkb_s, l_s, ss, LANES)
    var_s = [list(range(nkb_s))]
    xs = [st[0] for st in streams]
    outs_kv = []

    qk = DIFF_HEADS * 2 * DIFF_HEAD_DIM
    lam_init = 0.8 - 0.6 * float(np.exp(-0.3 * 0))
    q_scale = DIFF_HEAD_DIM ** -0.5 * LOG2E
    w_in0, w_out0 = w_in_diff[0].astype(BF16), w_out_diff[0].astype(BF16)
    wt_k0 = w_in_diff[0][:, qk:2 * qk].T.astype(BF16)
    moe_w0 = (w_gate[0].astype(BF16), w_up[0].astype(BF16), w_down[0].astype(BF16))
    for si, (_, (nb, ns), rows, tm, tm_moe) in enumerate(streams):
        m = mods(0, rows)
        if si == 0:
            cols = ((0, qk, ((0, 0, q_scale),)), (qk, qk, ((2, 0, 1.0),)), (2 * qk, qk, ((1, 0, 1.0), (3, 0, 1.0))))
            outs = ((qk, BF16), (qk, F32), (qk, BF16), (qk, BF16))
            q_bf, v_f, k_bf, v_bf, k_t = _in_proj(xs[si], g_norm_mix[0], m[0], m[1], w_in0, cols, outs, tm,
                                                  wt_k0, ((0, qk, F32),), ns)
            k_f = jnp.transpose(k_t.reshape(nb, DIFF_HEADS, 2, DIFF_HEAD_DIM, ns), (0, 4, 1, 2, 3))
        else:
            cols = ((0, qk, ((0, 0, q_scale),)), (qk, qk, ((1, 0, 1.0), (3, 0, 1.0))),
                    (2 * qk, qk, ((2, 0, 1.0), (4, 0, 1.0))))
            outs = ((qk, BF16), (qk, F32), (qk, F32), (qk, BF16), (qk, BF16))
            q_bf, k_f, v_f, k_bf, v_bf = _in_proj(xs[si], g_norm_mix[0], m[0], m[1], w_in0, cols, outs, tm)
        outs_kv.append((k_f, v_f))
        q3, k3, v3 = (a.reshape(nb, ns, qk) for a in (q_bf, k_bf, v_bf))
        if si == 0:
            o = _diff_prompt(q3, k3, v3, rel_bias, lam_diff[0], subln_g_diff[0], lam_init)
        else:
            ka = pad_keys(jnp.concatenate([cache_k_diff[0].reshape(nb, past, qk).astype(BF16), k3], axis=1))
            va = pad_keys(jnp.concatenate([cache_v_diff[0].reshape(nb, past, qk).astype(BF16), v3], axis=1))
            o = _diff_attn(q3, ka, va, bias_s, [(0, 0)], var_s, lam_diff[0], subln_g_diff[0], lam_init, ns, lp_s)
        x = _out_proj(o.reshape(nb * ns, qk), xs[si], m[2], w_out0, tm)
        xs[si] = _moe(x, g_norm_ffn[0], m[3], m[4], m[5], w_router, router_bias, *moe_w0, g_final, False, tm_moe)

    nq_w = 2 * DSA_HEADS * DSA_HEAD_DIM
    kv_w = DSA_KV_HEADS * DSA_HEAD_DIM
    qi_w = IDX_HEADS * IDX_DIM
    c_k, c_v, c_qi = nq_w, nq_w + kv_w, nq_w + 2 * kv_w
    c_ki, c_wi = c_qi + qi_w, c_qi + qi_w + 2 * LANES
    half_q = nq_w // 2
    q_scale = DSA_HEAD_DIM ** -0.5 * LOG2E
    cols_q = ((0, half_q, ((0, 0, q_scale),)), (half_q, half_q, ((0, half_q, q_scale),)))
    w_in1, w_out1 = _dsa_in_weights(w_in_dsa[0]), _dsa_out_weights(w_out_dsa[0])
    o_k = DSA_HEADS * DSA_HEAD_DIM
    o_v, o_ki = o_k + kv_w, o_k + 2 * kv_w + qi_w
    wt_1 = jnp.concatenate([w_in_dsa[0][:, o_k:o_v], w_in_dsa[0][:, o_v:o_v + kv_w],
                            w_in_dsa[0][:, o_ki:o_ki + IDX_DIM]], axis=1).T.astype(BF16)
    moe_w1 = (w_gate[1].astype(BF16), w_up[1].astype(BF16), w_down[1].astype(BF16))
    for si, (_, (nb, ns), rows, tm, tm_moe) in enumerate(streams):
        m = mods(1, rows)
        if si == 0:
            cols = cols_q + ((c_k, kv_w, ((1, 0, 1.0),)), (c_v, kv_w, ((2, 0, 1.0),)), (c_qi, qi_w, ((3, 0, 1.0),)),
                             (c_ki, 2 * LANES, ((4, 0, 1.0),)), (c_wi, LANES, ((5, 0, 1.0),)))
            outs = ((nq_w, BF16), (kv_w, BF16), (kv_w, BF16), (qi_w, BF16), (2 * LANES, BF16), (LANES, F32))
            q2, k_bf, v_bf, qi_bf, ki2_bf, wi_f, k_t, v_t, ki_t = _in_proj(
                xs[si], g_norm_mix[1], m[0], m[1], w_in1, cols, outs, tm, wt_1,
                ((0, kv_w, F32), (kv_w, kv_w, F32), (2 * kv_w, IDX_DIM, F32)), ns)
            k_f = jnp.transpose(k_t.reshape(nb, DSA_KV_HEADS, DSA_HEAD_DIM, ns), (0, 3, 1, 2))
            v_f = jnp.transpose(v_t.reshape(nb, DSA_KV_HEADS, DSA_HEAD_DIM, ns), (0, 3, 1, 2))
            ki_f = jnp.transpose(ki_t, (0, 2, 1))
        else:
            cols = cols_q + ((c_k, kv_w, ((1, 0, 1.0), (3, 0, 1.0))), (c_v, kv_w, ((2, 0, 1.0), (4, 0, 1.0))),
                             (c_qi, qi_w, ((5, 0, 1.0),)), (c_ki, 2 * LANES, ((7, 0, 1.0),)),
                             (c_ki, IDX_DIM, ((6, 0, 1.0),)), (c_wi, LANES, ((8, 0, 1.0),)))
            outs = ((nq_w, BF16), (kv_w, F32), (kv_w, F32), (kv_w, BF16), (kv_w, BF16), (qi_w, BF16),
                    (IDX_DIM, F32), (2 * LANES, BF16), (LANES, F32))
            q2, k_f, v_f, k_bf, v_bf, qi_bf, ki_f, ki2_bf, wi_f = _in_proj(
                xs[si], g_norm_mix[1], m[0], m[1], w_in1, cols, outs, tm)
        outs_kv.append((k_f, v_f, ki_f))
        q3, qi3, wi3 = q2.reshape(nb, ns, nq_w), qi_bf.reshape(nb, ns, qi_w), wi_f.reshape(nb, ns, LANES)
        k3, v3, ki3 = k_bf.reshape(nb, ns, kv_w), v_bf.reshape(nb, ns, kv_w), ki2_bf.reshape(nb, ns, 2 * LANES)
        if si == 0:
            o = _dsa_prompt(q3, qi3, wi3, ki3, k3, v3, rel_bias)
        else:
            cki = cache_kidx_dsa[0].astype(BF16)
            zki = jnp.zeros_like(cki)
            ka = pad_keys(jnp.concatenate([cache_k_dsa[0].reshape(nb, past, kv_w).astype(BF16), k3], axis=1))
            va = pad_keys(jnp.concatenate([cache_v_dsa[0].reshape(nb, past, kv_w).astype(BF16), v3], axis=1))
            kia = pad_keys(jnp.concatenate([jnp.concatenate([cki, zki, zki, cki], axis=2), ki3], axis=1))
            pad_q = lambda a: jnp.pad(a, ((0, 0), (0, LANES - ns), (0, 0)))
            tk_s = max(t for t in (512, 384, 256, 128) if lp_s % t == 0)
            o = _dsa_attn(q3, pad_q(qi3), pad_q(wi3), kia, ka, va, bias_s, [lp_s // tk_s], var_s, past, l_s,
                          min(IDX_TOPK, l_s // 4), LANES, tk_s)
        x = _out_proj(o.reshape(nb * ns, nq_w), xs[si], m[2], w_out1, tm)
        xs[si] = _moe(x, g_norm_ffn[1], m[3], m[4], m[5], w_router, router_bias, *moe_w1, g_final, True, tm_moe)

    h2 = (DIFF_HEADS, 2, DIFF_HEAD_DIM)
    hv = (DIFF_HEADS, 2 * DIFF_HEAD_DIM)
    hk = (DSA_KV_HEADS, DSA_HEAD_DIM)
    (kd_p, vd_p), (kd_s, vd_s), (ks_p, vs_p, ki_p), (ks_s, vs_s, ki_s) = outs_kv
    return (xs[0].reshape(b, s, d), xs[1].reshape(bs, ss, d),
            kd_p.reshape(1, b, s, *h2), vd_p.reshape(1, b, s, *hv),
            ks_p.reshape(1, b, s, *hk), vs_p.reshape(1, b, s, *hk), ki_p.reshape(1, b, s, IDX_DIM),
            kd_s.reshape(1, bs, ss, *h2), vd_s.reshape(1, bs, ss, *hv),
            ks_s.reshape(1, bs, ss, *hk), vs_s.reshape(1, bs, ss, *hk), ki_s.reshape(1, bs, ss, IDX_DIM))
```
